```python
import functools
import jax, jax.numpy as jnp
from jax import lax
import numpy as np

D_MODEL = 1024
BATCH = 8
SEQ = 2048
DEPTH = 2
DEC_BATCH = 32
DEC_SEQ = 8
PAST_LEN = 8192
PAGE_SIZE = 128

HEAD_DIM = 64
CONV_WIDTH = D_MODEL // 4
CONV_GROUPS = CONV_WIDTH // HEAD_DIM
FOX_HEADS = (3 * D_MODEL // 8) // HEAD_DIM
MOBA_HEADS = (3 * D_MODEL // 8) // HEAD_DIM
FOX_WIDTH = FOX_HEADS * HEAD_DIM
MOBA_WIDTH = MOBA_HEADS * HEAD_DIM
MIX_WIDTH = CONV_WIDTH + FOX_WIDTH + MOBA_WIDTH
SHORT_CONV_W = 3
FFN_CONV_W = 3
D_FF = 2816
MOBA_BLOCK = 256
MOBA_TOPK = 3
Q_BLOCK = 128
LN_EPS = 1e-5
DN_ALPHA = (2 * DEPTH) ** 0.25
DN_BETA = (8 * DEPTH) ** -0.25
ATTN_SCALE = HEAD_DIM ** -0.5
IN_SIZES = (CONV_WIDTH, CONV_WIDTH, CONV_WIDTH,
            FOX_WIDTH, FOX_WIDTH, FOX_WIDTH, FOX_HEADS,
            MOBA_WIDTH, MOBA_WIDTH, MOBA_WIDTH)
N_IN = sum(IN_SIZES)

kernel_name = "hymba_conv_fox_moba_decoder_step"


def layer_norm(x, g, b):
    xf = x.astype(jnp.float32)
    mu = jnp.mean(xf, axis=-1, keepdims=True)
    var = jnp.mean(jnp.square(xf - mu), axis=-1, keepdims=True)
    y = (xf - mu) * lax.rsqrt(var + LN_EPS) * g.astype(jnp.float32) + b.astype(jnp.float32)
    return y.astype(x.dtype)


def causal_dwconv(u, prev, w):
    width = w.shape[0]
    t = u.shape[1]
    ext = jnp.concatenate([prev.astype(u.dtype), u], axis=1)
    out = ext[:, 0:t] * w[0]
    for j in range(1, width):
        out = out + ext[:, j:j + t] * w[j]
    return out, ext[:, t:]


def split_cols(p):
    parts, off = [], 0
    for n in IN_SIZES:
        parts.append(p[..., off:off + n])
        off += n
    return parts


def gather_pages(pool, page_table):
    g = pool[page_table]
    return g.reshape((g.shape[0], g.shape[1] * g.shape[2]) + g.shape[3:])


def alibi_slopes(n_heads):
    return jnp.asarray(2.0 ** (-8.0 * np.arange(1, n_heads + 1) / n_heads), jnp.float32)


def fox_logits(q, k, fq, fk, tq, sk):
    s = jnp.einsum("bqhd,bshd->bhqs", q, k).astype(jnp.float32) * ATTN_SCALE
    s = s + jnp.transpose(fq, (0, 2, 1))[:, :, :, None] - jnp.transpose(fk, (0, 2, 1))[:, :, None, :]
    return jnp.where(sk[None, :] <= tq[:, None], s, -jnp.inf)


def fox_prompt(q, k, v, logf):
    g, t, h, d = q.shape
    cum = jnp.cumsum(logf.astype(jnp.float32), axis=1)
    pos = jnp.arange(t)

    def one_block(i):
        st = i * Q_BLOCK
        qb = lax.dynamic_slice_in_dim(q, st, Q_BLOCK, axis=1)
        fb = lax.dynamic_slice_in_dim(cum, st, Q_BLOCK, axis=1)
        s = fox_logits(qb, k, fb, cum, st + jnp.arange(Q_BLOCK), pos)
        p = jax.nn.softmax(s, axis=-1)
        return jnp.einsum("bhqs,bshd->bqhd", p.astype(v.dtype), v)

    out = lax.map(one_block, jnp.arange(t // Q_BLOCK))
    return jnp.transpose(out, (1, 0, 2, 3, 4)).reshape(g, t, h * d)


def fox_sample(q, k, v, logf, k_past, v_past, logf_past):
    g, t, h, d = q.shape
    p_len = k_past.shape[1]
    cum_p = jnp.cumsum(logf_past.astype(jnp.float32), axis=1)
    cum_n = cum_p[:, -1:] + jnp.cumsum(logf.astype(jnp.float32), axis=1)
    tq = p_len + jnp.arange(t)
    s_past = fox_logits(q, k_past.astype(q.dtype), cum_n, cum_p, tq, jnp.arange(p_len))
    s_new = fox_logits(q, k, cum_n, cum_n, tq, tq)
    p = jax.nn.softmax(jnp.concatenate([s_past, s_new], axis=-1), axis=-1).astype(v.dtype)
    out = (jnp.einsum("bhqs,bshd->bqhd", p[..., :p_len], v_past.astype(v.dtype))
           + jnp.einsum("bhqs,bshd->bqhd", p[..., p_len:], v))
    return out.reshape(g, t, h * d)


def moba_attend_seq(q, tq, k_seq, v_seq):
    l, h, d = k_seq.shape
    nb = l // MOBA_BLOCK
    kb = k_seq.reshape(nb, MOBA_BLOCK, h, d).transpose(2, 0, 1, 3)
    vb = v_seq.reshape(nb, MOBA_BLOCK, h, d).transpose(2, 0, 1, 3)
    means = jnp.mean(kb.astype(jnp.float32), axis=2)
    own = tq // MOBA_BLOCK
    gate = jnp.einsum("qhd,hnd->qhn", q.astype(jnp.float32), means)
    gate = jnp.where(jnp.arange(nb)[None, None, :] < own[:, None, None], gate, -jnp.inf)
    n_sel = min(MOBA_TOPK, nb)
    _, top = lax.top_k(gate, n_sel)
    tq_n = tq.shape[0]
    own_b = jnp.broadcast_to(own[:, None, None], (tq_n, h, 1)).astype(top.dtype)
    blocks = jnp.concatenate([top, own_b], axis=-1)
    sel_ok = jnp.arange(n_sel)[None, None, :] < own[:, None, None]
    ok = jnp.concatenate([jnp.broadcast_to(sel_ok, (tq_n, h, n_sel)),
                          jnp.ones((tq_n, h, 1), dtype=bool)], axis=-1)
    hidx = jnp.arange(h)[None, :, None]
    kg = kb[hidx, blocks]
    vg = vb[hidx, blocks]
    s = jnp.einsum("qhd,qhnkd->qhnk", q, kg).astype(jnp.float32) * ATTN_SCALE
    dist = tq[:, None, None, None] - (blocks[..., None] * MOBA_BLOCK + jnp.arange(MOBA_BLOCK))
    s = s - alibi_slopes(h)[None, :, None, None] * dist.astype(jnp.float32)
    s = jnp.where(ok[..., None] & (dist >= 0), s, -jnp.inf)
    p = jax.nn.softmax(s.reshape(tq_n, h, -1), axis=-1).reshape(s.shape)
    out = jnp.einsum("qhnk,qhnkd->qhd", p.astype(v_seq.dtype), vg)
    return out.reshape(tq_n, h * d)


def moba_prompt(q, k, v):
    g, t, h, d = q.shape
    lp = -(-t // MOBA_BLOCK) * MOBA_BLOCK
    pad = ((0, 0), (0, lp - t), (0, 0), (0, 0))
    kp, vp = jnp.pad(k, pad), jnp.pad(v, pad)
    nqb = t // Q_BLOCK
    qb = q.reshape(g * nqb, Q_BLOCK, h, d)
    seq_id = jnp.repeat(jnp.arange(g), nqb)
    start = jnp.tile(jnp.arange(nqb) * Q_BLOCK, g)

    def one_block(args):
        qi, si, st = args
        return moba_attend_seq(qi, st + jnp.arange(Q_BLOCK), kp[si], vp[si])

    out = lax.map(one_block, (qb, seq_id, start))
    return out.reshape(g, t, h * d)


def moba_sample(q, k, v, k_past, v_past):
    g, t, h, d = q.shape
    p_len = k_past.shape[1]
    total = p_len + t
    lp = -(-total // MOBA_BLOCK) * MOBA_BLOCK
    pad = ((0, 0), (0, lp - total), (0, 0), (0, 0))
    k_all = jnp.pad(jnp.concatenate([k_past.astype(k.dtype), k], axis=1), pad)
    v_all = jnp.pad(jnp.concatenate([v_past.astype(v.dtype), v], axis=1), pad)
    tq = p_len + jnp.arange(t)
    return jax.vmap(lambda qi, ki, vi: moba_attend_seq(qi, tq, ki, vi))(q, k_all, v_all)


def trunk_layer(x, c, conv_prev, ffn_prev, fox_fn, moba_fn,
                w_ada, b_ada, w_in, b_f, conv_w, w_o, ln1_g, ln1_b,
                w_up, ffn_conv_w, ffn_conv_b, w_down, ln2_g, ln2_b):
    g, t, _ = x.shape
    ada = jax.nn.silu(c) @ w_ada + b_ada
    sh_a, sc_a, gt_a, sh_f, sc_f, gt_f = [a[:, None, :] for a in jnp.split(ada, 6, axis=-1)]
    h = x * (1 + sc_a) + sh_a
    cb, cc, cx, fq, fk, fv, ff, mq, mk, mv = split_cols(h @ w_in)
    conv_out, conv_state = causal_dwconv(cc * cx, conv_prev, conv_w)
    y_conv = cb * conv_out

    def heads(a):
        return a.reshape(g, t, -1, HEAD_DIM)

    logf = jax.nn.log_sigmoid((ff + b_f).astype(jnp.float32))
    y_fox = fox_fn(heads(fq), heads(fk), heads(fv), logf)
    y_moba = moba_fn(heads(mq), heads(mk), heads(mv))
    mixed = jnp.concatenate([y_conv, y_fox.astype(x.dtype), y_moba.astype(x.dtype)], axis=-1) @ w_o
    x = layer_norm(DN_ALPHA * x + (1 + gt_a) * mixed, ln1_g, ln1_b)
    h = x * (1 + sc_f) + sh_f
    up, ffn_state = causal_dwconv(h @ w_up, ffn_prev, ffn_conv_w)
    gate, val = jnp.split(up + ffn_conv_b, 2, axis=-1)
    y_ffn = (jax.nn.silu(gate) * val) @ w_down
    x = layer_norm(DN_ALPHA * x + (1 + gt_f) * y_ffn, ln2_g, ln2_b)
    new_state = (heads(fk), heads(fv), logf.astype(x.dtype), heads(mk), heads(mv), conv_state, ffn_state)
    return x, new_state


def setup_inputs(seed: int = 0) -> dict:
    key = jax.random.key(seed)
    ks = jax.random.split(key, 32)
    n_pages = PAST_LEN // PAGE_SIZE
    n_used = DEC_BATCH * n_pages
    n_pool = n_used + max(1, n_used // 4)

    def nrm(k, shape, s=1.0):
        return s * jax.random.normal(k, shape, jnp.float32)

    x_prompt = nrm(ks[0], (BATCH, SEQ, D_MODEL))
    x_sample = nrm(ks[1], (DEC_BATCH, DEC_SEQ, D_MODEL))
    cache_fox_k = nrm(ks[2], (DEPTH, n_pool, PAGE_SIZE, FOX_HEADS, HEAD_DIM))
    cache_fox_v = nrm(ks[3], (DEPTH, n_pool, PAGE_SIZE, FOX_HEADS, HEAD_DIM))
    cache_fox_logf = jax.nn.log_sigmoid(3.0 + nrm(ks[4], (DEPTH, n_pool, PAGE_SIZE, FOX_HEADS)))
    cache_moba_k = nrm(ks[5], (DEPTH, n_pool, PAGE_SIZE, MOBA_HEADS, HEAD_DIM))
    cache_moba_v = nrm(ks[6], (DEPTH, n_pool, PAGE_SIZE, MOBA_HEADS, HEAD_DIM))
    state_conv = nrm(ks[7], (DEPTH, DEC_BATCH, SHORT_CONV_W - 1, CONV_WIDTH))
    state_ffn = nrm(ks[8], (DEPTH, DEC_BATCH, FFN_CONV_W - 1, 2 * D_FF))
    page_table = jax.random.permutation(ks[9], n_pool)[:n_used].reshape(DEC_BATCH, n_pages).astype(jnp.int32)
    c_prompt = nrm(ks[10], (BATCH, D_MODEL))
    c_sample = nrm(ks[11], (DEC_BATCH, D_MODEL))
    w_ada = nrm(ks[12], (DEPTH, D_MODEL, 6 * D_MODEL), 0.2 * D_MODEL ** -0.5)
    b_ada = nrm(ks[13], (DEPTH, 6 * D_MODEL), 0.02)
    w_in = nrm(ks[14], (DEPTH, D_MODEL, N_IN), D_MODEL ** -0.5)
    b_f = 3.0 + nrm(ks[15], (DEPTH, FOX_HEADS), 0.5)
    conv_w = nrm(ks[16], (DEPTH, SHORT_CONV_W, CONV_WIDTH), SHORT_CONV_W ** -0.5)
    w_o = nrm(ks[17], (DEPTH, MIX_WIDTH, D_MODEL), DN_BETA * MIX_WIDTH ** -0.5)
    ln1_g = 1.0 + nrm(ks[18], (DEPTH, D_MODEL), 0.05)
    ln1_b = nrm(ks[19], (DEPTH, D_MODEL), 0.02)
    w_up = nrm(ks[20], (DEPTH, D_MODEL, 2 * D_FF), D_MODEL ** -0.5)
    ffn_conv_w = nrm(ks[21], (DEPTH, FFN_CONV_W, 2 * D_FF), FFN_CONV_W ** -0.5)
    ffn_conv_b = nrm(ks[22], (DEPTH, 2 * D_FF), 0.02)
    w_down = nrm(ks[23], (DEPTH, D_FF, D_MODEL), DN_BETA * D_FF ** -0.5)
    ln2_g = 1.0 + nrm(ks[24], (DEPTH, D_MODEL), 0.05)
    ln2_b = nrm(ks[25], (DEPTH, D_MODEL), 0.02)
    return {"x_prompt": x_prompt, "x_sample": x_sample,
            "cache_fox_k": cache_fox_k, "cache_fox_v": cache_fox_v, "cache_fox_logf": cache_fox_logf,
            "cache_moba_k": cache_moba_k, "cache_moba_v": cache_moba_v,
            "state_conv": state_conv, "state_ffn": state_ffn, "page_table": page_table,
            "c_prompt": c_prompt, "c_sample": c_sample,
            "w_ada": w_ada, "b_ada": b_ada, "w_in": w_in, "b_f": b_f, "conv_w": conv_w, "w_o": w_o,
            "ln1_g": ln1_g, "ln1_b": ln1_b, "w_up": w_up, "ffn_conv_w": ffn_conv_w,
            "ffn_conv_b": ffn_conv_b, "w_down": w_down, "ln2_g": ln2_g, "ln2_b": ln2_b}


def reference(x_prompt, x_sample, cache_fox_k, cache_fox_v, cache_fox_logf, cache_moba_k, cache_moba_v,
              state_conv, state_ffn, page_table, c_prompt, c_sample,
              w_ada, b_ada, w_in, b_f, conv_w, w_o, ln1_g, ln1_b,
              w_up, ffn_conv_w, ffn_conv_b, w_down, ln2_g, ln2_b):
    xp, xs = x_prompt, x_sample
    states_p, states_s = [], []
    for l in range(DEPTH):
        params = (w_ada[l], b_ada[l], w_in[l], b_f[l], conv_w[l], w_o[l], ln1_g[l], ln1_b[l],
                  w_up[l], ffn_conv_w[l], ffn_conv_b[l], w_down[l], ln2_g[l], ln2_b[l])
        zc = jnp.zeros((xp.shape[0], SHORT_CONV_W - 1, CONV_WIDTH), xp.dtype)
        zf = jnp.zeros((xp.shape[0], FFN_CONV_W - 1, 2 * D_FF), xp.dtype)
        xp, st_p = trunk_layer(xp, c_prompt, zc, zf, fox_prompt, moba_prompt, *params)
        states_p.append(st_p)
        fox_fn = functools.partial(fox_sample,
                                   k_past=gather_pages(cache_fox_k[l], page_table),
                                   v_past=gather_pages(cache_fox_v[l], page_table),
                                   logf_past=gather_pages(cache_fox_logf[l], page_table))
        moba_fn = functools.partial(moba_sample,
                                    k_past=gather_pages(cache_moba_k[l], page_table),
                                    v_past=gather_pages(cache_moba_v[l], page_table))
        xs, st_s = trunk_layer(xs, c_sample, state_conv[l], state_ffn[l], fox_fn, moba_fn, *params)
        states_s.append(st_s)
    fk_p, fv_p, flf_p, mk_p, mv_p, conv_p, ffn_p = [jnp.stack([s[i] for s in states_p]) for i in range(7)]
    fk_s, fv_s, flf_s, mk_s, mv_s, conv_s, ffn_s = [jnp.stack([s[i] for s in states_s]) for i in range(7)]
    return (xp, xs, fk_p, fv_p, flf_p, mk_p, mv_p, conv_p, ffn_p,
            fk_s, fv_s, flf_s, mk_s, mv_s, conv_s, ffn_s)
```

```python
import functools

import jax
import jax.numpy as jnp
from jax import lax
from jax.experimental import pallas as pl
from jax.experimental.pallas import tpu as pltpu

F32 = jnp.float32
BF16 = jnp.bfloat16
HIGHEST = lax.Precision.HIGHEST

HEAD_DIM = 64
N_HEADS = 6
HEADS_W = N_HEADS * HEAD_DIM
CONV_W = 256
SHORT_W = 3
MOBA_BLOCK = 256
MOBA_TOPK = 3
PAGE = 128
LN_EPS = 1e-5
ATTN_SCALE = HEAD_DIM ** -0.5
ALIBI_SLOPES = tuple(2.0 ** (-8.0 * (h + 1) / N_HEADS) for h in range(N_HEADS))
MASKED = -1e30

V7X_LANES = 128
V7X_SUBLANES = 8
V7X_VMEM_BYTES = 64 * 1024 * 1024
VMEM_LIMIT = 52 * 1024 * 1024
FF_CHUNK = 256
PAGES_PER_STEP = 8


def _dot(a, b):
    return jnp.dot(a, b, preferred_element_type=F32)


def _dot_nt(a, b):
    return lax.dot_general(a, b, (((1,), (1,)), ((), ())), preferred_element_type=F32)


def _dot_exact(a, b):
    return jnp.dot(a, b, precision=HIGHEST, preferred_element_type=F32)


def _sigmoid(x):
    return 1.0 / (1.0 + jnp.exp(-x))


def _log_sigmoid(x):
    return jnp.minimum(x, 0.0) - jnp.log(1.0 + jnp.exp(-jnp.abs(x)))


def _layer_norm(z, g, b):
    mu = jnp.mean(z, axis=-1, keepdims=True)
    zc = z - mu
    var = jnp.mean(zc * zc, axis=-1, keepdims=True)
    return zc * lax.rsqrt(var + LN_EPS) * g + b


def _params(*semantics):
    return pltpu.CompilerParams(dimension_semantics=semantics, vmem_limit_bytes=VMEM_LIMIT)


def _const_spec(shape):
    zeros = (0,) * len(shape)
    return pl.BlockSpec(shape, lambda *_: zeros)


def _mod_spec(arr, tm):
    d = arr.shape[-1]
    if arr.shape[1] == 1:
        return pl.BlockSpec((1, 1, d), lambda g, i: (g, 0, 0))
    return pl.BlockSpec((1, tm, d), lambda g, i: (g, i, 0))


def _ada_kernel(c_ref, w_ref, b_ref, o_ref):
    c = c_ref[...]
    o_ref[0] = _dot_exact(c * _sigmoid(c), w_ref[0]) + b_ref[0]


def _ada(c_all, w_ada, b_ada):
    depth, d, n = w_ada.shape
    g = c_all.shape[0]
    tn = 1536
    return pl.pallas_call(
        _ada_kernel,
        out_shape=jax.ShapeDtypeStruct((depth, g, n), F32),
        grid=(depth, n // tn),
        in_specs=[pl.BlockSpec((g, d), lambda l, j: (0, 0)),
                  pl.BlockSpec((1, d, tn), lambda l, j: (l, 0, j)),
                  pl.BlockSpec((1, 1, tn), lambda l, j: (l, 0, j))],
        out_specs=pl.BlockSpec((1, g, tn), lambda l, j: (l, 0, j)),
        compiler_params=_params("arbitrary", "arbitrary"),
        name="ada",
    )(c_all, w_ada, b_ada.reshape(depth, 1, n))


def _inproj_kernel(x_ref, sc_ref, sh_ref, wtok_ref, wkvt_ref, wfft_ref, bf_ref,
                   conv_ref, fq_ref, mq_ref, fkt_ref, fvt_ref, mkt_ref, mvt_ref, lft_ref):
    h = (x_ref[0] * (1.0 + sc_ref[0]) + sh_ref[0]).astype(BF16)
    conv_ref[0] = _dot(h, wtok_ref[:, 0:3 * CONV_W])
    qq = _dot(h, wtok_ref[:, 3 * CONV_W:])
    fq_ref[0] = qq[:, :HEADS_W]
    mq_ref[0] = qq[:, HEADS_W:]
    for c, ref in enumerate((fkt_ref, fvt_ref, mkt_ref, mvt_ref)):
        t = _dot_nt(wkvt_ref[c * HEADS_W:(c + 1) * HEADS_W, :], h)
        ref[0] = t.reshape(N_HEADS, HEAD_DIM, t.shape[-1])
    ff = _dot_nt(wfft_ref[...], h)[:V7X_SUBLANES] + bf_ref[...]
    lft_ref[0] = _log_sigmoid(ff)


def _inproj(x3, sc, sh, wtok, wkvt, wfft, bf, tm):
    gb, t, d = x3.shape
    tok = lambda w: pl.BlockSpec((1, tm, w), lambda g, i: (g, i, 0))
    kvt = pl.BlockSpec((1, N_HEADS, HEAD_DIM, tm), lambda g, i: (g, 0, 0, i))
    kvt_shape = jax.ShapeDtypeStruct((gb, N_HEADS, HEAD_DIM, t), F32)
    return pl.pallas_call(
        _inproj_kernel,
        out_shape=(jax.ShapeDtypeStruct((gb, t, 3 * CONV_W), F32),
                   jax.ShapeDtypeStruct((gb, t, HEADS_W), F32),
                   jax.ShapeDtypeStruct((gb, t, HEADS_W), F32),
                   kvt_shape, kvt_shape, kvt_shape, kvt_shape,
                   jax.ShapeDtypeStruct((gb, V7X_SUBLANES, t), F32)),
        grid=(gb, t // tm),
        in_specs=[tok(d), _mod_spec(sc, tm), _mod_spec(sh, tm),
                  _const_spec(wtok.shape), _const_spec(wkvt.shape), _const_spec(wfft.shape),
                  _const_spec(bf.shape)],
        out_specs=(tok(3 * CONV_W), tok(HEADS_W), tok(HEADS_W), kvt, kvt, kvt, kvt,
                   pl.BlockSpec((1, V7X_SUBLANES, tm), lambda g, i: (g, 0, i))),
        compiler_params=_params("arbitrary", "arbitrary"),
        name="inproj",
    )(x3, sc, sh, wtok, wkvt, wfft, bf)


def _online_softmax_step(s, m, l, acc, vt):
    m_new = jnp.maximum(m, jnp.max(s, axis=1, keepdims=True))
    alpha = jnp.exp(m - m_new)
    p = jnp.exp(s - m_new)
    l = alpha * l + jnp.sum(p, axis=1, keepdims=True)
    acc = alpha * acc + _dot_nt(p.astype(BF16), vt)
    return m_new, l, acc


def _softmax_init(rows):
    return (jnp.full((rows, 1), -jnp.inf, F32), jnp.zeros((rows, 1), F32),
            jnp.zeros((rows, HEAD_DIM), F32))


def _fox_prompt_kernel(q_ref, kt_ref, vt_ref, lft_ref, o_ref, kb_ref, vb_ref, cum_ref, *, tq, nkt):
    i = pl.program_id(1)
    rows = lax.broadcasted_iota(jnp.int32, (tq, tq), 0)
    cols = lax.broadcasted_iota(jnp.int32, (tq, tq), 1)

    @pl.when(i == 0)
    def _prepare_sequence():
        upper = (rows <= cols).astype(F32)
        carry = jnp.zeros((V7X_SUBLANES, 1), F32)
        for c in range(nkt):
            sl = slice(c * tq, (c + 1) * tq)
            kb_ref[c] = kt_ref[0, :, :, sl].astype(BF16)
            vb_ref[c] = vt_ref[0, :, :, sl].astype(BF16)
            cum = _dot_exact(lft_ref[0, :, sl], upper) + carry
            cum_ref[c] = cum
            carry = cum[:, tq - 1:tq]

    q = q_ref[0] * ATTN_SCALE
    causal = cols <= rows
    heads = []
    for h in range(N_HEADS):
        qh = q[:, h * HEAD_DIM:(h + 1) * HEAD_DIM].astype(BF16)

        def tile(j, carry, diag, h=h, qh=qh):
            s = _dot(qh, kb_ref[j, h]) - cum_ref[j, pl.ds(h, 1), :]
            if diag:
                s = jnp.where(causal, s, -jnp.inf)
            return _online_softmax_step(s, *carry, vb_ref[j, h])

        carry = lax.fori_loop(0, i, lambda j, c: tile(j, c, False), _softmax_init(tq))
        _, l, acc = tile(i, carry, True)
        heads.append(acc / l)
    o_ref[0] = jnp.concatenate(heads, axis=1).astype(o_ref.dtype)


def _moba_prompt_kernel(q_ref, kt_ref, vt_ref, o_ref, kb_ref, vb_ref, means_ref, *, nb):
    blk = MOBA_BLOCK
    i = pl.program_id(1)

    @pl.when(i == 0)
    def _prepare_sequence():
        lane = lax.broadcasted_iota(jnp.int32, (N_HEADS, HEAD_DIM, V7X_LANES), 2)
        means = jnp.zeros((N_HEADS, HEAD_DIM, V7X_LANES), F32)
        for c in range(nb):
            kc = kt_ref[0, :, :, c * blk:(c + 1) * blk]
            kb_ref[c] = kc.astype(BF16)
            vb_ref[c] = vt_ref[0, :, :, c * blk:(c + 1) * blk].astype(BF16)
            means = jnp.where(lane == c, jnp.sum(kc, axis=-1, keepdims=True) * (1.0 / blk), means)
        means_ref[...] = means

    q = q_ref[0]
    lane = lax.broadcasted_iota(jnp.int32, (blk, V7X_LANES), 1)
    lane_f = lane.astype(F32)
    rows = lax.broadcasted_iota(jnp.int32, (blk, blk), 0)
    cols = lax.broadcasted_iota(jnp.int32, (blk, blk), 1)
    causal = cols <= rows
    kcol = lax.broadcasted_iota(jnp.int32, (1, blk), 1).astype(F32)
    heads = []
    for h in range(N_HEADS):
        qh32 = q[:, h * HEAD_DIM:(h + 1) * HEAD_DIM]
        gate = jnp.where(lane < i, _dot_exact(qh32, means_ref[h]), -jnp.inf)
        bias = jnp.full(gate.shape, MASKED, F32)
        for _ in range(MOBA_TOPK):
            best = jnp.max(gate, axis=1, keepdims=True)
            first = jnp.min(jnp.where(gate == best, lane_f, float(V7X_LANES)), axis=1, keepdims=True)
            hit = lane_f == first
            bias = jnp.where(hit, jnp.where(best > -jnp.inf, 0.0, bias), bias)
            gate = jnp.where(hit, -jnp.inf, gate)
        qh = (qh32 * ATTN_SCALE).astype(BF16)

        def tile(j, carry, diag, h=h, qh=qh, bias=bias):
            s = _dot(qh, kb_ref[j, h]) + ALIBI_SLOPES[h] * (kcol + ((j - i) * blk).astype(F32))
            if diag:
                s = jnp.where(causal, s, -jnp.inf)
            else:
                s = s + jnp.sum(jnp.where(lane == j, bias, 0.0), axis=1, keepdims=True)
            return _online_softmax_step(s, *carry, vb_ref[j, h])

        carry = lax.fori_loop(0, i, lambda j, c: tile(j, c, False), _softmax_init(blk))
        _, l, acc = tile(i, carry, True)
        heads.append(acc / l)
    o_ref[0] = jnp.concatenate(heads, axis=1).astype(o_ref.dtype)


def _prompt_attention(kind, q, kt, vt, lft=None):
    g, t, _ = q.shape
    tq = MOBA_BLOCK
    nkt = t // tq
    qspec = pl.BlockSpec((1, tq, HEADS_W), lambda b, i: (b, i, 0))
    kvspec = pl.BlockSpec((1, N_HEADS, HEAD_DIM, t), lambda b, i: (b, 0, 0, 0))
    scratch = [pltpu.VMEM((nkt, N_HEADS, HEAD_DIM, tq), BF16), pltpu.VMEM((nkt, N_HEADS, HEAD_DIM, tq), BF16)]
    if kind == "fox":
        body = functools.partial(_fox_prompt_kernel, tq=tq, nkt=nkt)
        in_specs = [qspec, kvspec, kvspec, pl.BlockSpec((1, V7X_SUBLANES, t), lambda b, i: (b, 0, 0))]
        args = (q, kt, vt, lft)
        scratch.append(pltpu.VMEM((nkt, V7X_SUBLANES, tq), F32))
    else:
        body = functools.partial(_moba_prompt_kernel, nb=nkt)
        in_specs = [qspec, kvspec, kvspec]
        args = (q, kt, vt)
        scratch.append(pltpu.VMEM((N_HEADS, HEAD_DIM, V7X_LANES), F32))
    return pl.pallas_call(
        body,
        out_shape=jax.ShapeDtypeStruct((g, t, HEADS_W), BF16),
        grid=(g, nkt),
        in_specs=in_specs,
        out_specs=qspec,
        scratch_shapes=scratch,
        compiler_params=_params("arbitrary", "arbitrary"),
        name=kind + "_prompt",
    )(*args)


def _page_specs(n, block, layer, pages_per_step):
    zeros = (0,) * (len(block) - 2)

    def spec(r):
        return pl.BlockSpec(block, lambda g, s, pt: (layer, pt[g, s * pages_per_step + r]) + zeros)

    return [spec(r) for r in range(n)]


def _head_q(q, h):
    return q[:, h * HEAD_DIM:(h + 1) * HEAD_DIM]


def _fox_sample_kernel(pt_ref, q_ref, knt_ref, vnt_ref, lfn_ref, *rest, pages, n_steps):
    kp, vp, lp = rest[:pages], rest[pages:2 * pages], rest[2 * pages:3 * pages]
    o_ref, m_ref, l_ref, acc_ref, carry_ref = rest[3 * pages:]
    s = pl.program_id(1)
    t_new = q_ref.shape[1]

    @pl.when(s == 0)
    def _init():
        m_ref[...] = jnp.full(m_ref.shape, -jnp.inf, F32)
        l_ref[...] = jnp.zeros(l_ref.shape, F32)
        acc_ref[...] = jnp.zeros(acc_ref.shape, F32)
        carry_ref[...] = jnp.zeros(carry_ref.shape, F32)

    rows = lax.broadcasted_iota(jnp.int32, (PAGE, PAGE), 0)
    cols = lax.broadcasted_iota(jnp.int32, (PAGE, PAGE), 1)
    upper = (rows <= cols).astype(F32)
    carry = carry_ref[:, 0:1]
    cums = []
    for r in range(pages):
        cum = _dot_exact(lp[r][...], upper) + carry
        cums.append(cum)
        carry = cum[:, PAGE - 1:PAGE]
    carry_ref[...] = jnp.broadcast_to(carry, carry_ref.shape)
    cum = jnp.concatenate(cums, axis=1)
    q = q_ref[0] * ATTN_SCALE

    def update(h, sc, vt):
        m_new, l, acc = _online_softmax_step(sc, m_ref[h][:, 0:1], l_ref[h][:, 0:1], acc_ref[h], vt)
        m_ref[h] = jnp.broadcast_to(m_new, m_ref.shape[1:])
        l_ref[h] = jnp.broadcast_to(l, l_ref.shape[1:])
        acc_ref[h] = acc

    for h in range(N_HEADS):
        kt = jnp.concatenate([kp[r][h] for r in range(pages)], axis=1).astype(BF16)
        vt = jnp.concatenate([vp[r][h] for r in range(pages)], axis=1).astype(BF16)
        update(h, _dot(_head_q(q, h).astype(BF16), kt) - cum[h:h + 1, :], vt)

    @pl.when(s == n_steps - 1)
    def _new_tokens():
        lane = lax.broadcasted_iota(jnp.int32, (t_new, PAGE), 1)
        qi = lax.broadcasted_iota(jnp.int32, (t_new, PAGE), 0)
        cum_n = _dot_exact(lfn_ref[0], upper) + carry_ref[:, 0:1]
        heads = []
        for h in range(N_HEADS):
            sc = _dot(_head_q(q, h).astype(BF16), knt_ref[0, h].astype(BF16)) - cum_n[h:h + 1, :]
            update(h, jnp.where(lane <= qi, sc, -jnp.inf), vnt_ref[0, h].astype(BF16))
            heads.append(acc_ref[h] / l_ref[h][:, 0:1])
        o_ref[0] = jnp.concatenate(heads, axis=1).astype(o_ref.dtype)


def _moba_sample_kernel(pt_ref, q_ref, knt_ref, vnt_ref, *rest, pages, n_steps, p_len):
    kp, vp = rest[:pages], rest[pages:2 * pages]
    o_ref, means_ref, mb_ref, lb_ref, accb_ref = rest[2 * pages:]
    blk = MOBA_BLOCK
    ppb = blk // PAGE
    nb = p_len // blk
    s = pl.program_id(1)
    t_new = q_ref.shape[1]

    @pl.when(s == 0)
    def _init():
        means_ref[...] = jnp.zeros(means_ref.shape, F32)
        mb_ref[...] = jnp.full(mb_ref.shape, -jnp.inf, F32)
        lb_ref[...] = jnp.zeros(lb_ref.shape, F32)

    q = q_ref[0]
    lane_q = lax.broadcasted_iota(jnp.int32, (t_new, V7X_LANES), 1)
    lane_d = lax.broadcasted_iota(jnp.int32, (HEAD_DIM, V7X_LANES), 1)
    kcol = lax.broadcasted_iota(jnp.int32, (1, blk), 1).astype(F32)

    for bb in range(pages // ppb):
        b = s * (pages // ppb) + bb
        for h in range(N_HEADS):
            kblk = jnp.concatenate([kp[bb * ppb + r][h] for r in range(ppb)], axis=1)
            vblk = jnp.concatenate([vp[bb * ppb + r][h] for r in range(ppb)], axis=1)
            mean = jnp.sum(kblk, axis=1, keepdims=True) * (1.0 / blk)
            means_ref[h] = jnp.where(lane_d == b, mean, means_ref[h])
            qh = (_head_q(q, h) * ATTN_SCALE).astype(BF16)
            sc = _dot(qh, kblk.astype(BF16)) + ALIBI_SLOPES[h] * (kcol + (b * blk - p_len).astype(F32))
            mx = jnp.max(sc, axis=1, keepdims=True)
            p = jnp.exp(sc - mx)
            accb_ref[b, h] = _dot_nt(p.astype(BF16), vblk.astype(BF16))
            mb_ref[h] = jnp.where(lane_q == b, mx, mb_ref[h])
            lb_ref[h] = jnp.where(lane_q == b, jnp.sum(p, axis=1, keepdims=True), lb_ref[h])

    @pl.when(s == n_steps - 1)
    def _select_and_combine():
        lane_f = lane_q.astype(F32)
        qi = lax.broadcasted_iota(jnp.int32, (t_new, V7X_LANES), 0)
        heads = []
        for h in range(N_HEADS):
            qh32 = _head_q(q, h)
            gate = jnp.where(lane_q < nb, _dot_exact(qh32, means_ref[h]), -jnp.inf)
            sel = jnp.zeros(gate.shape, F32)
            for _ in range(MOBA_TOPK):
                best = jnp.max(gate, axis=1, keepdims=True)
                first = jnp.min(jnp.where(gate == best, lane_f, float(V7X_LANES)), axis=1, keepdims=True)
                hit = lane_f == first
                sel = jnp.where(hit, jnp.where(best > -jnp.inf, 1.0, sel), sel)
                gate = jnp.where(hit, -jnp.inf, gate)
            qh = (qh32 * ATTN_SCALE).astype(BF16)
            sc = _dot(qh, knt_ref[0, h].astype(BF16)) + ALIBI_SLOPES[h] * lane_f
            sc = jnp.where(lane_q <= qi, sc, -jnp.inf)
            m_own = jnp.max(sc, axis=1, keepdims=True)
            p_own = jnp.exp(sc - m_own)
            l_own = jnp.sum(p_own, axis=1, keepdims=True)
            acc_own = _dot_nt(p_own.astype(BF16), vnt_ref[0, h].astype(BF16))
            picked = sel > 0.0
            m_all = jnp.maximum(m_own, jnp.max(jnp.where(picked, mb_ref[h], -jnp.inf), axis=1, keepdims=True))
            w = jnp.where(picked, jnp.exp(mb_ref[h] - m_all), 0.0)
            w_own = jnp.exp(m_own - m_all)
            den = jnp.sum(w * lb_ref[h], axis=1, keepdims=True) + w_own * l_own
            num = w_own * acc_own
            for b in range(nb):
                num = num + jnp.sum(jnp.where(lane_q == b, w, 0.0), axis=1, keepdims=True) * accb_ref[b, h]
            heads.append(num / den)
        o_ref[0] = jnp.concatenate(heads, axis=1).astype(o_ref.dtype)


def _sample_attention(kind, layer, page_table, q, knt, vnt, cache_kt, cache_vt, lfn=None, cache_lft=None):
    g, t_new, _ = q.shape
    n_pages = page_table.shape[1]
    pages = PAGES_PER_STEP
    n_steps = n_pages // pages
    per_seq = lambda blk: pl.BlockSpec(blk, lambda b, s, pt: (b,) + (0,) * (len(blk) - 1))
    page_blk = (None, None, N_HEADS, HEAD_DIM, PAGE)
    in_specs = [per_seq((1, t_new, HEADS_W)), per_seq((1, N_HEADS, HEAD_DIM, PAGE)),
                per_seq((1, N_HEADS, HEAD_DIM, PAGE))]
    args = [q, knt, vnt]
    if kind == "fox":
        in_specs.append(per_seq((1, V7X_SUBLANES, PAGE)))
        args.append(lfn)
    in_specs += _page_specs(pages, page_blk, layer, pages) + _page_specs(pages, page_blk, layer, pages)
    args += [cache_kt] * pages + [cache_vt] * pages
    stat = pltpu.VMEM((N_HEADS, t_new, V7X_LANES), F32)
    if kind == "fox":
        in_specs += _page_specs(pages, (None, None, V7X_SUBLANES, PAGE), layer, pages)
        args += [cache_lft] * pages
        body = functools.partial(_fox_sample_kernel, pages=pages, n_steps=n_steps)
        scratch = [stat, stat, pltpu.VMEM((N_HEADS, t_new, HEAD_DIM), F32),
                   pltpu.VMEM((V7X_SUBLANES, V7X_LANES), F32)]
    else:
        p_len = n_pages * PAGE
        body = functools.partial(_moba_sample_kernel, pages=pages, n_steps=n_steps, p_len=p_len)
        scratch = [pltpu.VMEM((N_HEADS, HEAD_DIM, V7X_LANES), F32), stat, stat,
                   pltpu.VMEM((p_len // MOBA_BLOCK, N_HEADS, t_new, HEAD_DIM), F32)]
    return pl.pallas_call(
        body,
        out_shape=jax.ShapeDtypeStruct((g, t_new, HEADS_W), BF16),
        grid_spec=pltpu.PrefetchScalarGridSpec(
            num_scalar_prefetch=1,
            grid=(g, n_steps),
            in_specs=in_specs,
            out_specs=per_seq((1, t_new, HEADS_W)),
            scratch_shapes=scratch),
        compiler_params=_params("arbitrary", "arbitrary"),
        name=kind + "_sample",
    )(page_table, *args)


def _causal_shifts(u, fill, period):
    out = []
    for j in (1, 2):
        r = pltpu.roll(u, j, axis=0)
        f = pltpu.roll(fill, j, axis=0)
        if period is None:
            row = lax.broadcasted_iota(jnp.int32, f.shape, 0)
            out.append(jnp.concatenate([jnp.where(row >= j, r[:V7X_SUBLANES], f), r[V7X_SUBLANES:]], axis=0))
        else:
            t = lax.broadcasted_iota(jnp.int32, u.shape, 0) & (period - 1)
            out.append(jnp.where(t >= j, r, f))
    return out


def _conv3(u, fill, period, w_ref, cols):
    s1, s2 = _causal_shifts(u, fill, period)
    return w_ref[0:1, cols] * s2 + w_ref[1:2, cols] * s1 + w_ref[2:3, cols] * u


def _outproj_kernel(x_ref, pc_ref, yf_ref, ym_ref, gt_ref, cw_ref, wo_ref, g_ref, b_ref, *rest,
                    period, dn_alpha):
    if period is None:
        o_ref, us_ref, carry_ref = rest

        @pl.when(pl.program_id(1) == 0)
        def _fresh_sequence():
            carry_ref[...] = jnp.zeros(carry_ref.shape, F32)

        fill = carry_ref[...]
    else:
        fill_ref, o_ref, us_ref = rest
        fill = fill_ref[0]
    pc = pc_ref[0]
    u = pc[:, CONV_W:2 * CONV_W] * pc[:, 2 * CONV_W:]
    y_conv = pc[:, :CONV_W] * _conv3(u, fill, period, cw_ref, slice(None))
    mixed = (_dot(y_conv.astype(BF16), wo_ref[0:CONV_W, :])
             + _dot(yf_ref[0], wo_ref[CONV_W:CONV_W + HEADS_W, :])
             + _dot(ym_ref[0], wo_ref[CONV_W + HEADS_W:, :]))
    z = dn_alpha * x_ref[0] + (1.0 + gt_ref[0]) * mixed
    o_ref[0] = _layer_norm(z, g_ref[...], b_ref[...])
    if period is None:
        tail = u[u.shape[0] - V7X_SUBLANES:]
        carry_ref[...] = tail
        us_ref[0] = tail
    else:
        us_ref[0] = u


def _outproj(x3, pconv, yf, ym, gt, conv_w, wo, ln_g, ln_b, fill, tm, dn_alpha):
    gb, t, d = x3.shape
    period = None if fill is None else V7X_SUBLANES
    tok = lambda w: pl.BlockSpec((1, tm, w), lambda g, i: (g, i, 0))
    in_specs = [tok(d), tok(3 * CONV_W), tok(HEADS_W), tok(HEADS_W), _mod_spec(gt, tm),
                _const_spec(conv_w.shape), _const_spec(wo.shape), _const_spec(ln_g.shape), _const_spec(ln_b.shape)]
    args = [x3, pconv, yf, ym, gt, conv_w, wo, ln_g, ln_b]
    scratch = []
    if period is None:
        us_rows = V7X_SUBLANES
        us_spec = pl.BlockSpec((1, us_rows, CONV_W), lambda g, i: (g, 0, 0))
        scratch.append(pltpu.VMEM((V7X_SUBLANES, CONV_W), F32))
    else:
        us_rows = t
        us_spec = tok(CONV_W)
        in_specs.append(tok(CONV_W))
        args.append(fill)
    return pl.pallas_call(
        functools.partial(_outproj_kernel, period=period, dn_alpha=dn_alpha),
        out_shape=(jax.ShapeDtypeStruct((gb, t, d), F32), jax.ShapeDtypeStruct((gb, us_rows, CONV_W), F32)),
        grid=(gb, t // tm),
        in_specs=in_specs,
        out_specs=(tok(d), us_spec),
        scratch_shapes=scratch,
        compiler_params=_params("arbitrary", "arbitrary"),
        name="outproj",
    )(*args)


def _ffn_kernel(x_ref, sc_ref, sh_ref, gt_ref, wup_ref, cw_ref, cb_ref, wdn_ref, g_ref, b_ref, *rest,
                period, dn_alpha):
    if period is None:
        o_ref, us_ref, carry_ref = rest

        @pl.when(pl.program_id(1) == 0)
        def _fresh_sequence():
            carry_ref[...] = jnp.zeros(carry_ref.shape, F32)
    else:
        fill_ref, o_ref, us_ref = rest
    x = x_ref[0]
    tm = x.shape[0]
    d_ff = wdn_ref.shape[0]
    h = (x * (1.0 + sc_ref[0]) + sh_ref[0]).astype(BF16)
    y = jnp.zeros(x.shape, F32)
    for c in range(d_ff // FF_CHUNK):
        halves = []
        for half in range(2):
            cols = slice(half * d_ff + c * FF_CHUNK, half * d_ff + (c + 1) * FF_CHUNK)
            u = _dot(h, wup_ref[:, cols])
            fill = carry_ref[:, cols] if period is None else fill_ref[0, :, cols]
            halves.append(_conv3(u, fill, period, cw_ref, cols) + cb_ref[:, cols])
            if period is None:
                tail = u[tm - V7X_SUBLANES:]
                carry_ref[:, cols] = tail
                us_ref[0, :, cols] = tail
            else:
                us_ref[0, :, cols] = u
        gate, val = halves
        act = (gate * _sigmoid(gate) * val).astype(BF16)
        y = y + _dot(act, wdn_ref[c * FF_CHUNK:(c + 1) * FF_CHUNK, :])
    z = dn_alpha * x + (1.0 + gt_ref[0]) * y
    o_ref[0] = _layer_norm(z, g_ref[...], b_ref[...])


def _ffn(x3, sc, sh, gt, wup, conv_w, conv_b, wdn, ln_g, ln_b, fill, tm, dn_alpha):
    gb, t, d = x3.shape
    n_up = wup.shape[1]
    period = None if fill is None else V7X_SUBLANES
    tok = lambda w: pl.BlockSpec((1, tm, w), lambda g, i: (g, i, 0))
    in_specs = [tok(d), _mod_spec(sc, tm), _mod_spec(sh, tm), _mod_spec(gt, tm),
                _const_spec(wup.shape), _const_spec(conv_w.shape), _const_spec(conv_b.shape),
                _const_spec(wdn.shape), _const_spec(ln_g.shape), _const_spec(ln_b.shape)]
    args = [x3, sc, sh, gt, wup, conv_w, conv_b, wdn, ln_g, ln_b]
    scratch = []
    if period is None:
        us_rows = V7X_SUBLANES
        us_spec = pl.BlockSpec((1, us_rows, n_up), lambda g, i: (g, 0, 0))
        scratch.append(pltpu.VMEM((V7X_SUBLANES, n_up), F32))
    else:
        us_rows = t
        us_spec = tok(n_up)
        in_specs.append(tok(n_up))
        args.append(fill)
    return pl.pallas_call(
        functools.partial(_ffn_kernel, period=period, dn_alpha=dn_alpha),
        out_shape=(jax.ShapeDtypeStruct((gb, t, d), F32), jax.ShapeDtypeStruct((gb, us_rows, n_up), F32)),
        grid=(gb, t // tm),
        in_specs=in_specs,
        out_specs=(tok(d), us_spec),
        scratch_shapes=scratch,
        compiler_params=_params("arbitrary", "arbitrary"),
        name="ffn",
    )(*args)


def _seq_fill(state):
    g, w, c = state.shape
    padded = jnp.pad(state, ((0, 0), (V7X_SUBLANES - w, 0), (0, 0)))
    return jnp.roll(padded, -1, axis=0).reshape(1, g * V7X_SUBLANES, c)


def _new_token_blocks(kt, g, t_new):
    per_seq = kt[0].reshape(N_HEADS, HEAD_DIM, g, t_new).transpose(2, 0, 1, 3)
    return jnp.pad(per_seq, ((0, 0), (0, 0), (0, 0), (0, PAGE - t_new)))


def _layer(l, group, x3, mods, weights, dn_alpha, tm, sample=None):
    sh_a, sc_a, gt_a, sh_f, sc_f, gt_f = mods
    (wtok, wkvt, wfft, bf, conv_w, wo, ln1_g, ln1_b, wup, ffn_w, ffn_b, wdn, ln2_g, ln2_b) = weights
    pconv, fq, mq, fkt, fvt, mkt, mvt, lft = _inproj(x3, sc_a, sh_a, wtok, wkvt, wfft, bf, tm)
    if sample is None:
        y_fox = _prompt_attention("fox", fq, fkt, fvt, lft)
        y_moba = _prompt_attention("moba", mq, mkt, mvt)
        conv_fill = ffn_fill = None
    else:
        g, t_new = sample["g"], sample["t"]
        per_seq = lambda a: a.reshape(g, t_new, a.shape[-1])
        lfn = lft[0].reshape(V7X_SUBLANES, g, t_new).transpose(1, 0, 2)
        lfn = jnp.pad(lfn, ((0, 0), (0, 0), (0, PAGE - t_new)))
        y_fox = _sample_attention("fox", l, sample["page_table"], per_seq(fq),
                                  _new_token_blocks(fkt, g, t_new), _new_token_blocks(fvt, g, t_new),
                                  sample["fox_kt"], sample["fox_vt"], lfn, sample["fox_lft"])
        y_moba = _sample_attention("moba", l, sample["page_table"], per_seq(mq),
                                   _new_token_blocks(mkt, g, t_new), _new_token_blocks(mvt, g, t_new),
                                   sample["moba_kt"], sample["moba_vt"])
        y_fox = y_fox.reshape(1, g * t_new, HEADS_W)
        y_moba = y_moba.reshape(1, g * t_new, HEADS_W)
        conv_fill = _seq_fill(sample["state_conv"][l])
        ffn_fill = _seq_fill(sample["state_ffn"][l])
    x1, conv_tail = _outproj(x3, pconv, y_fox, y_moba, gt_a, conv_w, wo, ln1_g, ln1_b, conv_fill, tm, dn_alpha)
    x2, ffn_tail = _ffn(x1, sc_f, sh_f, gt_f, wup, ffn_w, ffn_b, wdn, ln2_g, ln2_b, ffn_fill, tm, dn_alpha)
    return x2, (fkt, fvt, lft, mkt, mvt, conv_tail, ffn_tail)


def kernel(x_prompt, x_sample, cache_fox_k, cache_fox_v, cache_fox_logf, cache_moba_k, cache_moba_v, state_conv, state_ffn, page_table, c_prompt, c_sample, w_ada, b_ada, w_in, b_f, conv_w, w_o, ln1_g, ln1_b, w_up, ffn_conv_w, ffn_conv_b, w_down, ln2_g, ln2_b):
    depth, d, _ = w_ada.shape
    gp, tp, _ = x_prompt.shape
    gs, ts, _ = x_sample.shape
    dn_alpha = (2 * depth) ** 0.25
    n_conv, n_qkv = 3 * CONV_W, 3 * HEADS_W

    ada = _ada(jnp.concatenate([c_prompt, c_sample], axis=0), w_ada, b_ada)
    to_kt = lambda c: jnp.transpose(c, (0, 1, 3, 4, 2))
    lft_cache = jnp.pad(jnp.transpose(cache_fox_logf, (0, 1, 3, 2)),
                        ((0, 0), (0, 0), (0, V7X_SUBLANES - N_HEADS), (0, 0)))
    sample = dict(g=gs, t=ts, page_table=page_table, state_conv=state_conv, state_ffn=state_ffn,
                  fox_kt=to_kt(cache_fox_k), fox_vt=to_kt(cache_fox_v), fox_lft=lft_cache,
                  moba_kt=to_kt(cache_moba_k), moba_vt=to_kt(cache_moba_v))

    xp = x_prompt
    xs = x_sample.reshape(1, gs * ts, d)
    states_p, states_s = [], []
    for l in range(depth):
        w = w_in[l]
        fox0, ff0, moba0 = n_conv, n_conv + n_qkv, n_conv + n_qkv + N_HEADS
        wtok = jnp.concatenate([w[:, :n_conv], w[:, fox0:fox0 + HEADS_W], w[:, moba0:moba0 + HEADS_W]],
                               axis=1).astype(BF16)
        wkvt = jnp.concatenate([w[:, fox0 + HEADS_W:ff0], w[:, moba0 + HEADS_W:]], axis=1).T.astype(BF16)
        wfft = jnp.pad(w[:, ff0:moba0].T, ((0, 2 * V7X_SUBLANES - N_HEADS), (0, 0))).astype(BF16)
        bf = jnp.pad(b_f[l], (0, V7X_SUBLANES - N_HEADS)).reshape(V7X_SUBLANES, 1)
        weights = (wtok, wkvt, wfft, bf, conv_w[l], w_o[l].astype(BF16), ln1_g[l][None], ln1_b[l][None],
                   w_up[l].astype(BF16), ffn_conv_w[l], ffn_conv_b[l][None], w_down[l].astype(BF16),
                   ln2_g[l][None], ln2_b[l][None])
        mods_p = tuple(a[:, None, :] for a in jnp.split(ada[l, :gp], 6, axis=-1))
        mods_s = tuple(jnp.repeat(a, ts, axis=0)[None] for a in jnp.split(ada[l, gp:], 6, axis=-1))
        xp, st_p = _layer(l, "prompt", xp, mods_p, weights, dn_alpha, tm=512)
        xs, st_s = _layer(l, "sample", xs, mods_s, weights, dn_alpha, tm=gs * ts, sample=sample)
        states_p.append(st_p)
        states_s.append(st_s)

    stack = lambda states, i: jnp.stack([s[i] for s in states])
    out = [xp, xs.reshape(gs, ts, d)]
    kv_p = lambda i: jnp.transpose(stack(states_p, i), (0, 1, 4, 2, 3))
    out += [kv_p(0), kv_p(1), jnp.transpose(stack(states_p, 2)[:, :, :N_HEADS], (0, 1, 3, 2)), kv_p(3), kv_p(4),
            stack(states_p, 5)[:, :, V7X_SUBLANES - 2:], stack(states_p, 6)[:, :, V7X_SUBLANES - 2:]]

    def kv_s(i):
        a = stack(states_s, i)[:, 0]
        return jnp.transpose(a.reshape(depth, N_HEADS, HEAD_DIM, gs, ts), (0, 3, 4, 1, 2))

    lf_s = jnp.transpose(stack(states_s, 2)[:, 0, :N_HEADS].reshape(depth, N_HEADS, gs, ts), (0, 2, 3, 1))
    tail_s = lambda i: stack(states_s, i).reshape(depth, gs, ts, -1)[:, :, ts - 2:]
    out += [kv_s(0), kv_s(1), lf_s, kv_s(3), kv_s(4), tail_s(5), tail_s(6)]
    return tuple(out)
```

```python
import functools

import jax
import jax.numpy as jnp
from jax import lax
from jax.experimental import pallas as pl
from jax.experimental.pallas import tpu as pltpu

F32 = jnp.float32
BF16 = jnp.bfloat16
HIGHEST = lax.Precision.HIGHEST

HEAD_DIM = 64
N_HEADS = 6
HEADS_W = N_HEADS * HEAD_DIM
CONV_W = 256
MOBA_BLOCK = 256
MOBA_TOPK = 3
PAGE = 128
LN_EPS = 1e-5
ATTN_SCALE = HEAD_DIM ** -0.5
ALIBI_SLOPES = tuple(2.0 ** (-8.0 * (h + 1) / N_HEADS) for h in range(N_HEADS))
MASKED = -1e30

V7X_LANES = 128
V7X_SUBLANES = 8
V7X_VMEM_BYTES = 64 * 1024 * 1024
VMEM_LIMIT = 52 * 1024 * 1024
FF_CHUNK = 256
PAGES_PER_STEP = 16
AUG_KEY_TERM = HEAD_DIM
AUG_QUERY_TERM = HEAD_DIM + 3
AUG_END = HEAD_DIM + 6


def _dot(a, b):
    return jnp.dot(a, b, preferred_element_type=F32)


def _dot_nt(a, b):
    return lax.dot_general(a, b, (((1,), (1,)), ((), ())), preferred_element_type=F32)


def _dot_exact(a, b):
    return jnp.dot(a, b, precision=HIGHEST, preferred_element_type=F32)


def _sigmoid(x):
    return 1.0 / (1.0 + jnp.exp(-x))


def _log_sigmoid(x):
    return jnp.minimum(x, 0.0) - jnp.log(1.0 + jnp.exp(-jnp.abs(x)))


def _layer_norm(z, g, b):
    mu = jnp.mean(z, axis=-1, keepdims=True)
    zc = z - mu
    var = jnp.mean(zc * zc, axis=-1, keepdims=True)
    return zc * lax.rsqrt(var + LN_EPS) * g + b


def _split3(x):
    hi = x.astype(BF16).astype(F32)
    rest = x - hi
    mid = rest.astype(BF16).astype(F32)
    return hi, mid, rest - mid


def _params(*semantics):
    return pltpu.CompilerParams(dimension_semantics=semantics, vmem_limit_bytes=VMEM_LIMIT)


def _const_spec(shape):
    zeros = (0,) * len(shape)
    return pl.BlockSpec(shape, lambda *_: zeros)


def _mod_spec(arr, tm):
    d = arr.shape[-1]
    if arr.shape[1] == 1:
        return pl.BlockSpec((1, 1, d), lambda g, i: (g, 0, 0))
    return pl.BlockSpec((1, tm, d), lambda g, i: (g, i, 0))


def _ada_kernel(c_ref, w_ref, b_ref, o_ref):
    c = c_ref[...]
    o_ref[0] = _dot_exact(c * _sigmoid(c), w_ref[0]) + b_ref[0]


def _ada(c_all, w_ada, b_ada):
    depth, d, n = w_ada.shape
    g = c_all.shape[0]
    tn = 1536
    return pl.pallas_call(
        _ada_kernel,
        out_shape=jax.ShapeDtypeStruct((depth, g, n), F32),
        grid=(depth, n // tn),
        in_specs=[pl.BlockSpec((g, d), lambda l, j: (0, 0)),
                  pl.BlockSpec((1, d, tn), lambda l, j: (l, 0, j)),
                  pl.BlockSpec((1, 1, tn), lambda l, j: (l, 0, j))],
        out_specs=pl.BlockSpec((1, g, tn), lambda l, j: (l, 0, j)),
        compiler_params=_params("arbitrary", "arbitrary"),
        name="ada",
    )(c_all, w_ada, b_ada.reshape(depth, 1, n))


N_KV_STATES = 4


def _inproj_kernel(x_ref, sc_ref, sh_ref, wtok_ref, wqkvt_ref, bf_ref, *rest):
    conv_ref, lf_ref, fqt_ref, mqt_ref, fkt_ref, fvt_ref, mkt_ref, mvt_ref = rest[len(rest) - 8:]
    h = (x_ref[0] * (1.0 + sc_ref[0]) + sh_ref[0]).astype(BF16)
    conv_ref[0] = _dot(h, wtok_ref[:, 0:3 * CONV_W])
    lf_ref[0] = _log_sigmoid(_dot(h, wtok_ref[:, 3 * CONV_W:]) + bf_ref[...])
    for c, ref in enumerate((fqt_ref, fkt_ref, fvt_ref, mqt_ref, mkt_ref, mvt_ref)):
        t = _dot_nt(wqkvt_ref[c * HEADS_W:(c + 1) * HEADS_W, :], h)
        ref[0] = t.reshape(N_HEADS, HEAD_DIM, t.shape[-1])


def _inproj(layer, depth, x3, sc, sh, wtok, wqkvt, bf, kv_states, tm):
    gb, t, d = x3.shape
    tok = lambda w: pl.BlockSpec((1, tm, w), lambda g, i: (g, i, 0))
    qt = pl.BlockSpec((1, N_HEADS, HEAD_DIM, tm), lambda g, i: (g, 0, 0, i))
    kvt = pl.BlockSpec((None, 1, N_HEADS, HEAD_DIM, tm), lambda g, i: (layer, g, 0, 0, i))
    qt_shape = jax.ShapeDtypeStruct((gb, N_HEADS, HEAD_DIM, t), F32)
    kvt_shape = jax.ShapeDtypeStruct((depth, gb, N_HEADS, HEAD_DIM, t), F32)
    in_specs = [tok(d), _mod_spec(sc, tm), _mod_spec(sh, tm),
                _const_spec(wtok.shape), _const_spec(wqkvt.shape), _const_spec(bf.shape)]
    args = [x3, sc, sh, wtok, wqkvt, bf]
    aliases = {}
    if kv_states is not None:
        aliases = {len(args) + k: 4 + k for k in range(N_KV_STATES)}
        in_specs += [pl.BlockSpec(memory_space=pl.ANY)] * N_KV_STATES
        args += list(kv_states)
    return pl.pallas_call(
        _inproj_kernel,
        out_shape=(jax.ShapeDtypeStruct((gb, t, 3 * CONV_W), F32),
                   jax.ShapeDtypeStruct((gb, t, V7X_LANES), F32),
                   qt_shape, qt_shape, kvt_shape, kvt_shape, kvt_shape, kvt_shape),
        grid=(gb, t // tm),
        in_specs=in_specs,
        out_specs=(tok(3 * CONV_W), tok(V7X_LANES), qt, qt, kvt, kvt, kvt, kvt),
        input_output_aliases=aliases,
        compiler_params=_params("arbitrary", "arbitrary"),
        name="inproj",
    )(*args)


def _augment_keys(k_tok, key_term, lane):
    hi, mid, lo = _split3(key_term)
    x = jnp.where(lane < HEAD_DIM, jnp.concatenate([k_tok, jnp.zeros_like(k_tok)], axis=1),
                  jnp.where(lane < AUG_END, 1.0, 0.0))
    x = jnp.where(lane == AUG_KEY_TERM, hi, x)
    x = jnp.where(lane == AUG_KEY_TERM + 1, mid, x)
    return jnp.where(lane == AUG_KEY_TERM + 2, lo, x).astype(BF16)


def _augment_queries(qt, query_term):
    row = lax.broadcasted_iota(jnp.int32, qt.shape, 0)
    tail = jnp.where(row < AUG_QUERY_TERM - HEAD_DIM, 1.0, 0.0)
    if query_term is not None:
        for k, piece in enumerate(_split3(query_term)):
            tail = jnp.where(row == AUG_QUERY_TERM - HEAD_DIM + k, piece, tail)
    return jnp.concatenate([qt, tail], axis=0).astype(BF16)


def _flash_heads(i, tq, ka_ref, vb_ref, qa_ref, s_ref, p_ref, acc_ref, past_bias):
    key = lax.broadcasted_iota(jnp.int32, (tq, tq), 0)
    qry = lax.broadcasted_iota(jnp.int32, (tq, tq), 1)
    causal = key <= qry

    def step(j, carry, diag):
        for h in range(N_HEADS):
            s = _dot(ka_ref[j, h], qa_ref[h])
            if diag:
                s = jnp.where(causal, s, -jnp.inf)
            elif past_bias is not None:
                s = s + past_bias(h, j)
            s_ref[h] = s
        out, alphas = [], []
        for h in range(N_HEADS):
            m, l = carry[h]
            s = s_ref[h]
            m_new = jnp.maximum(m, jnp.max(s, axis=0, keepdims=True))
            alpha = jnp.exp(m - m_new)
            p = jnp.exp(s - m_new)
            p_ref[h] = p.astype(BF16)
            out.append((m_new, alpha * l + jnp.sum(p, axis=0, keepdims=True)))
            alphas.append(alpha)
        for h in range(N_HEADS):
            acc_ref[h] = alphas[h] * acc_ref[h] + _dot(vb_ref[j, h], p_ref[h])
        return tuple(out)

    acc_ref[...] = jnp.zeros(acc_ref.shape, F32)
    init = tuple((jnp.full((1, tq), -jnp.inf, F32), jnp.zeros((1, tq), F32)) for _ in range(N_HEADS))
    carry = lax.fori_loop(0, i, lambda j, c: step(j, c, False), init)
    carry = step(i, carry, True)
    return [(acc_ref[h] / carry[h][1]).T for h in range(N_HEADS)]


def _fox_prompt_kernel(qt_ref, kt_ref, vt_ref, lf_ref, o_ref, ka_ref, vb_ref, *work, tq, nkt):
    i = pl.program_id(1)

    @pl.when(i == 0)
    def _prepare_sequence():
        rows = lax.broadcasted_iota(jnp.int32, (tq, tq), 0)
        cols = lax.broadcasted_iota(jnp.int32, (tq, tq), 1)
        lower = (cols <= rows).astype(F32)
        lane = lax.broadcasted_iota(jnp.int32, (tq, V7X_LANES), 1)
        carry = jnp.zeros((1, V7X_LANES), F32)
        for c in range(nkt):
            sl = slice(c * tq, (c + 1) * tq)
            cum = _dot_exact(lower, lf_ref[0, sl, :]) + carry
            carry = cum[tq - 1:tq, :]
            vb_ref[c] = vt_ref[0, :, :, sl].astype(BF16)
            for h in range(N_HEADS):
                key_term = jnp.broadcast_to(-cum[:, h:h + 1], (tq, V7X_LANES))
                ka_ref[c, h] = _augment_keys(kt_ref[0, h, :, sl].T, key_term, lane)

    qt = qt_ref[0] * ATTN_SCALE
    qa_ref = work[0]
    for h in range(N_HEADS):
        qa_ref[h] = _augment_queries(qt[h], None)
    heads = _flash_heads(i, tq, ka_ref, vb_ref, *work, None)
    o_ref[0] = jnp.concatenate(heads, axis=1).astype(o_ref.dtype)


def _moba_prompt_kernel(qt_ref, kt_ref, vt_ref, o_ref, ka_ref, vb_ref, means_ref, *work, nb):
    blk = MOBA_BLOCK
    i = pl.program_id(1)

    @pl.when(i == 0)
    def _prepare_sequence():
        lane = lax.broadcasted_iota(jnp.int32, (blk, V7X_LANES), 1)
        pos = lax.broadcasted_iota(jnp.int32, (blk, V7X_LANES), 0)
        brow = lax.broadcasted_iota(jnp.int32, (V7X_SUBLANES, HEAD_DIM), 0)
        means = [jnp.zeros((V7X_SUBLANES, HEAD_DIM), F32) for _ in range(N_HEADS)]
        for c in range(nb):
            sl = slice(c * blk, (c + 1) * blk)
            vb_ref[c] = vt_ref[0, :, :, sl].astype(BF16)
            kpos = (pos + c * blk).astype(F32)
            for h in range(N_HEADS):
                k_tok = kt_ref[0, h, :, sl].T
                means[h] = jnp.where(brow == c, jnp.sum(k_tok, axis=0, keepdims=True) * (1.0 / blk), means[h])
                ka_ref[c, h] = _augment_keys(k_tok, ALIBI_SLOPES[h] * kpos, lane)
        for h in range(N_HEADS):
            means_ref[h] = means[h]

    q_raw = qt_ref[0]
    brow = lax.broadcasted_iota(jnp.int32, (V7X_SUBLANES, blk), 0)
    brow_f = brow.astype(F32)
    qa_ref = work[0]
    biases = []
    for h in range(N_HEADS):
        gate = jnp.where(brow < i, _dot_exact(means_ref[h], q_raw[h]), -jnp.inf)
        bias = jnp.full(gate.shape, MASKED, F32)
        for _ in range(MOBA_TOPK):
            best = jnp.max(gate, axis=0, keepdims=True)
            first = jnp.min(jnp.where(gate == best, brow_f, float(V7X_SUBLANES)), axis=0, keepdims=True)
            hit = brow_f == first
            bias = jnp.where(hit, jnp.where(best > -jnp.inf, 0.0, bias), bias)
            gate = jnp.where(hit, -jnp.inf, gate)
        biases.append(bias)
        tile_term = jnp.full((1, blk), -ALIBI_SLOPES[h], F32) * (i * blk).astype(F32)
        qa_ref[h] = _augment_queries(q_raw[h] * ATTN_SCALE, tile_term)

    def past_bias(h, j):
        return jnp.sum(jnp.where(brow == j, biases[h], 0.0), axis=0, keepdims=True)

    heads = _flash_heads(i, blk, ka_ref, vb_ref, *work, past_bias)
    o_ref[0] = jnp.concatenate(heads, axis=1).astype(o_ref.dtype)


def _prompt_attention(kind, layer, qt, kt, vt, lf=None):
    g, _, _, t = qt.shape
    tq = MOBA_BLOCK
    nkt = t // tq
    assert nkt <= V7X_SUBLANES, "block gates are kept on one sublane group"
    qspec = pl.BlockSpec((1, N_HEADS, HEAD_DIM, tq), lambda b, i: (b, 0, 0, i))
    kvspec = pl.BlockSpec((None, 1, N_HEADS, HEAD_DIM, t), lambda b, i: (layer, b, 0, 0, 0))
    scratch = [pltpu.VMEM((nkt, N_HEADS, tq, V7X_LANES), BF16), pltpu.VMEM((nkt, N_HEADS, HEAD_DIM, tq), BF16)]
    work = [pltpu.VMEM((N_HEADS, V7X_LANES, tq), BF16), pltpu.VMEM((N_HEADS, tq, tq), F32),
            pltpu.VMEM((N_HEADS, tq, tq), BF16), pltpu.VMEM((N_HEADS, HEAD_DIM, tq), F32)]
    if kind == "fox":
        body = functools.partial(_fox_prompt_kernel, tq=tq, nkt=nkt)
        in_specs = [qspec, kvspec, kvspec, pl.BlockSpec((1, t, V7X_LANES), lambda b, i: (b, 0, 0))]
        args = (qt, kt, vt, lf)
    else:
        body = functools.partial(_moba_prompt_kernel, nb=nkt)
        in_specs = [qspec, kvspec, kvspec]
        args = (qt, kt, vt)
        scratch.append(pltpu.VMEM((N_HEADS, V7X_SUBLANES, HEAD_DIM), F32))
    return pl.pallas_call(
        body,
        out_shape=jax.ShapeDtypeStruct((g, t, HEADS_W), BF16),
        grid=(g, nkt),
        in_specs=in_specs,
        out_specs=pl.BlockSpec((1, tq, HEADS_W), lambda b, i: (b, i, 0)),
        scratch_shapes=scratch + work,
        compiler_params=_params("arbitrary", "arbitrary"),
        name=kind + "_prompt",
    )(*args)


def _page_specs(n, block, layer, pages_per_step):
    zeros = (0,) * (len(block) - 2)

    def spec(r):
        return pl.BlockSpec(block, lambda g, s, pt: (layer, pt[g, s * pages_per_step + r]) + zeros)

    return [spec(r) for r in range(n)]


def _head_q(q, h):
    return q[:, h * HEAD_DIM:(h + 1) * HEAD_DIM]


def _softmax_updates(scores, state, vts):
    probs = []
    for s, (m, l, _) in zip(scores, state):
        m_new = jnp.maximum(m, jnp.max(s, axis=1, keepdims=True))
        p = jnp.exp(s - m_new)
        alpha = jnp.exp(m - m_new)
        probs.append((m_new, alpha * l + jnp.sum(p, axis=1, keepdims=True), alpha, p.astype(BF16)))
    return [(m_new, l, alpha * acc + _dot_nt(p, vt))
            for (m_new, l, alpha, p), (_, _, acc), vt in zip(probs, state, vts)]


def _fox_sample_kernel(pt_ref, q_ref, knt_ref, vnt_ref, lfn_ref, *rest, pages, n_steps):
    kp, vp, lp = rest[:pages], rest[pages:2 * pages], rest[2 * pages:3 * pages]
    o_ref, m_ref, l_ref, acc_ref, carry_ref = rest[3 * pages:]
    s = pl.program_id(1)
    t_new = q_ref.shape[1]

    @pl.when(s == 0)
    def _init():
        m_ref[...] = jnp.full(m_ref.shape, -jnp.inf, F32)
        l_ref[...] = jnp.zeros(l_ref.shape, F32)
        acc_ref[...] = jnp.zeros(acc_ref.shape, F32)
        carry_ref[...] = jnp.zeros(carry_ref.shape, F32)

    state = [(m_ref[h][:, 0:1], l_ref[h][:, 0:1], acc_ref[h]) for h in range(N_HEADS)]
    carry = carry_ref[:, 0:1]

    rows = lax.broadcasted_iota(jnp.int32, (PAGE, PAGE), 0)
    cols = lax.broadcasted_iota(jnp.int32, (PAGE, PAGE), 1)
    upper = (rows <= cols).astype(F32)
    local = _dot_exact(jnp.concatenate([lp[r][...] for r in range(pages)], axis=0), upper)
    cums = []
    for r in range(pages):
        cum = local[r * V7X_SUBLANES:(r + 1) * V7X_SUBLANES] + carry
        cums.append(cum)
        carry = cum[:, PAGE - 1:PAGE]
    cum = jnp.concatenate(cums, axis=1)
    q = q_ref[0] * ATTN_SCALE

    page_cat = lambda refs, h: jnp.concatenate([refs[r][h] for r in range(pages)], axis=1).astype(BF16)
    qs = [_head_q(q, h).astype(BF16) for h in range(N_HEADS)]
    scores = [_dot(qs[h], page_cat(kp, h)) - cum[h:h + 1, :] for h in range(N_HEADS)]
    state = _softmax_updates(scores, state, [page_cat(vp, h) for h in range(N_HEADS)])

    carry_ref[...] = jnp.broadcast_to(carry, carry_ref.shape)
    for h in range(N_HEADS):
        m_ref[h] = jnp.broadcast_to(state[h][0], m_ref.shape[1:])
        l_ref[h] = jnp.broadcast_to(state[h][1], l_ref.shape[1:])
        acc_ref[h] = state[h][2]

    @pl.when(s == n_steps - 1)
    def _new_tokens():
        lane = lax.broadcasted_iota(jnp.int32, (t_new, PAGE), 1)
        qi = lax.broadcasted_iota(jnp.int32, (t_new, PAGE), 0)
        cum_n = _dot_exact(lfn_ref[0], upper) + carry
        scores = [jnp.where(lane <= qi, _dot(qs[h], knt_ref[0, h].astype(BF16)) - cum_n[h:h + 1, :], -jnp.inf)
                  for h in range(N_HEADS)]
        final = _softmax_updates(scores, state, [vnt_ref[0, h].astype(BF16) for h in range(N_HEADS)])
        o_ref[0] = jnp.concatenate([acc / l for _, l, acc in final], axis=1).astype(o_ref.dtype)


def _moba_sample_kernel(pt_ref, q_ref, knt_ref, vnt_ref, *rest, pages, n_steps, p_len):
    kp, vp = rest[:pages], rest[pages:2 * pages]
    o_ref, means_ref, mb_ref, lb_ref, accb_ref = rest[2 * pages:]
    blk = MOBA_BLOCK
    ppb = blk // PAGE
    nb = p_len // blk
    s = pl.program_id(1)
    t_new = q_ref.shape[1]

    @pl.when(s == 0)
    def _init():
        means_ref[...] = jnp.zeros(means_ref.shape, F32)
        mb_ref[...] = jnp.full(mb_ref.shape, -jnp.inf, F32)
        lb_ref[...] = jnp.zeros(lb_ref.shape, F32)

    means = [means_ref[h] for h in range(N_HEADS)]
    mb = [mb_ref[h] for h in range(N_HEADS)]
    lb = [lb_ref[h] for h in range(N_HEADS)]
    q = q_ref[0]
    lane_q = lax.broadcasted_iota(jnp.int32, (t_new, V7X_LANES), 1)
    lane_d = lax.broadcasted_iota(jnp.int32, (HEAD_DIM, V7X_LANES), 1)
    kcol = lax.broadcasted_iota(jnp.int32, (1, blk), 1).astype(F32)
    qs = [(_head_q(q, h) * ATTN_SCALE).astype(BF16) for h in range(N_HEADS)]

    units = [(bb, h) for bb in range(pages // ppb) for h in range(N_HEADS)]
    block_of = lambda bb: s * (pages // ppb) + bb
    block_cat = lambda refs, bb, h: jnp.concatenate([refs[bb * ppb + r][h] for r in range(ppb)], axis=1)
    scores, probs = {}, {}
    for bb, h in units:
        b = block_of(bb)
        kblk = block_cat(kp, bb, h)
        means[h] = jnp.where(lane_d == b, jnp.sum(kblk, axis=1, keepdims=True) * (1.0 / blk), means[h])
        scores[bb, h] = (_dot(qs[h], kblk.astype(BF16))
                         + ALIBI_SLOPES[h] * (kcol + (b * blk - p_len).astype(F32)))
    for bb, h in units:
        b = block_of(bb)
        mx = jnp.max(scores[bb, h], axis=1, keepdims=True)
        p = jnp.exp(scores[bb, h] - mx)
        probs[bb, h] = p.astype(BF16)
        mb[h] = jnp.where(lane_q == b, mx, mb[h])
        lb[h] = jnp.where(lane_q == b, jnp.sum(p, axis=1, keepdims=True), lb[h])
    for bb, h in units:
        accb_ref[block_of(bb), h] = _dot_nt(probs[bb, h], block_cat(vp, bb, h).astype(BF16))

    for h in range(N_HEADS):
        means_ref[h] = means[h]
        mb_ref[h] = mb[h]
        lb_ref[h] = lb[h]

    @pl.when(s == n_steps - 1)
    def _select_and_combine():
        lane_f = lane_q.astype(F32)
        qi = lax.broadcasted_iota(jnp.int32, (t_new, V7X_LANES), 0)
        heads = []
        for h in range(N_HEADS):
            gate = jnp.where(lane_q < nb, _dot_exact(_head_q(q, h), means[h]), -jnp.inf)
            sel = jnp.zeros(gate.shape, F32)
            for _ in range(MOBA_TOPK):
                best = jnp.max(gate, axis=1, keepdims=True)
                first = jnp.min(jnp.where(gate == best, lane_f, float(V7X_LANES)), axis=1, keepdims=True)
                hit = lane_f == first
                sel = jnp.where(hit, jnp.where(best > -jnp.inf, 1.0, sel), sel)
                gate = jnp.where(hit, -jnp.inf, gate)
            sc = _dot(qs[h], knt_ref[0, h].astype(BF16)) + ALIBI_SLOPES[h] * lane_f
            sc = jnp.where(lane_q <= qi, sc, -jnp.inf)
            m_own = jnp.max(sc, axis=1, keepdims=True)
            p_own = jnp.exp(sc - m_own)
            l_own = jnp.sum(p_own, axis=1, keepdims=True)
            acc_own = _dot_nt(p_own.astype(BF16), vnt_ref[0, h].astype(BF16))
            picked = sel > 0.0
            m_all = jnp.maximum(m_own, jnp.max(jnp.where(picked, mb[h], -jnp.inf), axis=1, keepdims=True))
            w = jnp.where(picked, jnp.exp(mb[h] - m_all), 0.0)
            w_own = jnp.exp(m_own - m_all)
            den = jnp.sum(w * lb[h], axis=1, keepdims=True) + w_own * l_own
            num = w_own * acc_own
            for b in range(nb):
                num = num + jnp.sum(jnp.where(lane_q == b, w, 0.0), axis=1, keepdims=True) * accb_ref[b, h]
            heads.append(num / den)
        o_ref[0] = jnp.concatenate(heads, axis=1).astype(o_ref.dtype)


def _sample_attention(kind, layer, page_table, q, knt, vnt, cache_kt, cache_vt, lfn=None, cache_lft=None):
    g, t_new, _ = q.shape
    n_pages = page_table.shape[1]
    pages = min(PAGES_PER_STEP, n_pages)
    n_steps = n_pages // pages
    per_seq = lambda blk: pl.BlockSpec(blk, lambda b, s, pt: (b,) + (0,) * (len(blk) - 1))
    page_blk = (None, None, N_HEADS, HEAD_DIM, PAGE)
    in_specs = [per_seq((1, t_new, HEADS_W)), per_seq((1, N_HEADS, HEAD_DIM, PAGE)),
                per_seq((1, N_HEADS, HEAD_DIM, PAGE))]
    args = [q, knt, vnt]
    if kind == "fox":
        in_specs.append(per_seq((1, V7X_SUBLANES, PAGE)))
        args.append(lfn)
    in_specs += _page_specs(pages, page_blk, layer, pages) + _page_specs(pages, page_blk, layer, pages)
    args += [cache_kt] * pages + [cache_vt] * pages
    stat = pltpu.VMEM((N_HEADS, t_new, V7X_LANES), F32)
    if kind == "fox":
        in_specs += _page_specs(pages, (None, None, V7X_SUBLANES, PAGE), layer, pages)
        args += [cache_lft] * pages
        body = functools.partial(_fox_sample_kernel, pages=pages, n_steps=n_steps)
        scratch = [stat, stat, pltpu.VMEM((N_HEADS, t_new, HEAD_DIM), F32),
                   pltpu.VMEM((V7X_SUBLANES, V7X_LANES), F32)]
    else:
        p_len = n_pages * PAGE
        body = functools.partial(_moba_sample_kernel, pages=pages, n_steps=n_steps, p_len=p_len)
        scratch = [pltpu.VMEM((N_HEADS, HEAD_DIM, V7X_LANES), F32), stat, stat,
                   pltpu.VMEM((p_len // MOBA_BLOCK, N_HEADS, t_new, HEAD_DIM), F32)]
    return pl.pallas_call(
        body,
        out_shape=jax.ShapeDtypeStruct((g, t_new, HEADS_W), BF16),
        grid_spec=pltpu.PrefetchScalarGridSpec(
            num_scalar_prefetch=1,
            grid=(g, n_steps),
            in_specs=in_specs,
            out_specs=per_seq((1, t_new, HEADS_W)),
            scratch_shapes=scratch),
        compiler_params=_params("arbitrary", "arbitrary"),
        name=kind + "_sample",
    )(page_table, *args)


def _causal_shifts(u, fill, period):
    out = []
    for j in (1, 2):
        r = pltpu.roll(u, j, axis=0)
        f = pltpu.roll(fill, j, axis=0)
        if period is None:
            row = lax.broadcasted_iota(jnp.int32, f.shape, 0)
            out.append(jnp.concatenate([jnp.where(row >= j, r[:V7X_SUBLANES], f), r[V7X_SUBLANES:]], axis=0))
        else:
            t = lax.broadcasted_iota(jnp.int32, u.shape, 0) & (period - 1)
            out.append(jnp.where(t >= j, r, f))
    return out


def _conv3(u, fill, period, w_ref, cols):
    s1, s2 = _causal_shifts(u, fill, period)
    return w_ref[0:1, cols] * s2 + w_ref[1:2, cols] * s1 + w_ref[2:3, cols] * u


def _outproj_kernel(x_ref, pc_ref, yf_ref, ym_ref, gt_ref, cw_ref, wo_ref, g_ref, b_ref, *rest,
                    period, dn_alpha):
    if period is None:
        o_ref, us_ref, carry_ref = rest

        @pl.when(pl.program_id(1) == 0)
        def _fresh_sequence():
            carry_ref[...] = jnp.zeros(carry_ref.shape, F32)

        fill = carry_ref[...]
    else:
        fill_ref, o_ref, us_ref = rest
        fill = fill_ref[0]
    pc = pc_ref[0]
    u = pc[:, CONV_W:2 * CONV_W] * pc[:, 2 * CONV_W:]
    y_conv = pc[:, :CONV_W] * _conv3(u, fill, period, cw_ref, slice(None))
    mixed = (_dot(y_conv.astype(BF16), wo_ref[0:CONV_W, :])
             + _dot(yf_ref[0], wo_ref[CONV_W:CONV_W + HEADS_W, :])
             + _dot(ym_ref[0], wo_ref[CONV_W + HEADS_W:, :]))
    z = dn_alpha * x_ref[0] + (1.0 + gt_ref[0]) * mixed
    o_ref[0] = _layer_norm(z, g_ref[...], b_ref[...])
    if period is None:
        tail = u[u.shape[0] - V7X_SUBLANES:]
        carry_ref[...] = tail
        us_ref[0] = tail
    else:
        us_ref[0] = u


def _outproj(x3, pconv, yf, ym, gt, conv_w, wo, ln_g, ln_b, fill, tm, dn_alpha):
    gb, t, d = x3.shape
    period = None if fill is None else V7X_SUBLANES
    tok = lambda w: pl.BlockSpec((1, tm, w), lambda g, i: (g, i, 0))
    in_specs = [tok(d), tok(3 * CONV_W), tok(HEADS_W), tok(HEADS_W), _mod_spec(gt, tm),
                _const_spec(conv_w.shape), _const_spec(wo.shape), _const_spec(ln_g.shape), _const_spec(ln_b.shape)]
    args = [x3, pconv, yf, ym, gt, conv_w, wo, ln_g, ln_b]
    scratch = []
    if period is None:
        us_rows = V7X_SUBLANES
        us_spec = pl.BlockSpec((1, us_rows, CONV_W), lambda g, i: (g, 0, 0))
        scratch.append(pltpu.VMEM((V7X_SUBLANES, CONV_W), F32))
    else:
        us_rows = t
        us_spec = tok(CONV_W)
        in_specs.append(tok(CONV_W))
        args.append(fill)
    return pl.pallas_call(
        functools.partial(_outproj_kernel, period=period, dn_alpha=dn_alpha),
        out_shape=(jax.ShapeDtypeStruct((gb, t, d), F32), jax.ShapeDtypeStruct((gb, us_rows, CONV_W), F32)),
        grid=(gb, t // tm),
        in_specs=in_specs,
        out_specs=(tok(d), us_spec),
        scratch_shapes=scratch,
        compiler_params=_params("arbitrary", "arbitrary"),
        name="outproj",
    )(*args)


def _ffn_kernel(x_ref, sc_ref, sh_ref, gt_ref, wup_ref, cw_ref, cb_ref, wdn_ref, g_ref, b_ref, *rest,
                period, dn_alpha):
    if period is None:
        o_ref, us_ref, carry_ref = rest

        @pl.when(pl.program_id(1) == 0)
        def _fresh_sequence():
            carry_ref[...] = jnp.zeros(carry_ref.shape, F32)
    else:
        fill_ref, o_ref, us_ref = rest
    x = x_ref[0]
    tm = x.shape[0]
    d_ff = wdn_ref.shape[0]
    h = (x * (1.0 + sc_ref[0]) + sh_ref[0]).astype(BF16)
    y = jnp.zeros(x.shape, F32)
    for c in range(d_ff // FF_CHUNK):
        halves = []
        for half in range(2):
            cols = slice(half * d_ff + c * FF_CHUNK, half * d_ff + (c + 1) * FF_CHUNK)
            u = _dot(h, wup_ref[:, cols])
            fill = carry_ref[:, cols] if period is None else fill_ref[0, :, cols]
            halves.append(_conv3(u, fill, period, cw_ref, cols) + cb_ref[:, cols])
            if period is None:
                tail = u[tm - V7X_SUBLANES:]
                carry_ref[:, cols] = tail
                us_ref[0, :, cols] = tail
            else:
                us_ref[0, :, cols] = u
        gate, val = halves
        act = (gate * _sigmoid(gate) * val).astype(BF16)
        y = y + _dot(act, wdn_ref[c * FF_CHUNK:(c + 1) * FF_CHUNK, :])
    z = dn_alpha * x + (1.0 + gt_ref[0]) * y
    o_ref[0] = _layer_norm(z, g_ref[...], b_ref[...])


def _ffn(x3, sc, sh, gt, wup, conv_w, conv_b, wdn, ln_g, ln_b, fill, tm, dn_alpha):
    gb, t, d = x3.shape
    n_up = wup.shape[1]
    period = None if fill is None else V7X_SUBLANES
    tok = lambda w: pl.BlockSpec((1, tm, w), lambda g, i: (g, i, 0))
    in_specs = [tok(d), _mod_spec(sc, tm), _mod_spec(sh, tm), _mod_spec(gt, tm),
                _const_spec(wup.shape), _const_spec(conv_w.shape), _const_spec(conv_b.shape),
                _const_spec(wdn.shape), _const_spec(ln_g.shape), _const_spec(ln_b.shape)]
    args = [x3, sc, sh, gt, wup, conv_w, conv_b, wdn, ln_g, ln_b]
    scratch = []
    if period is None:
        us_rows = V7X_SUBLANES
        us_spec = pl.BlockSpec((1, us_rows, n_up), lambda g, i: (g, 0, 0))
        scratch.append(pltpu.VMEM((V7X_SUBLANES, n_up), F32))
    else:
        us_rows = t
        us_spec = tok(n_up)
        in_specs.append(tok(n_up))
        args.append(fill)
    return pl.pallas_call(
        functools.partial(_ffn_kernel, period=period, dn_alpha=dn_alpha),
        out_shape=(jax.ShapeDtypeStruct((gb, t, d), F32), jax.ShapeDtypeStruct((gb, us_rows, n_up), F32)),
        grid=(gb, t // tm),
        in_specs=in_specs,
        out_specs=(tok(d), us_spec),
        scratch_shapes=scratch,
        compiler_params=_params("arbitrary", "arbitrary"),
        name="ffn",
    )(*args)


def _seq_fill(state):
    g, w, c = state.shape
    padded = jnp.pad(state, ((0, 0), (V7X_SUBLANES - w, 0), (0, 0)))
    return jnp.roll(padded, -1, axis=0).reshape(1, g * V7X_SUBLANES, c)


def _per_seq_transposed(xt, g, t_new):
    return xt.reshape(N_HEADS, HEAD_DIM, g, t_new).transpose(2, 0, 1, 3)


def _new_token_blocks(xt, g, t_new):
    return jnp.pad(_per_seq_transposed(xt, g, t_new), ((0, 0), (0, 0), (0, 0), (0, PAGE - t_new)))


def _layer(l, depth, x3, mods, weights, kv_states, dn_alpha, tm, sample=None):
    sh_a, sc_a, gt_a, sh_f, sc_f, gt_f = mods
    (wtok, wqkvt, bf, conv_w, wo, ln1_g, ln1_b, wup, ffn_w, ffn_b, wdn, ln2_g, ln2_b) = weights
    pconv, lf, fqt, mqt, *kv_states = _inproj(l, depth, x3, sc_a, sh_a, wtok, wqkvt, bf, kv_states, tm)
    fkt, fvt, mkt, mvt = kv_states
    if sample is None:
        y_fox = _prompt_attention("fox", l, fqt, fkt, fvt, lf)
        y_moba = _prompt_attention("moba", l, mqt, mkt, mvt)
        conv_fill = ffn_fill = None
    else:
        g, t_new = sample["g"], sample["t"]
        tok_q = lambda qt: _per_seq_transposed(qt[0], g, t_new).transpose(0, 3, 1, 2).reshape(g, t_new, HEADS_W)
        new = lambda st: _new_token_blocks(st[l, 0], g, t_new)
        lfn = lf[0].reshape(g, t_new, V7X_LANES)[:, :, :V7X_SUBLANES].transpose(0, 2, 1)
        lfn = jnp.pad(lfn, ((0, 0), (0, 0), (0, PAGE - t_new)))
        y_fox = _sample_attention("fox", l, sample["page_table"], tok_q(fqt), new(fkt), new(fvt),
                                  sample["fox_kt"], sample["fox_vt"], lfn, sample["fox_lft"])
        y_moba = _sample_attention("moba", l, sample["page_table"], tok_q(mqt), new(mkt), new(mvt),
                                   sample["moba_kt"], sample["moba_vt"])
        y_fox = y_fox.reshape(1, g * t_new, HEADS_W)
        y_moba = y_moba.reshape(1, g * t_new, HEADS_W)
        conv_fill = _seq_fill(sample["state_conv"][l])
        ffn_fill = _seq_fill(sample["state_ffn"][l])
    x1, conv_tail = _outproj(x3, pconv, y_fox, y_moba, gt_a, conv_w, wo, ln1_g, ln1_b, conv_fill, tm, dn_alpha)
    x2, ffn_tail = _ffn(x1, sc_f, sh_f, gt_f, wup, ffn_w, ffn_b, wdn, ln2_g, ln2_b, ffn_fill, tm, dn_alpha)
    return x2, kv_states, (lf, conv_tail, ffn_tail)


def kernel(x_prompt, x_sample, cache_fox_k, cache_fox_v, cache_fox_logf, cache_moba_k, cache_moba_v, state_conv, state_ffn, page_table, c_prompt, c_sample, w_ada, b_ada, w_in, b_f, conv_w, w_o, ln1_g, ln1_b, w_up, ffn_conv_w, ffn_conv_b, w_down, ln2_g, ln2_b):
    depth, d, _ = w_ada.shape
    gp, tp, _ = x_prompt.shape
    gs, ts, _ = x_sample.shape
    dn_alpha = (2 * depth) ** 0.25
    n_conv, n_qkv = 3 * CONV_W, 3 * HEADS_W

    ada = _ada(jnp.concatenate([c_prompt, c_sample], axis=0), w_ada, b_ada)
    to_kt = lambda c: jnp.transpose(c, (0, 1, 3, 4, 2))
    lft_cache = jnp.pad(jnp.transpose(cache_fox_logf, (0, 1, 3, 2)),
                        ((0, 0), (0, 0), (0, V7X_SUBLANES - N_HEADS), (0, 0)))
    sample = dict(g=gs, t=ts, page_table=page_table, state_conv=state_conv, state_ffn=state_ffn,
                  fox_kt=to_kt(cache_fox_k), fox_vt=to_kt(cache_fox_v), fox_lft=lft_cache,
                  moba_kt=to_kt(cache_moba_k), moba_vt=to_kt(cache_moba_v))

    xp = x_prompt
    xs = x_sample.reshape(1, gs * ts, d)
    kv_p = kv_s = None
    small_p, small_s = [], []
    for l in range(depth):
        w = w_in[l]
        fox0, ff0, moba0 = n_conv, n_conv + n_qkv, n_conv + n_qkv + N_HEADS
        wtok = jnp.concatenate([w[:, :n_conv], jnp.pad(w[:, ff0:moba0], ((0, 0), (0, V7X_LANES - N_HEADS)))],
                               axis=1).astype(BF16)
        wqkvt = jnp.concatenate([w[:, fox0:ff0], w[:, moba0:]], axis=1).T.astype(BF16)
        bf = jnp.pad(b_f[l], (0, V7X_LANES - N_HEADS))[None]
        weights = (wtok, wqkvt, bf, conv_w[l], w_o[l].astype(BF16), ln1_g[l][None], ln1_b[l][None],
                   w_up[l].astype(BF16), ffn_conv_w[l], ffn_conv_b[l][None], w_down[l].astype(BF16),
                   ln2_g[l][None], ln2_b[l][None])
        mods_p = tuple(a[:, None, :] for a in jnp.split(ada[l, :gp], 6, axis=-1))
        mods_s = tuple(jnp.repeat(a, ts, axis=0)[None] for a in jnp.split(ada[l, gp:], 6, axis=-1))
        xp, kv_p, st_p = _layer(l, depth, xp, mods_p, weights, kv_p, dn_alpha, tm=512)
        xs, kv_s, st_s = _layer(l, depth, xs, mods_s, weights, kv_s, dn_alpha, tm=gs * ts, sample=sample)
        small_p.append(st_p)
        small_s.append(st_s)

    stack = lambda states, i: jnp.stack([s[i] for s in states])
    out = [xp, xs.reshape(gs, ts, d)]
    fk, fv, mk, mv = (jnp.transpose(a, (0, 1, 4, 2, 3)) for a in kv_p)
    out += [fk, fv, stack(small_p, 0)[..., :N_HEADS], mk, mv,
            stack(small_p, 1)[:, :, V7X_SUBLANES - 2:], stack(small_p, 2)[:, :, V7X_SUBLANES - 2:]]
    fk, fv, mk, mv = (jnp.transpose(a[:, 0].reshape(depth, N_HEADS, HEAD_DIM, gs, ts), (0, 3, 4, 1, 2))
                      for a in kv_s)
    tail_s = lambda i: stack(small_s, i).reshape(depth, gs, ts, -1)[:, :, ts - 2:]
    out += [fk, fv, stack(small_s, 0).reshape(depth, gs, ts, V7X_LANES)[..., :N_HEADS], mk, mv,
            tail_s(1), tail_s(2)]
    return tuple(out)
```

```python
import functools

import jax
import jax.numpy as jnp
from jax import lax
from jax.experimental import pallas as pl
from jax.experimental.pallas import tpu as pltpu

F32 = jnp.float32
BF16 = jnp.bfloat16
HIGHEST = lax.Precision.HIGHEST

HEAD_DIM = 64
N_HEADS = 6
HEADS_W = N_HEADS * HEAD_DIM
CONV_W = 256
MOBA_BLOCK = 256
MOBA_TOPK = 3
PAGE = 128
LN_EPS = 1e-5
ATTN_SCALE = HEAD_DIM ** -0.5
LOG2E = 1.4426950408889634
ALIBI_SLOPES = tuple(2.0 ** (-8.0 * (h + 1) / N_HEADS) for h in range(N_HEADS))
MASKED = -1e30

V7X_LANES = 128
V7X_SUBLANES = 8
V7X_VMEM_BYTES = 64 * 1024 * 1024
VMEM_LIMIT = 52 * 1024 * 1024
PAGES_PER_STEP = 16
AUG_KEY_TERM = HEAD_DIM
AUG_QUERY_TERM = HEAD_DIM + 3
AUG_END = HEAD_DIM + 6


def _dot(a, b):
    return jnp.dot(a, b, preferred_element_type=F32)


def _dot_nt(a, b):
    return lax.dot_general(a, b, (((1,), (1,)), ((), ())), preferred_element_type=F32)


def _dot_exact(a, b):
    return jnp.dot(a, b, precision=HIGHEST, preferred_element_type=F32)


def _sigmoid(x):
    return 1.0 / (1.0 + jnp.exp(-x))


def _log_sigmoid(x):
    return jnp.minimum(x, 0.0) - jnp.log(1.0 + jnp.exp(-jnp.abs(x)))


def _layer_norm(z, g, b):
    mu = jnp.mean(z, axis=-1, keepdims=True)
    zc = z - mu
    var = jnp.mean(zc * zc, axis=-1, keepdims=True)
    return zc * lax.rsqrt(var + LN_EPS) * g + b


def _split3(x):
    hi = x.astype(BF16).astype(F32)
    rest = x - hi
    mid = rest.astype(BF16).astype(F32)
    return hi, mid, rest - mid


def _params(*semantics):
    return pltpu.CompilerParams(dimension_semantics=semantics, vmem_limit_bytes=VMEM_LIMIT)


def _const_spec(shape):
    zeros = (0,) * len(shape)
    return pl.BlockSpec(shape, lambda *_: zeros)


def _mod_spec(arr, tm):
    d = arr.shape[-1]
    if arr.shape[1] == 1:
        return pl.BlockSpec((1, 1, d), lambda g, i: (g, 0, 0))
    return pl.BlockSpec((1, tm, d), lambda g, i: (g, i, 0))


def _ada_kernel(c_ref, w_ref, b_ref, o_ref):
    c = c_ref[...]
    o_ref[0] = _dot_exact(c * _sigmoid(c), w_ref[0]) + b_ref[0]


def _ada(c_all, w_ada, b_ada):
    depth, d, n = w_ada.shape
    g = c_all.shape[0]
    tn = 1536
    return pl.pallas_call(
        _ada_kernel,
        out_shape=jax.ShapeDtypeStruct((depth, g, n), F32),
        grid=(depth, n // tn),
        in_specs=[pl.BlockSpec((g, d), lambda l, j: (0, 0)),
                  pl.BlockSpec((1, d, tn), lambda l, j: (l, 0, j)),
                  pl.BlockSpec((1, 1, tn), lambda l, j: (l, 0, j))],
        out_specs=pl.BlockSpec((1, g, tn), lambda l, j: (l, 0, j)),
        compiler_params=_params("arbitrary", "arbitrary"),
        name="ada",
    )(c_all, w_ada, b_ada.reshape(depth, 1, n))


N_KV_STATES = 4


def _inproj_kernel(x_ref, sc_ref, sh_ref, wtok_ref, wqkvt_ref, bf_ref, *rest):
    conv_ref, lf_ref, fqt_ref, mqt_ref, fkt_ref, fvt_ref, mkt_ref, mvt_ref = rest[len(rest) - 8:]
    h = (x_ref[0] * (1.0 + sc_ref[0]) + sh_ref[0]).astype(BF16)
    conv_ref[0] = _dot(h, wtok_ref[:, 0:3 * CONV_W])
    lf_ref[0] = _log_sigmoid(_dot(h, wtok_ref[:, 3 * CONV_W:]) + bf_ref[...])
    for c, ref in enumerate((fqt_ref, fkt_ref, fvt_ref, mqt_ref, mkt_ref, mvt_ref)):
        t = _dot_nt(wqkvt_ref[c * HEADS_W:(c + 1) * HEADS_W, :], h)
        ref[0] = t.reshape(N_HEADS, HEAD_DIM, t.shape[-1])


def _inproj(layer, depth, x3, sc, sh, wtok, wqkvt, bf, kv_states, tm):
    gb, t, d = x3.shape
    tok = lambda w: pl.BlockSpec((1, tm, w), lambda g, i: (g, i, 0))
    qt = pl.BlockSpec((1, N_HEADS, HEAD_DIM, tm), lambda g, i: (g, 0, 0, i))
    kvt = pl.BlockSpec((None, 1, N_HEADS, HEAD_DIM, tm), lambda g, i: (layer, g, 0, 0, i))
    qt_shape = jax.ShapeDtypeStruct((gb, N_HEADS, HEAD_DIM, t), F32)
    kvt_shape = jax.ShapeDtypeStruct((depth, gb, N_HEADS, HEAD_DIM, t), F32)
    in_specs = [tok(d), _mod_spec(sc, tm), _mod_spec(sh, tm),
                _const_spec(wtok.shape), _const_spec(wqkvt.shape), _const_spec(bf.shape)]
    args = [x3, sc, sh, wtok, wqkvt, bf]
    aliases = {}
    if kv_states is not None:
        aliases = {len(args) + k: 4 + k for k in range(N_KV_STATES)}
        in_specs += [pl.BlockSpec(memory_space=pl.ANY)] * N_KV_STATES
        args += list(kv_states)
    return pl.pallas_call(
        _inproj_kernel,
        out_shape=(jax.ShapeDtypeStruct((gb, t, 3 * CONV_W), F32),
                   jax.ShapeDtypeStruct((gb, t, V7X_LANES), F32),
                   qt_shape, qt_shape, kvt_shape, kvt_shape, kvt_shape, kvt_shape),
        grid=(gb, t // tm),
        in_specs=in_specs,
        out_specs=(tok(3 * CONV_W), tok(V7X_LANES), qt, qt, kvt, kvt, kvt, kvt),
        input_output_aliases=aliases,
        compiler_params=_params("arbitrary", "arbitrary"),
        name="inproj",
    )(*args)


def _augment_keys(k_tok, key_term, lane):
    hi, mid, lo = _split3(key_term)
    x = jnp.where(lane < HEAD_DIM, jnp.concatenate([k_tok, jnp.zeros_like(k_tok)], axis=1),
                  jnp.where(lane < AUG_END, 1.0, 0.0))
    x = jnp.where(lane == AUG_KEY_TERM, hi, x)
    x = jnp.where(lane == AUG_KEY_TERM + 1, mid, x)
    return jnp.where(lane == AUG_KEY_TERM + 2, lo, x).astype(BF16)


def _augment_queries(qt, query_term):
    row = lax.broadcasted_iota(jnp.int32, qt.shape, 0)
    tail = jnp.where(row < AUG_QUERY_TERM - HEAD_DIM, 1.0, 0.0)
    if query_term is not None:
        for k, piece in enumerate(_split3(query_term)):
            tail = jnp.where(row == AUG_QUERY_TERM - HEAD_DIM + k, piece, tail)
    return jnp.concatenate([qt, tail], axis=0).astype(BF16)


def _flash_heads(i, tq, ka_ref, vb_ref, qa_ref, s_ref, p_ref, acc_ref, past_bias):
    key = lax.broadcasted_iota(jnp.int32, (tq, tq), 0)
    qry = lax.broadcasted_iota(jnp.int32, (tq, tq), 1)
    causal = key <= qry

    def step(tiles, carry):
        for t, (j, diag) in enumerate(tiles):
            for h in range(N_HEADS):
                s = _dot(ka_ref[j, h], qa_ref[h])
                if diag:
                    s = jnp.where(causal, s, -jnp.inf)
                elif past_bias is not None:
                    s = s + past_bias(h, j)
                s_ref[t, h] = s
        out, alphas = [], []
        for h in range(N_HEADS):
            m, l = carry[h]
            scores = [s_ref[t, h] for t in range(len(tiles))]
            m_new = m
            for s in scores:
                m_new = jnp.maximum(m_new, jnp.max(s, axis=0, keepdims=True))
            alpha = jnp.exp2(m - m_new)
            l = alpha * l
            for t, s in enumerate(scores):
                p = jnp.exp2(s - m_new)
                p_ref[t, h] = p.astype(BF16)
                l = l + jnp.sum(p, axis=0, keepdims=True)
            out.append((m_new, l))
            alphas.append(alpha)
        for h in range(N_HEADS):
            acc = alphas[h] * acc_ref[h]
            for t, (j, _) in enumerate(tiles):
                acc = acc + _dot(vb_ref[j, h], p_ref[t, h])
            acc_ref[h] = acc
        return tuple(out)

    acc_ref[...] = jnp.zeros(acc_ref.shape, F32)
    init = tuple((jnp.full((1, tq), -jnp.inf, F32), jnp.zeros((1, tq), F32)) for _ in range(N_HEADS))
    carry = lax.fori_loop(0, i // 2, lambda jj, c: step([(2 * jj, False), (2 * jj + 1, False)], c), init)
    carry = lax.cond(i % 2 == 1, lambda c: step([(i - 1, False), (i, True)], c),
                     lambda c: step([(i, True)], c), carry)
    return [(acc_ref[h] / carry[h][1]).T for h in range(N_HEADS)]


def _fox_prompt_kernel(qt_ref, kt_ref, vt_ref, lf_ref, o_ref, ka_ref, vb_ref, *work, tq, nkt):
    i = pl.program_id(1)

    @pl.when(i == 0)
    def _prepare_sequence():
        rows = lax.broadcasted_iota(jnp.int32, (tq, tq), 0)
        cols = lax.broadcasted_iota(jnp.int32, (tq, tq), 1)
        lower = (cols <= rows).astype(F32)
        lane = lax.broadcasted_iota(jnp.int32, (tq, V7X_LANES), 1)
        carry = jnp.zeros((1, V7X_LANES), F32)
        for c in range(nkt):
            sl = slice(c * tq, (c + 1) * tq)
            cum = _dot_exact(lower, lf_ref[0, sl, :]) + carry
            carry = cum[tq - 1:tq, :]
            vb_ref[c] = vt_ref[0, :, :, sl].astype(BF16)
            for h in range(N_HEADS):
                key_term = jnp.broadcast_to(-LOG2E * cum[:, h:h + 1], (tq, V7X_LANES))
                ka_ref[c, h] = _augment_keys(kt_ref[0, h, :, sl].T, key_term, lane)

    qt = qt_ref[0] * (ATTN_SCALE * LOG2E)
    qa_ref = work[0]
    for h in range(N_HEADS):
        qa_ref[h] = _augment_queries(qt[h], None)
    heads = _flash_heads(i, tq, ka_ref, vb_ref, *work, None)
    o_ref[0] = jnp.concatenate(heads, axis=1).astype(o_ref.dtype)


def _moba_prompt_kernel(qt_ref, kt_ref, vt_ref, o_ref, ka_ref, vb_ref, means_ref, *work, nb):
    blk = MOBA_BLOCK
    i = pl.program_id(1)

    @pl.when(i == 0)
    def _prepare_sequence():
        lane = lax.broadcasted_iota(jnp.int32, (blk, V7X_LANES), 1)
        pos = lax.broadcasted_iota(jnp.int32, (blk, V7X_LANES), 0)
        brow = lax.broadcasted_iota(jnp.int32, (V7X_SUBLANES, HEAD_DIM), 0)
        means = [jnp.zeros((V7X_SUBLANES, HEAD_DIM), F32) for _ in range(N_HEADS)]
        for c in range(nb):
            sl = slice(c * blk, (c + 1) * blk)
            vb_ref[c] = vt_ref[0, :, :, sl].astype(BF16)
            kpos = (pos + c * blk).astype(F32)
            for h in range(N_HEADS):
                k_tok = kt_ref[0, h, :, sl].T
                means[h] = jnp.where(brow == c, jnp.sum(k_tok, axis=0, keepdims=True) * (1.0 / blk), means[h])
                ka_ref[c, h] = _augment_keys(k_tok, (LOG2E * ALIBI_SLOPES[h]) * kpos, lane)
        for h in range(N_HEADS):
            means_ref[h] = means[h]

    q_raw = qt_ref[0]
    brow = lax.broadcasted_iota(jnp.int32, (V7X_SUBLANES, blk), 0)
    brow_f = brow.astype(F32)
    qa_ref = work[0]
    biases = []
    for h in range(N_HEADS):
        gate = jnp.where(brow < i, _dot_exact(means_ref[h], q_raw[h]), -jnp.inf)
        bias = jnp.full(gate.shape, MASKED, F32)
        for _ in range(MOBA_TOPK):
            best = jnp.max(gate, axis=0, keepdims=True)
            first = jnp.min(jnp.where(gate == best, brow_f, float(V7X_SUBLANES)), axis=0, keepdims=True)
            hit = brow_f == first
            bias = jnp.where(hit, jnp.where(best > -jnp.inf, 0.0, bias), bias)
            gate = jnp.where(hit, -jnp.inf, gate)
        biases.append(bias)
        tile_term = jnp.full((1, blk), -LOG2E * ALIBI_SLOPES[h], F32) * (i * blk).astype(F32)
        qa_ref[h] = _augment_queries(q_raw[h] * (ATTN_SCALE * LOG2E), tile_term)

    def past_bias(h, j):
        return jnp.sum(jnp.where(brow == j, biases[h], 0.0), axis=0, keepdims=True)

    heads = _flash_heads(i, blk, ka_ref, vb_ref, *work, past_bias)
    o_ref[0] = jnp.concatenate(heads, axis=1).astype(o_ref.dtype)


def _prompt_attention(kind, layer, qt, kt, vt, lf=None):
    g, _, _, t = qt.shape
    tq = MOBA_BLOCK
    nkt = t // tq
    assert nkt <= V7X_SUBLANES, "block gates are kept on one sublane group"
    qspec = pl.BlockSpec((1, N_HEADS, HEAD_DIM, tq), lambda b, i: (b, 0, 0, i))
    kvspec = pl.BlockSpec((None, 1, N_HEADS, HEAD_DIM, t), lambda b, i: (layer, b, 0, 0, 0))
    scratch = [pltpu.VMEM((nkt, N_HEADS, tq, V7X_LANES), BF16), pltpu.VMEM((nkt, N_HEADS, HEAD_DIM, tq), BF16)]
    work = [pltpu.VMEM((N_HEADS, V7X_LANES, tq), BF16), pltpu.VMEM((2, N_HEADS, tq, tq), F32),
            pltpu.VMEM((2, N_HEADS, tq, tq), BF16), pltpu.VMEM((N_HEADS, HEAD_DIM, tq), F32)]
    if kind == "fox":
        body = functools.partial(_fox_prompt_kernel, tq=tq, nkt=nkt)
        in_specs = [qspec, kvspec, kvspec, pl.BlockSpec((1, t, V7X_LANES), lambda b, i: (b, 0, 0))]
        args = (qt, kt, vt, lf)
    else:
        body = functools.partial(_moba_prompt_kernel, nb=nkt)
        in_specs = [qspec, kvspec, kvspec]
        args = (qt, kt, vt)
        scratch.append(pltpu.VMEM((N_HEADS, V7X_SUBLANES, HEAD_DIM), F32))
    return pl.pallas_call(
        body,
        out_shape=jax.ShapeDtypeStruct((g, t, HEADS_W), BF16),
        grid=(g, nkt),
        in_specs=in_specs,
        out_specs=pl.BlockSpec((1, tq, HEADS_W), lambda b, i: (b, i, 0)),
        scratch_shapes=scratch + work,
        compiler_params=_params("arbitrary", "arbitrary"),
        name=kind + "_prompt",
    )(*args)


def _page_specs(n, block, layer, pages_per_step):
    zeros = (0,) * (len(block) - 2)

    def spec(r):
        return pl.BlockSpec(block, lambda g, s, pt: (layer, pt[g, s * pages_per_step + r]) + zeros)

    return [spec(r) for r in range(n)]


QK_PACK = 256 // HEAD_DIM
KEY_CHUNK = 2 * PAGE


def _packed_queries(qh):
    z = jnp.zeros_like(qh)
    return jnp.concatenate([jnp.concatenate([qh if c == r else z for c in range(QK_PACK)], axis=1)
                            for r in range(QK_PACK)], axis=0)


def _packed_scores(q_packed, chunks):
    t = q_packed.shape[0] // QK_PACK
    out = []
    for c in range(0, len(chunks), QK_PACK):
        stacked = jnp.concatenate(chunks[c:c + QK_PACK], axis=0).astype(BF16)
        o = _dot(q_packed, stacked)
        out += [o[r * t:(r + 1) * t] for r in range(QK_PACK)]
    return out


def _key_chunks(page_refs, h):
    return [jnp.concatenate([page_refs[c + r][h] for r in range(KEY_CHUNK // PAGE)], axis=1)
            for c in range(0, len(page_refs), KEY_CHUNK // PAGE)]


def _softmax_updates(scores, state, vts):
    probs = []
    for s, (m, l, _) in zip(scores, state):
        m_new = jnp.maximum(m, jnp.max(s, axis=1, keepdims=True))
        p = jnp.exp(s - m_new)
        alpha = jnp.exp(m - m_new)
        probs.append((m_new, alpha * l + jnp.sum(p, axis=1, keepdims=True), alpha, p.astype(BF16)))
    return [(m_new, l, alpha * acc + _dot_nt(p, vt))
            for (m_new, l, alpha, p), (_, _, acc), vt in zip(probs, state, vts)]


def _fox_sample_kernel(pt_ref, q_ref, knt_ref, vnt_ref, lfn_ref, *rest, pages, n_steps):
    kp, vp, lp = rest[:pages], rest[pages:2 * pages], rest[2 * pages:3 * pages]
    o_ref, qp_ref, m_ref, l_ref, acc_ref, carry_ref = rest[3 * pages:]
    s = pl.program_id(1)
    t_new = q_ref.shape[2]

    @pl.when(s == 0)
    def _init():
        for h in range(N_HEADS):
            qp_ref[h] = _packed_queries((q_ref[0, h] * ATTN_SCALE).astype(BF16))
        m_ref[...] = jnp.full(m_ref.shape, -jnp.inf, F32)
        l_ref[...] = jnp.zeros(l_ref.shape, F32)
        acc_ref[...] = jnp.zeros(acc_ref.shape, F32)
        carry_ref[...] = jnp.zeros(carry_ref.shape, F32)

    state = [(m_ref[h][:, 0:1], l_ref[h][:, 0:1], acc_ref[h]) for h in range(N_HEADS)]
    carry = carry_ref[:, 0:1]

    rows = lax.broadcasted_iota(jnp.int32, (PAGE, PAGE), 0)
    cols = lax.broadcasted_iota(jnp.int32, (PAGE, PAGE), 1)
    upper = (rows <= cols).astype(F32)
    local = _dot_exact(jnp.concatenate([lp[r][...] for r in range(pages)], axis=0), upper)
    cums = []
    for r in range(pages):
        cum = local[r * V7X_SUBLANES:(r + 1) * V7X_SUBLANES] + carry
        cums.append(cum)
        carry = cum[:, PAGE - 1:PAGE]
    cum = jnp.concatenate(cums, axis=1)

    scores = [jnp.concatenate(_packed_scores(qp_ref[h], _key_chunks(kp, h)), axis=1) - cum[h:h + 1, :]
              for h in range(N_HEADS)]
    values = [jnp.concatenate([vp[r][h] for r in range(pages)], axis=1).astype(BF16) for h in range(N_HEADS)]
    state = _softmax_updates(scores, state, values)

    carry_ref[...] = jnp.broadcast_to(carry, carry_ref.shape)
    for h in range(N_HEADS):
        m_ref[h] = jnp.broadcast_to(state[h][0], m_ref.shape[1:])
        l_ref[h] = jnp.broadcast_to(state[h][1], l_ref.shape[1:])
        acc_ref[h] = state[h][2]

    @pl.when(s == n_steps - 1)
    def _new_tokens():
        lane = lax.broadcasted_iota(jnp.int32, (t_new, PAGE), 1)
        qi = lax.broadcasted_iota(jnp.int32, (t_new, PAGE), 0)
        cum_n = _dot_exact(lfn_ref[0], upper) + carry
        scores = [jnp.where(lane <= qi, _dot(qp_ref[h][:t_new, :HEAD_DIM], knt_ref[0, h].astype(BF16))
                            - cum_n[h:h + 1, :], -jnp.inf) for h in range(N_HEADS)]
        final = _softmax_updates(scores, state, [vnt_ref[0, h].astype(BF16) for h in range(N_HEADS)])
        o_ref[0] = jnp.concatenate([acc / l for _, l, acc in final], axis=1).astype(o_ref.dtype)


def _moba_sample_kernel(pt_ref, q_ref, knt_ref, vnt_ref, *rest, pages, n_steps, p_len):
    kp, vp = rest[:pages], rest[pages:2 * pages]
    o_ref, qp_ref, means_ref, mb_ref, lb_ref, accb_ref = rest[2 * pages:]
    blk = MOBA_BLOCK
    nb = p_len // blk
    per_step = pages * PAGE // blk
    s = pl.program_id(1)
    t_new = q_ref.shape[2]

    @pl.when(s == 0)
    def _init():
        for h in range(N_HEADS):
            qp_ref[h] = _packed_queries((q_ref[0, h] * ATTN_SCALE).astype(BF16))
        means_ref[...] = jnp.zeros(means_ref.shape, F32)
        mb_ref[...] = jnp.full(mb_ref.shape, -jnp.inf, F32)
        lb_ref[...] = jnp.zeros(lb_ref.shape, F32)

    means = [means_ref[h] for h in range(N_HEADS)]
    mb = [mb_ref[h] for h in range(N_HEADS)]
    lb = [lb_ref[h] for h in range(N_HEADS)]
    lane_q = lax.broadcasted_iota(jnp.int32, (t_new, V7X_LANES), 1)
    lane_d = lax.broadcasted_iota(jnp.int32, (HEAD_DIM, V7X_LANES), 1)
    kcol = lax.broadcasted_iota(jnp.int32, (1, blk), 1).astype(F32)

    units = [(bb, h) for bb in range(per_step) for h in range(N_HEADS)]
    block_of = lambda bb: s * per_step + bb
    scores, probs = {}, {}
    for h in range(N_HEADS):
        kblks = _key_chunks(kp, h)
        for bb, sc in enumerate(_packed_scores(qp_ref[h], kblks)):
            b = block_of(bb)
            means[h] = jnp.where(lane_d == b, jnp.sum(kblks[bb], axis=1, keepdims=True) * (1.0 / blk), means[h])
            scores[bb, h] = sc + ALIBI_SLOPES[h] * (kcol + (b * blk - p_len).astype(F32))
    for bb, h in units:
        b = block_of(bb)
        mx = jnp.max(scores[bb, h], axis=1, keepdims=True)
        p = jnp.exp(scores[bb, h] - mx)
        probs[bb, h] = p.astype(BF16)
        mb[h] = jnp.where(lane_q == b, mx, mb[h])
        lb[h] = jnp.where(lane_q == b, jnp.sum(p, axis=1, keepdims=True), lb[h])
    for h in range(N_HEADS):
        for bb, vblk in enumerate(_key_chunks(vp, h)):
            accb_ref[h, block_of(bb)] = _dot_nt(probs[bb, h], vblk.astype(BF16))

    for h in range(N_HEADS):
        means_ref[h] = means[h]
        mb_ref[h] = mb[h]
        lb_ref[h] = lb[h]

    @pl.when(s == n_steps - 1)
    def _select_and_combine():
        lane_f = lane_q.astype(F32)
        qi = lax.broadcasted_iota(jnp.int32, (t_new, V7X_LANES), 0)
        heads = []
        for h in range(N_HEADS):
            gate = jnp.where(lane_q < nb, _dot_exact(q_ref[0, h], means[h]), -jnp.inf)
            sel = jnp.zeros(gate.shape, F32)
            for _ in range(MOBA_TOPK):
                best = jnp.max(gate, axis=1, keepdims=True)
                first = jnp.min(jnp.where(gate == best, lane_f, float(V7X_LANES)), axis=1, keepdims=True)
                hit = lane_f == first
                sel = jnp.where(hit, jnp.where(best > -jnp.inf, 1.0, sel), sel)
                gate = jnp.where(hit, -jnp.inf, gate)
            sc = _dot(qp_ref[h][:t_new, :HEAD_DIM], knt_ref[0, h].astype(BF16)) + ALIBI_SLOPES[h] * lane_f
            sc = jnp.where(lane_q <= qi, sc, -jnp.inf)
            m_own = jnp.max(sc, axis=1, keepdims=True)
            p_own = jnp.exp(sc - m_own)
            l_own = jnp.sum(p_own, axis=1, keepdims=True)
            acc_own = _dot_nt(p_own.astype(BF16), vnt_ref[0, h].astype(BF16))
            picked = sel > 0.0
            m_all = jnp.maximum(m_own, jnp.max(jnp.where(picked, mb[h], -jnp.inf), axis=1, keepdims=True))
            w = jnp.where(picked, jnp.exp(mb[h] - m_all), 0.0)
            w_own = jnp.exp(m_own - m_all)
            den = jnp.sum(w * lb[h], axis=1, keepdims=True) + w_own * l_own
            num = w_own * acc_own
            for b in range(nb):
                num = num + jnp.sum(jnp.where(lane_q == b, w, 0.0), axis=1, keepdims=True) * accb_ref[h, b]
            heads.append(num / den)
        o_ref[0] = jnp.concatenate(heads, axis=1).astype(o_ref.dtype)


def _sample_attention(kind, layer, page_table, q, knt, vnt, cache_kt, cache_vt, lfn=None, cache_lft=None):
    g, _, t_new, _ = q.shape
    n_pages = page_table.shape[1]
    pages = min(PAGES_PER_STEP, n_pages)
    n_steps = n_pages // pages
    assert pages * PAGE % (QK_PACK * KEY_CHUNK) == 0 and t_new & (t_new - 1) == 0
    per_seq = lambda blk: pl.BlockSpec(blk, lambda b, s, pt: (b,) + (0,) * (len(blk) - 1))
    page_blk = (None, None, N_HEADS, HEAD_DIM, PAGE)
    in_specs = [per_seq((1, N_HEADS, t_new, HEAD_DIM)), per_seq((1, N_HEADS, HEAD_DIM, PAGE)),
                per_seq((1, N_HEADS, HEAD_DIM, PAGE))]
    args = [q, knt, vnt]
    if kind == "fox":
        in_specs.append(per_seq((1, V7X_SUBLANES, PAGE)))
        args.append(lfn)
    in_specs += _page_specs(pages, page_blk, layer, pages) + _page_specs(pages, page_blk, layer, pages)
    args += [cache_kt] * pages + [cache_vt] * pages
    stat = pltpu.VMEM((N_HEADS, t_new, V7X_LANES), F32)
    q_packed = pltpu.VMEM((N_HEADS, QK_PACK * t_new, QK_PACK * HEAD_DIM), BF16)
    if kind == "fox":
        in_specs += _page_specs(pages, (None, None, V7X_SUBLANES, PAGE), layer, pages)
        args += [cache_lft] * pages
        body = functools.partial(_fox_sample_kernel, pages=pages, n_steps=n_steps)
        scratch = [q_packed, stat, stat, pltpu.VMEM((N_HEADS, t_new, HEAD_DIM), F32),
                   pltpu.VMEM((V7X_SUBLANES, V7X_LANES), F32)]
    else:
        p_len = n_pages * PAGE
        assert p_len // MOBA_BLOCK <= V7X_LANES and MOBA_BLOCK == KEY_CHUNK
        body = functools.partial(_moba_sample_kernel, pages=pages, n_steps=n_steps, p_len=p_len)
        scratch = [q_packed, pltpu.VMEM((N_HEADS, HEAD_DIM, V7X_LANES), F32), stat, stat,
                   pltpu.VMEM((N_HEADS, p_len // MOBA_BLOCK, t_new, HEAD_DIM), F32)]
    return pl.pallas_call(
        body,
        out_shape=jax.ShapeDtypeStruct((g, t_new, HEADS_W), BF16),
        grid_spec=pltpu.PrefetchScalarGridSpec(
            num_scalar_prefetch=1,
            grid=(g, n_steps),
            in_specs=in_specs,
            out_specs=pl.BlockSpec((1, t_new, HEADS_W), lambda b, s, pt: (b, 0, 0)),
            scratch_shapes=scratch),
        compiler_params=_params("arbitrary", "arbitrary"),
        name=kind + "_sample",
    )(page_table, *args)


def _causal_shifts(u, fill, period):
    out = []
    for j in (1, 2):
        r = pltpu.roll(u, j, axis=0)
        f = pltpu.roll(fill, j, axis=0)
        if period is None:
            row = lax.broadcasted_iota(jnp.int32, f.shape, 0)
            out.append(jnp.concatenate([jnp.where(row >= j, r[:V7X_SUBLANES], f), r[V7X_SUBLANES:]], axis=0))
        else:
            t = lax.broadcasted_iota(jnp.int32, u.shape, 0) & (period - 1)
            out.append(jnp.where(t >= j, r, f))
    return out


def _conv3(u, fill, period, w_ref, cols):
    s1, s2 = _causal_shifts(u, fill, period)
    return w_ref[0:1, cols] * s2 + w_ref[1:2, cols] * s1 + w_ref[2:3, cols] * u


def _outproj_kernel(x_ref, pc_ref, yf_ref, ym_ref, gt_ref, cw_ref, wo_ref, g_ref, b_ref, *rest,
                    period, dn_alpha):
    if period is None:
        o_ref, us_ref, carry_ref = rest

        @pl.when(pl.program_id(1) == 0)
        def _fresh_sequence():
            carry_ref[...] = jnp.zeros(carry_ref.shape, F32)

        fill = carry_ref[...]
    else:
        fill_ref, o_ref, us_ref = rest
        fill = fill_ref[0]
    pc = pc_ref[0]
    u = pc[:, CONV_W:2 * CONV_W] * pc[:, 2 * CONV_W:]
    y_conv = pc[:, :CONV_W] * _conv3(u, fill, period, cw_ref, slice(None))
    mixed = (_dot(y_conv.astype(BF16), wo_ref[0:CONV_W, :])
             + _dot(yf_ref[0], wo_ref[CONV_W:CONV_W + HEADS_W, :])
             + _dot(ym_ref[0], wo_ref[CONV_W + HEADS_W:, :]))
    z = dn_alpha * x_ref[0] + (1.0 + gt_ref[0]) * mixed
    o_ref[0] = _layer_norm(z, g_ref[...], b_ref[...])
    if period is None:
        tail = u[u.shape[0] - V7X_SUBLANES:]
        carry_ref[...] = tail
        us_ref[0] = tail
    else:
        us_ref[0] = u


def _outproj(x3, pconv, yf, ym, gt, conv_w, wo, ln_g, ln_b, fill, tm, dn_alpha):
    gb, t, d = x3.shape
    period = None if fill is None else V7X_SUBLANES
    tok = lambda w: pl.BlockSpec((1, tm, w), lambda g, i: (g, i, 0))
    in_specs = [tok(d), tok(3 * CONV_W), tok(HEADS_W), tok(HEADS_W), _mod_spec(gt, tm),
                _const_spec(conv_w.shape), _const_spec(wo.shape), _const_spec(ln_g.shape), _const_spec(ln_b.shape)]
    args = [x3, pconv, yf, ym, gt, conv_w, wo, ln_g, ln_b]
    scratch = []
    if period is None:
        us_rows = V7X_SUBLANES
        us_spec = pl.BlockSpec((1, us_rows, CONV_W), lambda g, i: (g, 0, 0))
        scratch.append(pltpu.VMEM((V7X_SUBLANES, CONV_W), F32))
    else:
        us_rows = t
        us_spec = tok(CONV_W)
        in_specs.append(tok(CONV_W))
        args.append(fill)
    return pl.pallas_call(
        functools.partial(_outproj_kernel, period=period, dn_alpha=dn_alpha),
        out_shape=(jax.ShapeDtypeStruct((gb, t, d), F32), jax.ShapeDtypeStruct((gb, us_rows, CONV_W), F32)),
        grid=(gb, t // tm),
        in_specs=in_specs,
        out_specs=(tok(d), us_spec),
        scratch_shapes=scratch,
        compiler_params=_params("arbitrary", "arbitrary"),
        name="outproj",
    )(*args)


def _ffn_kernel(x_ref, sc_ref, sh_ref, gt_ref, wup_ref, cw_ref, cb_ref, wdn_ref, g_ref, b_ref, *rest,
                period, dn_alpha):
    if period is None:
        o_ref, us_ref, carry_ref = rest

        @pl.when(pl.program_id(1) == 0)
        def _fresh_sequence():
            carry_ref[...] = jnp.zeros(carry_ref.shape, F32)
    else:
        fill_ref, o_ref, us_ref = rest
    x = x_ref[0]
    tm = x.shape[0]
    d_ff = wdn_ref.shape[0]
    h = (x * (1.0 + sc_ref[0]) + sh_ref[0]).astype(BF16)
    halves = []
    for half in range(2):
        cols = slice(half * d_ff, (half + 1) * d_ff)
        u = _dot(h, wup_ref[:, cols])
        fill = carry_ref[:, cols] if period is None else fill_ref[0, :, cols]
        halves.append(_conv3(u, fill, period, cw_ref, cols) + cb_ref[:, cols])
        if period is None:
            tail = u[tm - V7X_SUBLANES:]
            carry_ref[:, cols] = tail
            us_ref[0, :, cols] = tail
        else:
            us_ref[0, :, cols] = u
    gate, val = halves
    y = _dot((gate * _sigmoid(gate) * val).astype(BF16), wdn_ref[...])
    z = dn_alpha * x + (1.0 + gt_ref[0]) * y
    o_ref[0] = _layer_norm(z, g_ref[...], b_ref[...])


def _ffn(x3, sc, sh, gt, wup, conv_w, conv_b, wdn, ln_g, ln_b, fill, tm, dn_alpha):
    gb, t, d = x3.shape
    n_up = wup.shape[1]
    period = None if fill is None else V7X_SUBLANES
    tok = lambda w: pl.BlockSpec((1, tm, w), lambda g, i: (g, i, 0))
    in_specs = [tok(d), _mod_spec(sc, tm), _mod_spec(sh, tm), _mod_spec(gt, tm),
                _const_spec(wup.shape), _const_spec(conv_w.shape), _const_spec(conv_b.shape),
                _const_spec(wdn.shape), _const_spec(ln_g.shape), _const_spec(ln_b.shape)]
    args = [x3, sc, sh, gt, wup, conv_w, conv_b, wdn, ln_g, ln_b]
    scratch = []
    if period is None:
        us_rows = V7X_SUBLANES
        us_spec = pl.BlockSpec((1, us_rows, n_up), lambda g, i: (g, 0, 0))
        scratch.append(pltpu.VMEM((V7X_SUBLANES, n_up), F32))
    else:
        us_rows = t
        us_spec = tok(n_up)
        in_specs.append(tok(n_up))
        args.append(fill)
    return pl.pallas_call(
        functools.partial(_ffn_kernel, period=period, dn_alpha=dn_alpha),
        out_shape=(jax.ShapeDtypeStruct((gb, t, d), F32), jax.ShapeDtypeStruct((gb, us_rows, n_up), F32)),
        grid=(gb, t // tm),
        in_specs=in_specs,
        out_specs=(tok(d), us_spec),
        scratch_shapes=scratch,
        compiler_params=_params("arbitrary", "arbitrary"),
        name="ffn",
    )(*args)


def _seq_fill(state):
    g, w, c = state.shape
    padded = jnp.pad(state, ((0, 0), (V7X_SUBLANES - w, 0), (0, 0)))
    return jnp.roll(padded, -1, axis=0).reshape(1, g * V7X_SUBLANES, c)


def _per_seq_transposed(xt, g, t_new):
    return xt.reshape(N_HEADS, HEAD_DIM, g, t_new).transpose(2, 0, 1, 3)


def _new_token_blocks(xt, g, t_new):
    return jnp.pad(_per_seq_transposed(xt, g, t_new), ((0, 0), (0, 0), (0, 0), (0, PAGE - t_new)))


def _layer(l, depth, x3, mods, weights, kv_states, dn_alpha, tm, sample=None):
    sh_a, sc_a, gt_a, sh_f, sc_f, gt_f = mods
    (wtok, wqkvt, bf, conv_w, wo, ln1_g, ln1_b, wup, ffn_w, ffn_b, wdn, ln2_g, ln2_b) = weights
    pconv, lf, fqt, mqt, *kv_states = _inproj(l, depth, x3, sc_a, sh_a, wtok, wqkvt, bf, kv_states, tm)
    fkt, fvt, mkt, mvt = kv_states
    if sample is None:
        y_fox = _prompt_attention("fox", l, fqt, fkt, fvt, lf)
        y_moba = _prompt_attention("moba", l, mqt, mkt, mvt)
        conv_fill = ffn_fill = None
    else:
        g, t_new = sample["g"], sample["t"]
        tok_q = lambda qt: _per_seq_transposed(qt[0], g, t_new).transpose(0, 1, 3, 2)
        new = lambda st: _new_token_blocks(st[l, 0], g, t_new)
        lfn = lf[0].reshape(g, t_new, V7X_LANES)[:, :, :V7X_SUBLANES].transpose(0, 2, 1)
        lfn = jnp.pad(lfn, ((0, 0), (0, 0), (0, PAGE - t_new)))
        y_fox = _sample_attention("fox", l, sample["page_table"], tok_q(fqt), new(fkt), new(fvt),
                                  sample["fox_kt"], sample["fox_vt"], lfn, sample["fox_lft"])
        y_moba = _sample_attention("moba", l, sample["page_table"], tok_q(mqt), new(mkt), new(mvt),
                                   sample["moba_kt"], sample["moba_vt"])
        y_fox = y_fox.reshape(1, g * t_new, HEADS_W)
        y_moba = y_moba.reshape(1, g * t_new, HEADS_W)
        conv_fill = _seq_fill(sample["state_conv"][l])
        ffn_fill = _seq_fill(sample["state_ffn"][l])
    x1, conv_tail = _outproj(x3, pconv, y_fox, y_moba, gt_a, conv_w, wo, ln1_g, ln1_b, conv_fill, tm, dn_alpha)
    x2, ffn_tail = _ffn(x1, sc_f, sh_f, gt_f, wup, ffn_w, ffn_b, wdn, ln2_g, ln2_b, ffn_fill, tm, dn_alpha)
    return x2, kv_states, (lf, conv_tail, ffn_tail)


def kernel(x_prompt, x_sample, cache_fox_k, cache_fox_v, cache_fox_logf, cache_moba_k, cache_moba_v, state_conv, state_ffn, page_table, c_prompt, c_sample, w_ada, b_ada, w_in, b_f, conv_w, w_o, ln1_g, ln1_b, w_up, ffn_conv_w, ffn_conv_b, w_down, ln2_g, ln2_b):
    depth, d, _ = w_ada.shape
    gp, tp, _ = x_prompt.shape
    gs, ts, _ = x_sample.shape
    dn_alpha = (2 * depth) ** 0.25
    n_conv, n_qkv = 3 * CONV_W, 3 * HEADS_W

    ada = _ada(jnp.concatenate([c_prompt, c_sample], axis=0), w_ada, b_ada)
    to_kt = lambda c: jnp.transpose(c, (0, 1, 3, 4, 2))
    lft_cache = jnp.pad(jnp.transpose(cache_fox_logf, (0, 1, 3, 2)),
                        ((0, 0), (0, 0), (0, V7X_SUBLANES - N_HEADS), (0, 0)))
    sample = dict(g=gs, t=ts, page_table=page_table, state_conv=state_conv, state_ffn=state_ffn,
                  fox_kt=to_kt(cache_fox_k), fox_vt=to_kt(cache_fox_v), fox_lft=lft_cache,
                  moba_kt=to_kt(cache_moba_k), moba_vt=to_kt(cache_moba_v))

    xp = x_prompt
    xs = x_sample.reshape(1, gs * ts, d)
    kv_p = kv_s = None
    small_p, small_s = [], []
    for l in range(depth):
        w = w_in[l]
        fox0, ff0, moba0 = n_conv, n_conv + n_qkv, n_conv + n_qkv + N_HEADS
        wtok = jnp.concatenate([w[:, :n_conv], jnp.pad(w[:, ff0:moba0], ((0, 0), (0, V7X_LANES - N_HEADS)))],
                               axis=1).astype(BF16)
        wqkvt = jnp.concatenate([w[:, fox0:ff0], w[:, moba0:]], axis=1).T.astype(BF16)
        bf = jnp.pad(b_f[l], (0, V7X_LANES - N_HEADS))[None]
        weights = (wtok, wqkvt, bf, conv_w[l], w_o[l].astype(BF16), ln1_g[l][None], ln1_b[l][None],
                   w_up[l].astype(BF16), ffn_conv_w[l], ffn_conv_b[l][None], w_down[l].astype(BF16),
                   ln2_g[l][None], ln2_b[l][None])
        mods_p = tuple(a[:, None, :] for a in jnp.split(ada[l, :gp], 6, axis=-1))
        mods_s = tuple(jnp.repeat(a, ts, axis=0)[None] for a in jnp.split(ada[l, gp:], 6, axis=-1))
        xp, kv_p, st_p = _layer(l, depth, xp, mods_p, weights, kv_p, dn_alpha, tm=512)
        xs, kv_s, st_s = _layer(l, depth, xs, mods_s, weights, kv_s, dn_alpha, tm=gs * ts, sample=sample)
        small_p.append(st_p)
        small_s.append(st_s)

    stack = lambda states, i: jnp.stack([s[i] for s in states])
    out = [xp, xs.reshape(gs, ts, d)]
    fk, fv, mk, mv = (jnp.transpose(a, (0, 1, 4, 2, 3)) for a in kv_p)
    out += [fk, fv, stack(small_p, 0)[..., :N_HEADS], mk, mv,
            stack(small_p, 1)[:, :, V7X_SUBLANES - 2:], stack(small_p, 2)[:, :, V7X_SUBLANES - 2:]]
    fk, fv, mk, mv = (jnp.transpose(a[:, 0].reshape(depth, N_HEADS, HEAD_DIM, gs, ts), (0, 3, 4, 1, 2))
                      for a in kv_s)
    tail_s = lambda i: stack(small_s, i).reshape(depth, gs, ts, -1)[:, :, ts - 2:]
    out += [fk, fv, stack(small_s, 0).reshape(depth, gs, ts, V7X_LANES)[..., :N_HEADS], mk, mv,
            tail_s(1), tail_s(2)]
    return tuple(out)
```

```python
import functools

import jax
import jax.numpy as jnp
from jax import lax
from jax.experimental import pallas as pl
from jax.experimental.pallas import tpu as pltpu

F32 = jnp.float32
BF16 = jnp.bfloat16
HIGHEST = lax.Precision.HIGHEST

HEAD_DIM = 64
N_HEADS = 6
HEADS_W = N_HEADS * HEAD_DIM
CONV_W = 256
MOBA_BLOCK = 256
MOBA_TOPK = 3
PAGE = 128
LN_EPS = 1e-5
ATTN_SCALE = HEAD_DIM ** -0.5
LOG2E = 1.4426950408889634
ALIBI_SLOPES = tuple(2.0 ** (-8.0 * (h + 1) / N_HEADS) for h in range(N_HEADS))
MASKED = -1e30

V7X_LANES = 128
V7X_SUBLANES = 8
V7X_VMEM_BYTES = 64 * 1024 * 1024
VMEM_LIMIT = 52 * 1024 * 1024
PAGES_PER_STEP = 16
AUG_KEY_TERM = HEAD_DIM
AUG_QUERY_TERM = HEAD_DIM + 3
AUG_END = HEAD_DIM + 6


def _dot(a, b):
    return jnp.dot(a, b, preferred_element_type=F32)


def _dot_nt(a, b):
    return lax.dot_general(a, b, (((1,), (1,)), ((), ())), preferred_element_type=F32)


def _dot_exact(a, b):
    return jnp.dot(a, b, precision=HIGHEST, preferred_element_type=F32)


def _sigmoid(x):
    return 1.0 / (1.0 + jnp.exp(-x))


def _log_sigmoid(x):
    return jnp.minimum(x, 0.0) - jnp.log(1.0 + jnp.exp(-jnp.abs(x)))


def _layer_norm(z, g, b):
    mu = jnp.mean(z, axis=-1, keepdims=True)
    zc = z - mu
    var = jnp.mean(zc * zc, axis=-1, keepdims=True)
    return zc * lax.rsqrt(var + LN_EPS) * g + b


def _split3(x):
    hi = x.astype(BF16).astype(F32)
    rest = x - hi
    mid = rest.astype(BF16).astype(F32)
    return hi, mid, rest - mid


def _params(*semantics):
    return pltpu.CompilerParams(dimension_semantics=semantics, vmem_limit_bytes=VMEM_LIMIT)


def _const_spec(shape):
    zeros = (0,) * len(shape)
    return pl.BlockSpec(shape, lambda *_: zeros)


def _mod_spec(arr, tm):
    d = arr.shape[-1]
    if arr.shape[1] == 1:
        return pl.BlockSpec((1, 1, d), lambda g, i: (g, 0, 0))
    return pl.BlockSpec((1, tm, d), lambda g, i: (g, i, 0))


def _ada_kernel(c_ref, w_ref, b_ref, o_ref):
    c = c_ref[...]
    o_ref[0] = _dot_exact(c * _sigmoid(c), w_ref[0]) + b_ref[0]


def _ada(c_all, w_ada, b_ada):
    depth, d, n = w_ada.shape
    g = c_all.shape[0]
    tn = 1536
    return pl.pallas_call(
        _ada_kernel,
        out_shape=jax.ShapeDtypeStruct((depth, g, n), F32),
        grid=(depth, n // tn),
        in_specs=[pl.BlockSpec((g, d), lambda l, j: (0, 0)),
                  pl.BlockSpec((1, d, tn), lambda l, j: (l, 0, j)),
                  pl.BlockSpec((1, 1, tn), lambda l, j: (l, 0, j))],
        out_specs=pl.BlockSpec((1, g, tn), lambda l, j: (l, 0, j)),
        compiler_params=_params("arbitrary", "arbitrary"),
        name="ada",
    )(c_all, w_ada, b_ada.reshape(depth, 1, n))


N_KV_STATES = 4


def _inproj_kernel(x_ref, sc_ref, sh_ref, wtok_ref, wqkvt_ref, bf_ref, *rest):
    conv_ref, lf_ref, fqt_ref, mqt_ref, fkt_ref, fvt_ref, mkt_ref, mvt_ref = rest[len(rest) - 8:]
    h = (x_ref[0] * (1.0 + sc_ref[0]) + sh_ref[0]).astype(BF16)
    conv_ref[0] = _dot(h, wtok_ref[:, 0:3 * CONV_W])
    lf_ref[0] = _log_sigmoid(_dot(h, wtok_ref[:, 3 * CONV_W:]) + bf_ref[...])
    for c, ref in enumerate((fqt_ref, fkt_ref, fvt_ref, mqt_ref, mkt_ref, mvt_ref)):
        t = _dot_nt(wqkvt_ref[c * HEADS_W:(c + 1) * HEADS_W, :], h)
        ref[0] = t.reshape(N_HEADS, HEAD_DIM, t.shape[-1])


def _inproj(layer, depth, x3, sc, sh, wtok, wqkvt, bf, kv_states, tm):
    gb, t, d = x3.shape
    tok = lambda w: pl.BlockSpec((1, tm, w), lambda g, i: (g, i, 0))
    qt = pl.BlockSpec((1, N_HEADS, HEAD_DIM, tm), lambda g, i: (g, 0, 0, i))
    kvt = pl.BlockSpec((None, 1, N_HEADS, HEAD_DIM, tm), lambda g, i: (layer, g, 0, 0, i))
    qt_shape = jax.ShapeDtypeStruct((gb, N_HEADS, HEAD_DIM, t), F32)
    kvt_shape = jax.ShapeDtypeStruct((depth, gb, N_HEADS, HEAD_DIM, t), F32)
    in_specs = [tok(d), _mod_spec(sc, tm), _mod_spec(sh, tm),
                _const_spec(wtok.shape), _const_spec(wqkvt.shape), _const_spec(bf.shape)]
    args = [x3, sc, sh, wtok, wqkvt, bf]
    aliases = {}
    if kv_states is not None:
        aliases = {len(args) + k: 4 + k for k in range(N_KV_STATES)}
        in_specs += [pl.BlockSpec(memory_space=pl.ANY)] * N_KV_STATES
        args += list(kv_states)
    return pl.pallas_call(
        _inproj_kernel,
        out_shape=(jax.ShapeDtypeStruct((gb, t, 3 * CONV_W), F32),
                   jax.ShapeDtypeStruct((gb, t, V7X_LANES), F32),
                   qt_shape, qt_shape, kvt_shape, kvt_shape, kvt_shape, kvt_shape),
        grid=(gb, t // tm),
        in_specs=in_specs,
        out_specs=(tok(3 * CONV_W), tok(V7X_LANES), qt, qt, kvt, kvt, kvt, kvt),
        input_output_aliases=aliases,
        compiler_params=_params("arbitrary", "arbitrary"),
        name="inproj",
    )(*args)


def _augment_keys(k_tok, key_term, lane):
    hi, mid, lo = _split3(key_term)
    x = jnp.where(lane < HEAD_DIM, jnp.concatenate([k_tok, jnp.zeros_like(k_tok)], axis=1),
                  jnp.where(lane < AUG_END, 1.0, 0.0))
    x = jnp.where(lane == AUG_KEY_TERM, hi, x)
    x = jnp.where(lane == AUG_KEY_TERM + 1, mid, x)
    return jnp.where(lane == AUG_KEY_TERM + 2, lo, x).astype(BF16)


def _augment_queries(qt, query_term):
    row = lax.broadcasted_iota(jnp.int32, qt.shape, 0)
    tail = jnp.where(row < AUG_QUERY_TERM - HEAD_DIM, 1.0, 0.0)
    if query_term is not None:
        for k, piece in enumerate(_split3(query_term)):
            tail = jnp.where(row == AUG_QUERY_TERM - HEAD_DIM + k, piece, tail)
    return jnp.concatenate([qt, tail], axis=0).astype(BF16)


def _flash_heads(i, tq, ka_ref, vb_ref, qa_ref, s_ref, p_ref, acc_ref, past_bias):
    key = lax.broadcasted_iota(jnp.int32, (tq, tq), 0)
    qry = lax.broadcasted_iota(jnp.int32, (tq, tq), 1)
    causal = key <= qry

    def step(tiles, carry):
        for t, (j, diag) in enumerate(tiles):
            for h in range(N_HEADS):
                s = _dot(ka_ref[j, h], qa_ref[h])
                if diag:
                    s = jnp.where(causal, s, -jnp.inf)
                elif past_bias is not None:
                    s = s + past_bias(h, j)
                s_ref[t, h] = s
        out, alphas = [], []
        for h in range(N_HEADS):
            m, l = carry[h]
            scores = [s_ref[t, h] for t in range(len(tiles))]
            m_new = m
            for s in scores:
                m_new = jnp.maximum(m_new, jnp.max(s, axis=0, keepdims=True))
            alpha = jnp.exp2(m - m_new)
            l = alpha * l
            for t, s in enumerate(scores):
                p = jnp.exp2(s - m_new)
                p_ref[t, h] = p.astype(BF16)
                l = l + jnp.sum(p, axis=0, keepdims=True)
            out.append((m_new, l))
            alphas.append(alpha)
        for h in range(N_HEADS):
            acc = alphas[h] * acc_ref[h]
            for t, (j, _) in enumerate(tiles):
                acc = acc + _dot(vb_ref[j, h], p_ref[t, h])
            acc_ref[h] = acc
        return tuple(out)

    acc_ref[...] = jnp.zeros(acc_ref.shape, F32)
    init = tuple((jnp.full((1, tq), -jnp.inf, F32), jnp.zeros((1, tq), F32)) for _ in range(N_HEADS))
    carry = lax.fori_loop(0, i // 2, lambda jj, c: step([(2 * jj, False), (2 * jj + 1, False)], c), init)
    carry = lax.cond(i % 2 == 1, lambda c: step([(i - 1, False), (i, True)], c),
                     lambda c: step([(i, True)], c), carry)
    return [(acc_ref[h] / carry[h][1]).T for h in range(N_HEADS)]


def _fox_prompt_kernel(qt_ref, kt_ref, vt_ref, lf_ref, o_ref, ka_ref, vb_ref, *work, tq, nkt):
    i = pl.program_id(1)

    @pl.when(i == 0)
    def _prepare_sequence():
        rows = lax.broadcasted_iota(jnp.int32, (tq, tq), 0)
        cols = lax.broadcasted_iota(jnp.int32, (tq, tq), 1)
        lower = (cols <= rows).astype(F32)
        lane = lax.broadcasted_iota(jnp.int32, (tq, V7X_LANES), 1)
        carry = jnp.zeros((1, V7X_LANES), F32)
        for c in range(nkt):
            sl = slice(c * tq, (c + 1) * tq)
            cum = _dot_exact(lower, lf_ref[0, sl, :]) + carry
            carry = cum[tq - 1:tq, :]
            vb_ref[c] = vt_ref[0, :, :, sl].astype(BF16)
            for h in range(N_HEADS):
                key_term = jnp.broadcast_to(-LOG2E * cum[:, h:h + 1], (tq, V7X_LANES))
                ka_ref[c, h] = _augment_keys(kt_ref[0, h, :, sl].T, key_term, lane)

    qt = qt_ref[0] * (ATTN_SCALE * LOG2E)
    qa_ref = work[0]
    for h in range(N_HEADS):
        qa_ref[h] = _augment_queries(qt[h], None)
    heads = _flash_heads(i, tq, ka_ref, vb_ref, *work, None)
    o_ref[0] = jnp.concatenate(heads, axis=1).astype(o_ref.dtype)


def _moba_prompt_kernel(qt_ref, kt_ref, vt_ref, o_ref, ka_ref, vb_ref, means_ref, *work, nb):
    blk = MOBA_BLOCK
    i = pl.program_id(1)

    @pl.when(i == 0)
    def _prepare_sequence():
        lane = lax.broadcasted_iota(jnp.int32, (blk, V7X_LANES), 1)
        pos = lax.broadcasted_iota(jnp.int32, (blk, V7X_LANES), 0)
        brow = lax.broadcasted_iota(jnp.int32, (V7X_SUBLANES, HEAD_DIM), 0)
        means = [jnp.zeros((V7X_SUBLANES, HEAD_DIM), F32) for _ in range(N_HEADS)]
        for c in range(nb):
            sl = slice(c * blk, (c + 1) * blk)
            vb_ref[c] = vt_ref[0, :, :, sl].astype(BF16)
            kpos = (pos + c * blk).astype(F32)
            for h in range(N_HEADS):
                k_tok = kt_ref[0, h, :, sl].T
                means[h] = jnp.where(brow == c, jnp.sum(k_tok, axis=0, keepdims=True) * (1.0 / blk), means[h])
                ka_ref[c, h] = _augment_keys(k_tok, (LOG2E * ALIBI_SLOPES[h]) * kpos, lane)
        for h in range(N_HEADS):
            means_ref[h] = means[h]

    q_raw = qt_ref[0]
    brow = lax.broadcasted_iota(jnp.int32, (V7X_SUBLANES, blk), 0)
    brow_f = brow.astype(F32)
    qa_ref = work[0]
    biases = []
    for h in range(N_HEADS):
        gate = jnp.where(brow < i, _dot_exact(means_ref[h], q_raw[h]), -jnp.inf)
        bias = jnp.full(gate.shape, MASKED, F32)
        for _ in range(MOBA_TOPK):
            best = jnp.max(gate, axis=0, keepdims=True)
            first = jnp.min(jnp.where(gate == best, brow_f, float(V7X_SUBLANES)), axis=0, keepdims=True)
            hit = brow_f == first
            bias = jnp.where(hit, jnp.where(best > -jnp.inf, 0.0, bias), bias)
            gate = jnp.where(hit, -jnp.inf, gate)
        biases.append(bias)
        tile_term = jnp.full((1, blk), -LOG2E * ALIBI_SLOPES[h], F32) * (i * blk).astype(F32)
        qa_ref[h] = _augment_queries(q_raw[h] * (ATTN_SCALE * LOG2E), tile_term)

    def past_bias(h, j):
        return jnp.sum(jnp.where(brow == j, biases[h], 0.0), axis=0, keepdims=True)

    heads = _flash_heads(i, blk, ka_ref, vb_ref, *work, past_bias)
    o_ref[0] = jnp.concatenate(heads, axis=1).astype(o_ref.dtype)


def _prompt_attention(kind, layer, qt, kt, vt, lf=None):
    g, _, _, t = qt.shape
    tq = MOBA_BLOCK
    nkt = t // tq
    assert nkt <= V7X_SUBLANES, "block gates are kept on one sublane group"
    qspec = pl.BlockSpec((1, N_HEADS, HEAD_DIM, tq), lambda b, i: (b, 0, 0, i))
    kvspec = pl.BlockSpec((None, 1, N_HEADS, HEAD_DIM, t), lambda b, i: (layer, b, 0, 0, 0))
    scratch = [pltpu.VMEM((nkt, N_HEADS, tq, V7X_LANES), BF16), pltpu.VMEM((nkt, N_HEADS, HEAD_DIM, tq), BF16)]
    work = [pltpu.VMEM((N_HEADS, V7X_LANES, tq), BF16), pltpu.VMEM((2, N_HEADS, tq, tq), F32),
            pltpu.VMEM((2, N_HEADS, tq, tq), BF16), pltpu.VMEM((N_HEADS, HEAD_DIM, tq), F32)]
    if kind == "fox":
        body = functools.partial(_fox_prompt_kernel, tq=tq, nkt=nkt)
        in_specs = [qspec, kvspec, kvspec, pl.BlockSpec((1, t, V7X_LANES), lambda b, i: (b, 0, 0))]
        args = (qt, kt, vt, lf)
    else:
        body = functools.partial(_moba_prompt_kernel, nb=nkt)
        in_specs = [qspec, kvspec, kvspec]
        args = (qt, kt, vt)
        scratch.append(pltpu.VMEM((N_HEADS, V7X_SUBLANES, HEAD_DIM), F32))
    return pl.pallas_call(
        body,
        out_shape=jax.ShapeDtypeStruct((g, t, HEADS_W), BF16),
        grid=(g, nkt),
        in_specs=in_specs,
        out_specs=pl.BlockSpec((1, tq, HEADS_W), lambda b, i: (b, i, 0)),
        scratch_shapes=scratch + work,
        compiler_params=_params("arbitrary", "arbitrary"),
        name=kind + "_prompt",
    )(*args)


def _page_specs(n, block, layer, pages_per_step):
    zeros = (0,) * (len(block) - 2)

    def spec(r):
        return pl.BlockSpec(block, lambda g, s, pt: (layer, pt[g, s * pages_per_step + r]) + zeros)

    return [spec(r) for r in range(n)]


def _all_heads(page_ref):
    return page_ref[...].reshape(HEADS_W, page_ref.shape[-1])


def _block_diag_queries(q):
    z = jnp.zeros(q.shape[1:], q.dtype)
    return jnp.concatenate([jnp.concatenate([q[h] if c == h else z for c in range(N_HEADS)], axis=1)
                            for h in range(N_HEADS)], axis=0)


def _own_head_columns(x, t):
    lane = lax.broadcasted_iota(jnp.int32, (t, HEADS_W), 1)
    out = jnp.zeros((t, HEADS_W), x.dtype)
    for h in range(N_HEADS):
        own = (lane >= h * HEAD_DIM) & (lane < (h + 1) * HEAD_DIM)
        out = jnp.where(own, x[h * t:(h + 1) * t], out)
    return out


def _per_head_rows(values, t):
    row = lax.broadcasted_iota(jnp.int32, (N_HEADS * t, 1), 0)
    out = jnp.zeros((N_HEADS * t, 1), F32)
    for h, v in enumerate(values):
        out = jnp.where((row >= h * t) & (row < (h + 1) * t), v, out)
    return out


def _fox_sample_kernel(pt_ref, q_ref, knt_ref, vnt_ref, lfn_ref, *rest, pages, n_steps):
    kp, vp, lp = rest[:pages], rest[pages:2 * pages], rest[2 * pages:3 * pages]
    o_ref, qbd_ref, m_ref, l_ref, acc_ref, carry_ref = rest[3 * pages:]
    s = pl.program_id(1)
    t_new = q_ref.shape[2]
    n_rows = N_HEADS * t_new

    @pl.when(s == 0)
    def _init():
        qbd_ref[...] = _block_diag_queries(q_ref[0] * ATTN_SCALE).astype(BF16)
        m_ref[...] = jnp.full(m_ref.shape, -jnp.inf, F32)
        l_ref[...] = jnp.zeros(l_ref.shape, F32)
        acc_ref[...] = jnp.zeros(acc_ref.shape, F32)
        carry_ref[...] = jnp.zeros(carry_ref.shape, F32)

    m, l, acc = m_ref[:, 0:1], l_ref[:, 0:1], acc_ref[...]

    n_rows_lf = pages * V7X_SUBLANES
    lf = jnp.concatenate([lp[r][...] for r in range(pages)], axis=0)
    ri = lax.broadcasted_iota(jnp.int32, (PAGE, PAGE), 0)
    ci = lax.broadcasted_iota(jnp.int32, (PAGE, PAGE), 1)
    upper = jnp.where(ri <= ci, 1.0, 0.0).astype(BF16)
    pi = lax.broadcasted_iota(jnp.int32, (n_rows_lf, n_rows_lf), 0)
    pj = lax.broadcasted_iota(jnp.int32, (n_rows_lf, n_rows_lf), 1)
    same_head = (pi & (V7X_SUBLANES - 1)) == (pj & (V7X_SUBLANES - 1))
    earlier_page = jnp.where(same_head, jnp.where(pj < (pi & -V7X_SUBLANES), 1.0, 0.0), 0.0).astype(BF16)
    any_page = jnp.where(same_head, 1.0, 0.0).astype(BF16)
    lf_pieces = [x.astype(BF16) for x in _split3(lf)]
    totals = sum(_dot(x, jnp.ones((PAGE, PAGE), BF16)) for x in lf_pieces)
    total_pieces = [x.astype(BF16) for x in _split3(totals)]
    carry = carry_ref[...]
    cum = (sum(_dot(x, upper) for x in lf_pieces) + sum(_dot(earlier_page, x) for x in total_pieces)) + carry
    carry = carry + sum(_dot(any_page, x) for x in total_pieces)
    key_gate = jnp.concatenate(
        [jnp.concatenate([jnp.broadcast_to(cum[r * V7X_SUBLANES + h:r * V7X_SUBLANES + h + 1, :], (t_new, PAGE))
                          for r in range(pages)], axis=1) for h in range(N_HEADS)], axis=0)

    def update(scores, vt, m, l, acc):
        m_new = jnp.maximum(m, jnp.max(scores, axis=1, keepdims=True))
        alpha = jnp.exp(m - m_new)
        p = jnp.exp(scores - m_new)
        return m_new, alpha * l + jnp.sum(p, axis=1, keepdims=True), alpha * acc + _dot_nt(p.astype(BF16), vt)

    kt = jnp.concatenate([_all_heads(kp[r]) for r in range(pages)], axis=1).astype(BF16)
    vt = jnp.concatenate([_all_heads(vp[r]) for r in range(pages)], axis=1).astype(BF16)
    m, l, acc = update(_dot(qbd_ref[...], kt) - key_gate, vt, m, l, acc)

    carry_ref[...] = carry
    m_ref[...] = jnp.broadcast_to(m, m_ref.shape)
    l_ref[...] = jnp.broadcast_to(l, l_ref.shape)
    acc_ref[...] = acc

    @pl.when(s == n_steps - 1)
    def _new_tokens():
        lane = lax.broadcasted_iota(jnp.int32, (n_rows, PAGE), 1)
        qi = lax.broadcasted_iota(jnp.int32, (n_rows, PAGE), 0) & (t_new - 1)
        cum_n = sum(_dot(x.astype(BF16), upper) for x in _split3(lfn_ref[0])) + carry[:V7X_SUBLANES]
        gate_n = jnp.concatenate([jnp.broadcast_to(cum_n[h:h + 1, :], (t_new, PAGE)) for h in range(N_HEADS)], axis=0)
        scores = jnp.where(lane <= qi, _dot(qbd_ref[...], _all_heads(knt_ref.at[0]).astype(BF16)) - gate_n, -jnp.inf)
        _, l_fin, acc_fin = update(scores, _all_heads(vnt_ref.at[0]).astype(BF16), m, l, acc)
        o_ref[0] = _own_head_columns(acc_fin / l_fin, t_new).astype(o_ref.dtype)


def _moba_sample_kernel(pt_ref, q_ref, knt_ref, vnt_ref, *rest, pages, n_steps, p_len):
    kp, vp = rest[:pages], rest[pages:2 * pages]
    o_ref, qbd_ref, q32_ref, means_ref, mb_ref, lb_ref, accb_ref = rest[2 * pages:]
    blk = MOBA_BLOCK
    nb = p_len // blk
    per_step = pages * PAGE // blk
    s = pl.program_id(1)
    t_new = q_ref.shape[2]
    n_rows = N_HEADS * t_new

    @pl.when(s == 0)
    def _init():
        q32_ref[...] = _block_diag_queries(q_ref[0])
        qbd_ref[...] = _block_diag_queries(q_ref[0] * ATTN_SCALE).astype(BF16)
        means_ref[...] = jnp.zeros(means_ref.shape, F32)
        mb_ref[...] = jnp.full(mb_ref.shape, -jnp.inf, F32)
        lb_ref[...] = jnp.zeros(lb_ref.shape, F32)

    means, mb, lb = means_ref[...], mb_ref[...], lb_ref[...]
    lane_q = lax.broadcasted_iota(jnp.int32, (n_rows, V7X_LANES), 1)
    lane_d = lax.broadcasted_iota(jnp.int32, (HEADS_W, V7X_LANES), 1)
    kcol = lax.broadcasted_iota(jnp.int32, (1, blk), 1).astype(F32)
    slope = _per_head_rows(ALIBI_SLOPES, t_new)

    k32 = jnp.concatenate([_all_heads(kp[r]) for r in range(pages)], axis=1)
    scores = _dot(qbd_ref[...], k32.astype(BF16))
    probs = []
    for bb in range(per_step):
        b = s * per_step + bb
        cols = slice(bb * blk, (bb + 1) * blk)
        means = jnp.where(lane_d == b, jnp.sum(k32[:, cols], axis=1, keepdims=True) * (1.0 / blk), means)
        sc = scores[:, cols] + slope * (kcol + (b * blk - p_len).astype(F32))
        mx = jnp.max(sc, axis=1, keepdims=True)
        p = jnp.exp(sc - mx)
        probs.append(p.astype(BF16))
        mb = jnp.where(lane_q == b, mx, mb)
        lb = jnp.where(lane_q == b, jnp.sum(p, axis=1, keepdims=True), lb)
    ppb = blk // PAGE
    for bb in range(per_step):
        vblk = jnp.concatenate([_all_heads(vp[bb * ppb + r]) for r in range(ppb)], axis=1).astype(BF16)
        accb_ref[s * per_step + bb] = _dot_nt(probs[bb], vblk)

    means_ref[...] = means
    mb_ref[...] = mb
    lb_ref[...] = lb

    @pl.when(s == n_steps - 1)
    def _select_and_combine():
        lane_f = lane_q.astype(F32)
        qi = lax.broadcasted_iota(jnp.int32, (n_rows, V7X_LANES), 0) & (t_new - 1)
        gate = jnp.where(lane_q < nb, _dot_exact(q32_ref[...], means), -jnp.inf)
        sel = jnp.zeros(gate.shape, F32)
        for _ in range(MOBA_TOPK):
            best = jnp.max(gate, axis=1, keepdims=True)
            first = jnp.min(jnp.where(gate == best, lane_f, float(V7X_LANES)), axis=1, keepdims=True)
            hit = lane_f == first
            sel = jnp.where(hit, jnp.where(best > -jnp.inf, 1.0, sel), sel)
            gate = jnp.where(hit, -jnp.inf, gate)
        sc = _dot(qbd_ref[...], _all_heads(knt_ref.at[0]).astype(BF16)) + slope * lane_f
        sc = jnp.where(lane_q <= qi, sc, -jnp.inf)
        m_own = jnp.max(sc, axis=1, keepdims=True)
        p_own = jnp.exp(sc - m_own)
        l_own = jnp.sum(p_own, axis=1, keepdims=True)
        acc_own = _dot_nt(p_own.astype(BF16), _all_heads(vnt_ref.at[0]).astype(BF16))
        picked = sel > 0.0
        m_all = jnp.maximum(m_own, jnp.max(jnp.where(picked, mb, -jnp.inf), axis=1, keepdims=True))
        w = jnp.where(picked, jnp.exp(mb - m_all), 0.0)
        w_own = jnp.exp(m_own - m_all)
        den = jnp.sum(w * lb, axis=1, keepdims=True) + w_own * l_own
        num = w_own * acc_own
        for b in range(nb):
            num = num + jnp.sum(jnp.where(lane_q == b, w, 0.0), axis=1, keepdims=True) * accb_ref[b]
        o_ref[0] = _own_head_columns(num / den, t_new).astype(o_ref.dtype)


def _sample_attention(kind, layer, page_table, q, knt, vnt, cache_kt, cache_vt, lfn=None, cache_lft=None):
    g, _, t_new, _ = q.shape
    n_pages = page_table.shape[1]
    pages = min(PAGES_PER_STEP, n_pages)
    n_steps = n_pages // pages
    n_rows = N_HEADS * t_new
    assert t_new & (t_new - 1) == 0 and pages * PAGE % MOBA_BLOCK == 0
    per_seq = lambda blk: pl.BlockSpec(blk, lambda b, s, pt: (b,) + (0,) * (len(blk) - 1))
    page_blk = (None, None, N_HEADS, HEAD_DIM, PAGE)
    in_specs = [per_seq((1, N_HEADS, t_new, HEAD_DIM)), per_seq((1, N_HEADS, HEAD_DIM, PAGE)),
                per_seq((1, N_HEADS, HEAD_DIM, PAGE))]
    args = [q, knt, vnt]
    if kind == "fox":
        in_specs.append(per_seq((1, V7X_SUBLANES, PAGE)))
        args.append(lfn)
    in_specs += _page_specs(pages, page_blk, layer, pages) + _page_specs(pages, page_blk, layer, pages)
    args += [cache_kt] * pages + [cache_vt] * pages
    stat = pltpu.VMEM((n_rows, V7X_LANES), F32)
    q_bd = pltpu.VMEM((n_rows, HEADS_W), BF16)
    if kind == "fox":
        in_specs += _page_specs(pages, (None, None, V7X_SUBLANES, PAGE), layer, pages)
        args += [cache_lft] * pages
        body = functools.partial(_fox_sample_kernel, pages=pages, n_steps=n_steps)
        scratch = [q_bd, stat, stat, pltpu.VMEM((n_rows, HEADS_W), F32),
                   pltpu.VMEM((pages * V7X_SUBLANES, V7X_LANES), F32)]
    else:
        p_len = n_pages * PAGE
        assert p_len // MOBA_BLOCK <= V7X_LANES
        body = functools.partial(_moba_sample_kernel, pages=pages, n_steps=n_steps, p_len=p_len)
        scratch = [q_bd, pltpu.VMEM((n_rows, HEADS_W), F32), pltpu.VMEM((HEADS_W, V7X_LANES), F32), stat, stat,
                   pltpu.VMEM((p_len // MOBA_BLOCK, n_rows, HEADS_W), F32)]
    return pl.pallas_call(
        body,
        out_shape=jax.ShapeDtypeStruct((g, t_new, HEADS_W), BF16),
        grid_spec=pltpu.PrefetchScalarGridSpec(
            num_scalar_prefetch=1,
            grid=(g, n_steps),
            in_specs=in_specs,
            out_specs=pl.BlockSpec((1, t_new, HEADS_W), lambda b, s, pt: (b, 0, 0)),
            scratch_shapes=scratch),
        compiler_params=_params("arbitrary", "arbitrary"),
        name=kind + "_sample",
    )(page_table, *args)


def _causal_shifts(u, fill, period):
    out = []
    for j in (1, 2):
        r = pltpu.roll(u, j, axis=0)
        f = pltpu.roll(fill, j, axis=0)
        if period is None:
            row = lax.broadcasted_iota(jnp.int32, f.shape, 0)
            out.append(jnp.concatenate([jnp.where(row >= j, r[:V7X_SUBLANES], f), r[V7X_SUBLANES:]], axis=0))
        else:
            t = lax.broadcasted_iota(jnp.int32, u.shape, 0) & (period - 1)
            out.append(jnp.where(t >= j, r, f))
    return out


def _conv3(u, fill, period, w_ref, cols):
    s1, s2 = _causal_shifts(u, fill, period)
    return w_ref[0:1, cols] * s2 + w_ref[1:2, cols] * s1 + w_ref[2:3, cols] * u


def _outproj_kernel(x_ref, pc_ref, yf_ref, ym_ref, gt_ref, cw_ref, wo_ref, g_ref, b_ref, *rest,
                    period, dn_alpha):
    if period is None:
        o_ref, us_ref, carry_ref = rest

        @pl.when(pl.program_id(1) == 0)
        def _fresh_sequence():
            carry_ref[...] = jnp.zeros(carry_ref.shape, F32)

        fill = carry_ref[...]
    else:
        fill_ref, o_ref, us_ref = rest
        fill = fill_ref[0]
    pc = pc_ref[0]
    u = pc[:, CONV_W:2 * CONV_W] * pc[:, 2 * CONV_W:]
    y_conv = pc[:, :CONV_W] * _conv3(u, fill, period, cw_ref, slice(None))
    mixed = (_dot(y_conv.astype(BF16), wo_ref[0:CONV_W, :])
             + _dot(yf_ref[0], wo_ref[CONV_W:CONV_W + HEADS_W, :])
             + _dot(ym_ref[0], wo_ref[CONV_W + HEADS_W:, :]))
    z = dn_alpha * x_ref[0] + (1.0 + gt_ref[0]) * mixed
    o_ref[0] = _layer_norm(z, g_ref[...], b_ref[...])
    if period is None:
        tail = u[u.shape[0] - V7X_SUBLANES:]
        carry_ref[...] = tail
        us_ref[0] = tail
    else:
        us_ref[0] = u


def _outproj(x3, pconv, yf, ym, gt, conv_w, wo, ln_g, ln_b, fill, tm, dn_alpha):
    gb, t, d = x3.shape
    period = None if fill is None else V7X_SUBLANES
    tok = lambda w: pl.BlockSpec((1, tm, w), lambda g, i: (g, i, 0))
    in_specs = [tok(d), tok(3 * CONV_W), tok(HEADS_W), tok(HEADS_W), _mod_spec(gt, tm),
                _const_spec(conv_w.shape), _const_spec(wo.shape), _const_spec(ln_g.shape), _const_spec(ln_b.shape)]
    args = [x3, pconv, yf, ym, gt, conv_w, wo, ln_g, ln_b]
    scratch = []
    if period is None:
        us_rows = V7X_SUBLANES
        us_spec = pl.BlockSpec((1, us_rows, CONV_W), lambda g, i: (g, 0, 0))
        scratch.append(pltpu.VMEM((V7X_SUBLANES, CONV_W), F32))
    else:
        us_rows = t
        us_spec = tok(CONV_W)
        in_specs.append(tok(CONV_W))
        args.append(fill)
    return pl.pallas_call(
        functools.partial(_outproj_kernel, period=period, dn_alpha=dn_alpha),
        out_shape=(jax.ShapeDtypeStruct((gb, t, d), F32), jax.ShapeDtypeStruct((gb, us_rows, CONV_W), F32)),
        grid=(gb, t // tm),
        in_specs=in_specs,
        out_specs=(tok(d), us_spec),
        scratch_shapes=scratch,
        compiler_params=_params("arbitrary", "arbitrary"),
        name="outproj",
    )(*args)


def _ffn_kernel(x_ref, sc_ref, sh_ref, gt_ref, wup_ref, cw_ref, cb_ref, wdn_ref, g_ref, b_ref, *rest,
                period, dn_alpha):
    if period is None:
        o_ref, us_ref, carry_ref = rest

        @pl.when(pl.program_id(1) == 0)
        def _fresh_sequence():
            carry_ref[...] = jnp.zeros(carry_ref.shape, F32)
    else:
        fill_ref, o_ref, us_ref = rest
    x = x_ref[0]
    tm = x.shape[0]
    d_ff = wdn_ref.shape[0]
    h = (x * (1.0 + sc_ref[0]) + sh_ref[0]).astype(BF16)
    halves = []
    for half in range(2):
        cols = slice(half * d_ff, (half + 1) * d_ff)
        u = _dot(h, wup_ref[:, cols])
        fill = carry_ref[:, cols] if period is None else fill_ref[0, :, cols]
        halves.append(_conv3(u, fill, period, cw_ref, cols) + cb_ref[:, cols])
        if period is None:
            tail = u[tm - V7X_SUBLANES:]
            carry_ref[:, cols] = tail
            us_ref[0, :, cols] = tail
        else:
            us_ref[0, :, cols] = u
    gate, val = halves
    y = _dot((gate * _sigmoid(gate) * val).astype(BF16), wdn_ref[...])
    z = dn_alpha * x + (1.0 + gt_ref[0]) * y
    o_ref[0] = _layer_norm(z, g_ref[...], b_ref[...])


def _ffn(x3, sc, sh, gt, wup, conv_w, conv_b, wdn, ln_g, ln_b, fill, tm, dn_alpha):
    gb, t, d = x3.shape
    n_up = wup.shape[1]
    period = None if fill is None else V7X_SUBLANES
    tok = lambda w: pl.BlockSpec((1, tm, w), lambda g, i: (g, i, 0))
    in_specs = [tok(d), _mod_spec(sc, tm), _mod_spec(sh, tm), _mod_spec(gt, tm),
                _const_spec(wup.shape), _const_spec(conv_w.shape), _const_spec(conv_b.shape),
                _const_spec(wdn.shape), _const_spec(ln_g.shape), _const_spec(ln_b.shape)]
    args = [x3, sc, sh, gt, wup, conv_w, conv_b, wdn, ln_g, ln_b]
    scratch = []
    if period is None:
        us_rows = V7X_SUBLANES
        us_spec = pl.BlockSpec((1, us_rows, n_up), lambda g, i: (g, 0, 0))
        scratch.append(pltpu.VMEM((V7X_SUBLANES, n_up), F32))
    else:
        us_rows = t
        us_spec = tok(n_up)
        in_specs.append(tok(n_up))
        args.append(fill)
    return pl.pallas_call(
        functools.partial(_ffn_kernel, period=period, dn_alpha=dn_alpha),
        out_shape=(jax.ShapeDtypeStruct((gb, t, d), F32), jax.ShapeDtypeStruct((gb, us_rows, n_up), F32)),
        grid=(gb, t // tm),
        in_specs=in_specs,
        out_specs=(tok(d), us_spec),
        scratch_shapes=scratch,
        compiler_params=_params("arbitrary", "arbitrary"),
        name="ffn",
    )(*args)


def _seq_fill(state):
    g, w, c = state.shape
    padded = jnp.pad(state, ((0, 0), (V7X_SUBLANES - w, 0), (0, 0)))
    return jnp.roll(padded, -1, axis=0).reshape(1, g * V7X_SUBLANES, c)


def _per_seq_transposed(xt, g, t_new):
    return xt.reshape(N_HEADS, HEAD_DIM, g, t_new).transpose(2, 0, 1, 3)


def _new_token_blocks(xt, g, t_new):
    return jnp.pad(_per_seq_transposed(xt, g, t_new), ((0, 0), (0, 0), (0, 0), (0, PAGE - t_new)))


def _layer(l, depth, x3, mods, weights, kv_states, dn_alpha, tm, sample=None):
    sh_a, sc_a, gt_a, sh_f, sc_f, gt_f = mods
    (wtok, wqkvt, bf, conv_w, wo, ln1_g, ln1_b, wup, ffn_w, ffn_b, wdn, ln2_g, ln2_b) = weights
    pconv, lf, fqt, mqt, *kv_states = _inproj(l, depth, x3, sc_a, sh_a, wtok, wqkvt, bf, kv_states, tm)
    fkt, fvt, mkt, mvt = kv_states
    if sample is None:
        y_fox = _prompt_attention("fox", l, fqt, fkt, fvt, lf)
        y_moba = _prompt_attention("moba", l, mqt, mkt, mvt)
        conv_fill = ffn_fill = None
    else:
        g, t_new = sample["g"], sample["t"]
        tok_q = lambda qt: _per_seq_transposed(qt[0], g, t_new).transpose(0, 1, 3, 2)
        new = lambda st: _new_token_blocks(st[l, 0], g, t_new)
        lfn = lf[0].reshape(g, t_new, V7X_LANES)[:, :, :V7X_SUBLANES].transpose(0, 2, 1)
        lfn = jnp.pad(lfn, ((0, 0), (0, 0), (0, PAGE - t_new)))
        y_fox = _sample_attention("fox", l, sample["page_table"], tok_q(fqt), new(fkt), new(fvt),
                                  sample["fox_kt"], sample["fox_vt"], lfn, sample["fox_lft"])
        y_moba = _sample_attention("moba", l, sample["page_table"], tok_q(mqt), new(mkt), new(mvt),
                                   sample["moba_kt"], sample["moba_vt"])
        y_fox = y_fox.reshape(1, g * t_new, HEADS_W)
        y_moba = y_moba.reshape(1, g * t_new, HEADS_W)
        conv_fill = _seq_fill(sample["state_conv"][l])
        ffn_fill = _seq_fill(sample["state_ffn"][l])
    x1, conv_tail = _outproj(x3, pconv, y_fox, y_moba, gt_a, conv_w, wo, ln1_g, ln1_b, conv_fill, tm, dn_alpha)
    x2, ffn_tail = _ffn(x1, sc_f, sh_f, gt_f, wup, ffn_w, ffn_b, wdn, ln2_g, ln2_b, ffn_fill, tm, dn_alpha)
    return x2, kv_states, (lf, conv_tail, ffn_tail)


def kernel(x_prompt, x_sample, cache_fox_k, cache_fox_v, cache_fox_logf, cache_moba_k, cache_moba_v, state_conv, state_ffn, page_table, c_prompt, c_sample, w_ada, b_ada, w_in, b_f, conv_w, w_o, ln1_g, ln1_b, w_up, ffn_conv_w, ffn_conv_b, w_down, ln2_g, ln2_b):
    depth, d, _ = w_ada.shape
    gp, tp, _ = x_prompt.shape
    gs, ts, _ = x_sample.shape
    dn_alpha = (2 * depth) ** 0.25
    n_conv, n_qkv = 3 * CONV_W, 3 * HEADS_W

    ada = _ada(jnp.concatenate([c_prompt, c_sample], axis=0), w_ada, b_ada)
    to_kt = lambda c: jnp.transpose(c, (0, 1, 3, 4, 2))
    lft_cache = jnp.pad(jnp.transpose(cache_fox_logf, (0, 1, 3, 2)),
                        ((0, 0), (0, 0), (0, V7X_SUBLANES - N_HEADS), (0, 0)))
    sample = dict(g=gs, t=ts, page_table=page_table, state_conv=state_conv, state_ffn=state_ffn,
                  fox_kt=to_kt(cache_fox_k), fox_vt=to_kt(cache_fox_v), fox_lft=lft_cache,
                  moba_kt=to_kt(cache_moba_k), moba_vt=to_kt(cache_moba_v))

    xp = x_prompt
    xs = x_sample.reshape(1, gs * ts, d)
    kv_p = kv_s = None
    small_p, small_s = [], []
    for l in range(depth):
        w = w_in[l]
        fox0, ff0, moba0 = n_conv, n_conv + n_qkv, n_conv + n_qkv + N_HEADS
        wtok = jnp.concatenate([w[:, :n_conv], jnp.pad(w[:, ff0:moba0], ((0, 0), (0, V7X_LANES - N_HEADS)))],
                               axis=1).astype(BF16)
        wqkvt = jnp.concatenate([w[:, fox0:ff0], w[:, moba0:]], axis=1).T.astype(BF16)
        bf = jnp.pad(b_f[l], (0, V7X_LANES - N_HEADS))[None]
        weights = (wtok, wqkvt, bf, conv_w[l], w_o[l].astype(BF16), ln1_g[l][None], ln1_b[l][None],
                   w_up[l].astype(BF16), ffn_conv_w[l], ffn_conv_b[l][None], w_down[l].astype(BF16),
                   ln2_g[l][None], ln2_b[l][None])
        mods_p = tuple(a[:, None, :] for a in jnp.split(ada[l, :gp], 6, axis=-1))
        mods_s = tuple(jnp.repeat(a, ts, axis=0)[None] for a in jnp.split(ada[l, gp:], 6, axis=-1))
        xp, kv_p, st_p = _layer(l, depth, xp, mods_p, weights, kv_p, dn_alpha, tm=512)
        xs, kv_s, st_s = _layer(l, depth, xs, mods_s, weights, kv_s, dn_alpha, tm=gs * ts, sample=sample)
        small_p.append(st_p)
        small_s.append(st_s)

    stack = lambda states, i: jnp.stack([s[i] for s in states])
    out = [xp, xs.reshape(gs, ts, d)]
    fk, fv, mk, mv = (jnp.transpose(a, (0, 1, 4, 2, 3)) for a in kv_p)
    out += [fk, fv, stack(small_p, 0)[..., :N_HEADS], mk, mv,
            stack(small_p, 1)[:, :, V7X_SUBLANES - 2:], stack(small_p, 2)[:, :, V7X_SUBLANES - 2:]]
    fk, fv, mk, mv = (jnp.transpose(a[:, 0].reshape(depth, N_HEADS, HEAD_DIM, gs, ts), (0, 3, 4, 1, 2))
                      for a in kv_s)
    tail_s = lambda i: stack(small_s, i).reshape(depth, gs, ts, -1)[:, :, ts - 2:]
    out += [fk, fv, stack(small_s, 0).reshape(depth, gs, ts, V7X_LANES)[..., :N_HEADS], mk, mv,
            tail_s(1), tail_s(2)]
    return tuple(out)
```

```python
import functools

import jax
import jax.numpy as jnp
from jax import lax
from jax.experimental import pallas as pl
from jax.experimental.pallas import tpu as pltpu

F32 = jnp.float32
BF16 = jnp.bfloat16
HIGHEST = lax.Precision.HIGHEST

HEAD_DIM = 64
N_HEADS = 6
HEADS_W = N_HEADS * HEAD_DIM
CONV_W = 256
MOBA_BLOCK = 256
MOBA_TOPK = 3
PAGE = 128
LN_EPS = 1e-5
ATTN_SCALE = HEAD_DIM ** -0.5
LOG2E = 1.4426950408889634
ALIBI_SLOPES = tuple(2.0 ** (-8.0 * (h + 1) / N_HEADS) for h in range(N_HEADS))
MASKED = -1e30

V7X_LANES = 128
V7X_SUBLANES = 8
V7X_VMEM_BYTES = 64 * 1024 * 1024
VMEM_LIMIT = 52 * 1024 * 1024
PAGES_PER_STEP = 16
AUG_KEY_TERM = HEAD_DIM
AUG_QUERY_TERM = HEAD_DIM + 3
AUG_END = HEAD_DIM + 6


def _dot(a, b):
    return jnp.dot(a, b, preferred_element_type=F32)


def _dot_nt(a, b):
    return lax.dot_general(a, b, (((1,), (1,)), ((), ())), preferred_element_type=F32)


def _dot_exact(a, b):
    return jnp.dot(a, b, precision=HIGHEST, preferred_element_type=F32)


def _sigmoid(x):
    return 1.0 / (1.0 + jnp.exp(-x))


def _log_sigmoid(x):
    return jnp.minimum(x, 0.0) - jnp.log(1.0 + jnp.exp(-jnp.abs(x)))


def _layer_norm(z, g, b):
    mu = jnp.mean(z, axis=-1, keepdims=True)
    zc = z - mu
    var = jnp.mean(zc * zc, axis=-1, keepdims=True)
    return zc * lax.rsqrt(var + LN_EPS) * g + b


def _split3(x):
    hi = x.astype(BF16).astype(F32)
    rest = x - hi
    mid = rest.astype(BF16).astype(F32)
    return hi, mid, rest - mid


def _params(*semantics):
    return pltpu.CompilerParams(dimension_semantics=semantics, vmem_limit_bytes=VMEM_LIMIT)


def _const_spec(shape):
    zeros = (0,) * len(shape)
    return pl.BlockSpec(shape, lambda *_: zeros)


def _mod_spec(arr, tm):
    d = arr.shape[-1]
    if arr.shape[1] == 1:
        return pl.BlockSpec((1, 1, d), lambda g, i: (g, 0, 0))
    return pl.BlockSpec((1, tm, d), lambda g, i: (g, i, 0))


def _ada_kernel(c_ref, w_ref, b_ref, o_ref):
    c = c_ref[...]
    o_ref[0] = _dot_exact(c * _sigmoid(c), w_ref[0]) + b_ref[0]


def _ada(c_all, w_ada, b_ada):
    depth, d, n = w_ada.shape
    g = c_all.shape[0]
    tn = 1536
    return pl.pallas_call(
        _ada_kernel,
        out_shape=jax.ShapeDtypeStruct((depth, g, n), F32),
        grid=(depth, n // tn),
        in_specs=[pl.BlockSpec((g, d), lambda l, j: (0, 0)),
                  pl.BlockSpec((1, d, tn), lambda l, j: (l, 0, j)),
                  pl.BlockSpec((1, 1, tn), lambda l, j: (l, 0, j))],
        out_specs=pl.BlockSpec((1, g, tn), lambda l, j: (l, 0, j)),
        compiler_params=_params("arbitrary", "arbitrary"),
        name="ada",
    )(c_all, w_ada, b_ada.reshape(depth, 1, n))


N_KV_STATES = 4


def _inproj_kernel(x_ref, sc_ref, sh_ref, wtok_ref, wqkvt_ref, bf_ref, *rest):
    conv_ref, lf_ref, fqt_ref, mqt_ref, fkt_ref, fvt_ref, mkt_ref, mvt_ref = rest[len(rest) - 8:]
    h = (x_ref[0] * (1.0 + sc_ref[0]) + sh_ref[0]).astype(BF16)
    conv_ref[0] = _dot(h, wtok_ref[:, 0:3 * CONV_W])
    lf_ref[0] = _log_sigmoid(_dot(h, wtok_ref[:, 3 * CONV_W:]) + bf_ref[...])
    for c, ref in enumerate((fqt_ref, fkt_ref, fvt_ref, mqt_ref, mkt_ref, mvt_ref)):
        t = _dot_nt(wqkvt_ref[c * HEADS_W:(c + 1) * HEADS_W, :], h)
        ref[0] = t.reshape(N_HEADS, HEAD_DIM, t.shape[-1])


def _inproj(layer, depth, x3, sc, sh, wtok, wqkvt, bf, kv_states, tm):
    gb, t, d = x3.shape
    tok = lambda w: pl.BlockSpec((1, tm, w), lambda g, i: (g, i, 0))
    qt = pl.BlockSpec((1, N_HEADS, HEAD_DIM, tm), lambda g, i: (g, 0, 0, i))
    kvt = pl.BlockSpec((None, 1, N_HEADS, HEAD_DIM, tm), lambda g, i: (layer, g, 0, 0, i))
    qt_shape = jax.ShapeDtypeStruct((gb, N_HEADS, HEAD_DIM, t), F32)
    kvt_shape = jax.ShapeDtypeStruct((depth, gb, N_HEADS, HEAD_DIM, t), F32)
    in_specs = [tok(d), _mod_spec(sc, tm), _mod_spec(sh, tm),
                _const_spec(wtok.shape), _const_spec(wqkvt.shape), _const_spec(bf.shape)]
    args = [x3, sc, sh, wtok, wqkvt, bf]
    aliases = {}
    if kv_states is not None:
        aliases = {len(args) + k: 4 + k for k in range(N_KV_STATES)}
        in_specs += [pl.BlockSpec(memory_space=pl.ANY)] * N_KV_STATES
        args += list(kv_states)
    return pl.pallas_call(
        _inproj_kernel,
        out_shape=(jax.ShapeDtypeStruct((gb, t, 3 * CONV_W), F32),
                   jax.ShapeDtypeStruct((gb, t, V7X_LANES), F32),
                   qt_shape, qt_shape, kvt_shape, kvt_shape, kvt_shape, kvt_shape),
        grid=(gb, t // tm),
        in_specs=in_specs,
        out_specs=(tok(3 * CONV_W), tok(V7X_LANES), qt, qt, kvt, kvt, kvt, kvt),
        input_output_aliases=aliases,
        compiler_params=_params("arbitrary", "arbitrary"),
        name="inproj",
    )(*args)


def _augment_keys(k_tok, key_term, lane):
    hi, mid, lo = _split3(key_term)
    x = jnp.where(lane < HEAD_DIM, jnp.concatenate([k_tok, jnp.zeros_like(k_tok)], axis=1),
                  jnp.where(lane < AUG_END, 1.0, 0.0))
    x = jnp.where(lane == AUG_KEY_TERM, hi, x)
    x = jnp.where(lane == AUG_KEY_TERM + 1, mid, x)
    return jnp.where(lane == AUG_KEY_TERM + 2, lo, x).astype(BF16)


def _augment_queries(qt, query_term):
    row = lax.broadcasted_iota(jnp.int32, qt.shape, 0)
    tail = jnp.where(row < AUG_QUERY_TERM - HEAD_DIM, 1.0, 0.0)
    if query_term is not None:
        for k, piece in enumerate(_split3(query_term)):
            tail = jnp.where(row == AUG_QUERY_TERM - HEAD_DIM + k, piece, tail)
    return jnp.concatenate([qt, tail], axis=0).astype(BF16)


def _flash_heads(i, tq, ka_ref, vb_ref, qa_ref, s_ref, p_ref, acc_ref, past_bias):
    key = lax.broadcasted_iota(jnp.int32, (tq, tq), 0)
    qry = lax.broadcasted_iota(jnp.int32, (tq, tq), 1)
    causal = key <= qry

    def step(tiles, carry):
        for t, (j, diag) in enumerate(tiles):
            for h in range(N_HEADS):
                s = _dot(ka_ref[j, h], qa_ref[h])
                if diag:
                    s = jnp.where(causal, s, -jnp.inf)
                elif past_bias is not None:
                    s = s + past_bias(h, j)
                s_ref[t, h] = s
        out, alphas = [], []
        for h in range(N_HEADS):
            m, l = carry[h]
            scores = [s_ref[t, h] for t in range(len(tiles))]
            m_new = m
            for s in scores:
                m_new = jnp.maximum(m_new, jnp.max(s, axis=0, keepdims=True))
            alpha = jnp.exp2(m - m_new)
            l = alpha * l
            for t, s in enumerate(scores):
                p = jnp.exp2(s - m_new)
                p_ref[t, h] = p.astype(BF16)
                l = l + jnp.sum(p, axis=0, keepdims=True)
            out.append((m_new, l))
            alphas.append(alpha)
        for h in range(N_HEADS):
            acc = alphas[h] * acc_ref[h]
            for t, (j, _) in enumerate(tiles):
                acc = acc + _dot(vb_ref[j, h], p_ref[t, h])
            acc_ref[h] = acc
        return tuple(out)

    acc_ref[...] = jnp.zeros(acc_ref.shape, F32)
    init = tuple((jnp.full((1, tq), -jnp.inf, F32), jnp.zeros((1, tq), F32)) for _ in range(N_HEADS))
    carry = lax.fori_loop(0, i // 2, lambda jj, c: step([(2 * jj, False), (2 * jj + 1, False)], c), init)
    carry = lax.cond(i % 2 == 1, lambda c: step([(i - 1, False), (i, True)], c),
                     lambda c: step([(i, True)], c), carry)
    return [(acc_ref[h] / carry[h][1]).T for h in range(N_HEADS)]


def _fox_prompt_kernel(qt_ref, kt_ref, vt_ref, lf_ref, o_ref, ka_ref, vb_ref, *work, tq, nkt):
    i = pl.program_id(1)

    @pl.when(i == 0)
    def _prepare_sequence():
        rows = lax.broadcasted_iota(jnp.int32, (tq, tq), 0)
        cols = lax.broadcasted_iota(jnp.int32, (tq, tq), 1)
        lower = (cols <= rows).astype(F32)
        lane = lax.broadcasted_iota(jnp.int32, (tq, V7X_LANES), 1)
        carry = jnp.zeros((1, V7X_LANES), F32)
        for c in range(nkt):
            sl = slice(c * tq, (c + 1) * tq)
            cum = _dot_exact(lower, lf_ref[0, sl, :]) + carry
            carry = cum[tq - 1:tq, :]
            vb_ref[c] = vt_ref[0, :, :, sl].astype(BF16)
            for h in range(N_HEADS):
                key_term = jnp.broadcast_to(-LOG2E * cum[:, h:h + 1], (tq, V7X_LANES))
                ka_ref[c, h] = _augment_keys(kt_ref[0, h, :, sl].T, key_term, lane)

    qt = qt_ref[0] * (ATTN_SCALE * LOG2E)
    qa_ref = work[0]
    for h in range(N_HEADS):
        qa_ref[h] = _augment_queries(qt[h], None)
    heads = _flash_heads(i, tq, ka_ref, vb_ref, *work, None)
    o_ref[0] = jnp.concatenate(heads, axis=1).astype(o_ref.dtype)


def _moba_prompt_kernel(qt_ref, kt_ref, vt_ref, o_ref, ka_ref, vb_ref, means_ref, *work, nb):
    blk = MOBA_BLOCK
    i = pl.program_id(1)

    @pl.when(i == 0)
    def _prepare_sequence():
        lane = lax.broadcasted_iota(jnp.int32, (blk, V7X_LANES), 1)
        pos = lax.broadcasted_iota(jnp.int32, (blk, V7X_LANES), 0)
        brow = lax.broadcasted_iota(jnp.int32, (V7X_SUBLANES, HEAD_DIM), 0)
        means = [jnp.zeros((V7X_SUBLANES, HEAD_DIM), F32) for _ in range(N_HEADS)]
        for c in range(nb):
            sl = slice(c * blk, (c + 1) * blk)
            vb_ref[c] = vt_ref[0, :, :, sl].astype(BF16)
            kpos = (pos + c * blk).astype(F32)
            for h in range(N_HEADS):
                k_tok = kt_ref[0, h, :, sl].T
                means[h] = jnp.where(brow == c, jnp.sum(k_tok, axis=0, keepdims=True) * (1.0 / blk), means[h])
                ka_ref[c, h] = _augment_keys(k_tok, (LOG2E * ALIBI_SLOPES[h]) * kpos, lane)
        for h in range(N_HEADS):
            means_ref[h] = means[h]

    q_raw = qt_ref[0]
    brow = lax.broadcasted_iota(jnp.int32, (V7X_SUBLANES, blk), 0)
    brow_f = brow.astype(F32)
    qa_ref = work[0]
    biases = []
    for h in range(N_HEADS):
        gate = jnp.where(brow < i, _dot_exact(means_ref[h], q_raw[h]), -jnp.inf)
        bias = jnp.full(gate.shape, MASKED, F32)
        for _ in range(MOBA_TOPK):
            best = jnp.max(gate, axis=0, keepdims=True)
            first = jnp.min(jnp.where(gate == best, brow_f, float(V7X_SUBLANES)), axis=0, keepdims=True)
            hit = brow_f == first
            bias = jnp.where(hit, jnp.where(best > -jnp.inf, 0.0, bias), bias)
            gate = jnp.where(hit, -jnp.inf, gate)
        biases.append(bias)
        tile_term = jnp.full((1, blk), -LOG2E * ALIBI_SLOPES[h], F32) * (i * blk).astype(F32)
        qa_ref[h] = _augment_queries(q_raw[h] * (ATTN_SCALE * LOG2E), tile_term)

    def past_bias(h, j):
        return jnp.sum(jnp.where(brow == j, biases[h], 0.0), axis=0, keepdims=True)

    heads = _flash_heads(i, blk, ka_ref, vb_ref, *work, past_bias)
    o_ref[0] = jnp.concatenate(heads, axis=1).astype(o_ref.dtype)


def _prompt_attention(kind, layer, qt, kt, vt, lf=None):
    g, _, _, t = qt.shape
    tq = MOBA_BLOCK
    nkt = t // tq
    assert nkt <= V7X_SUBLANES, "block gates are kept on one sublane group"
    qspec = pl.BlockSpec((1, N_HEADS, HEAD_DIM, tq), lambda b, i: (b, 0, 0, i))
    kvspec = pl.BlockSpec((None, 1, N_HEADS, HEAD_DIM, t), lambda b, i: (layer, b, 0, 0, 0))
    scratch = [pltpu.VMEM((nkt, N_HEADS, tq, V7X_LANES), BF16), pltpu.VMEM((nkt, N_HEADS, HEAD_DIM, tq), BF16)]
    work = [pltpu.VMEM((N_HEADS, V7X_LANES, tq), BF16), pltpu.VMEM((2, N_HEADS, tq, tq), F32),
            pltpu.VMEM((2, N_HEADS, tq, tq), BF16), pltpu.VMEM((N_HEADS, HEAD_DIM, tq), F32)]
    if kind == "fox":
        body = functools.partial(_fox_prompt_kernel, tq=tq, nkt=nkt)
        in_specs = [qspec, kvspec, kvspec, pl.BlockSpec((1, t, V7X_LANES), lambda b, i: (b, 0, 0))]
        args = (qt, kt, vt, lf)
    else:
        body = functools.partial(_moba_prompt_kernel, nb=nkt)
        in_specs = [qspec, kvspec, kvspec]
        args = (qt, kt, vt)
        scratch.append(pltpu.VMEM((N_HEADS, V7X_SUBLANES, HEAD_DIM), F32))
    return pl.pallas_call(
        body,
        out_shape=jax.ShapeDtypeStruct((g, t, HEADS_W), BF16),
        grid=(g, nkt),
        in_specs=in_specs,
        out_specs=pl.BlockSpec((1, tq, HEADS_W), lambda b, i: (b, i, 0)),
        scratch_shapes=scratch + work,
        compiler_params=_params("arbitrary", "arbitrary"),
        name=kind + "_prompt",
    )(*args)


def _page_copies(pt_ref, caches, bufs, sems, layer, g, s, slot, pages):
    return [pltpu.make_async_copy(cache.at[layer, pt_ref[g, s * pages + r]], buf.at[slot, r], sems.at[slot, a])
            for r in range(pages) for a, (cache, buf) in enumerate(zip(caches, bufs))]


def _stream_pages(pt_ref, caches, bufs, sems, layer, pages, n_steps):
    g, s = pl.program_id(0), pl.program_id(1)
    step = g * n_steps + s
    slot = step & 1
    last = pl.num_programs(0) * n_steps - 1

    @pl.when(step == 0)
    def _first():
        for c in _page_copies(pt_ref, caches, bufs, sems, layer, g, s, slot, pages):
            c.start()

    @pl.when(step < last)
    def _prefetch():
        wrap = s == n_steps - 1
        g_next = jnp.where(wrap, g + 1, g)
        s_next = jnp.where(wrap, 0, s + 1)
        for c in _page_copies(pt_ref, caches, bufs, sems, layer, g_next, s_next, 1 - slot, pages):
            c.start()

    for c in _page_copies(pt_ref, caches, bufs, sems, layer, g, s, slot, pages):
        c.wait()
    return slot


def _all_heads(page_ref):
    return page_ref[...].reshape(HEADS_W, page_ref.shape[-1])


def _block_diag_queries(q):
    z = jnp.zeros(q.shape[1:], q.dtype)
    return jnp.concatenate([jnp.concatenate([q[h] if c == h else z for c in range(N_HEADS)], axis=1)
                            for h in range(N_HEADS)], axis=0)


def _own_head_columns(x, t):
    lane = lax.broadcasted_iota(jnp.int32, (t, HEADS_W), 1)
    out = jnp.zeros((t, HEADS_W), x.dtype)
    for h in range(N_HEADS):
        own = (lane >= h * HEAD_DIM) & (lane < (h + 1) * HEAD_DIM)
        out = jnp.where(own, x[h * t:(h + 1) * t], out)
    return out


def _per_head_rows(values, t):
    row = lax.broadcasted_iota(jnp.int32, (N_HEADS * t, 1), 0)
    out = jnp.zeros((N_HEADS * t, 1), F32)
    for h, v in enumerate(values):
        out = jnp.where((row >= h * t) & (row < (h + 1) * t), v, out)
    return out


def _fox_sample_kernel(pt_ref, q_ref, knt_ref, vnt_ref, lfn_ref, ck_ref, cv_ref, cl_ref, o_ref,
                       qbd_ref, m_ref, l_ref, acc_ref, carry_ref, kbuf, vbuf, lbuf, sems, *, layer, pages, n_steps):
    slot = _stream_pages(pt_ref, (ck_ref, cv_ref, cl_ref), (kbuf, vbuf, lbuf), sems, layer, pages, n_steps)
    kp, vp, lp = ([buf.at[slot, r] for r in range(pages)] for buf in (kbuf, vbuf, lbuf))
    s = pl.program_id(1)
    t_new = q_ref.shape[2]
    n_rows = N_HEADS * t_new

    @pl.when(s == 0)
    def _init():
        qbd_ref[...] = _block_diag_queries(q_ref[0] * ATTN_SCALE).astype(BF16)
        m_ref[...] = jnp.full(m_ref.shape, -jnp.inf, F32)
        l_ref[...] = jnp.zeros(l_ref.shape, F32)
        acc_ref[...] = jnp.zeros(acc_ref.shape, F32)
        carry_ref[...] = jnp.zeros(carry_ref.shape, F32)

    m, l, acc = m_ref[:, 0:1], l_ref[:, 0:1], acc_ref[...]

    n_rows_lf = pages * V7X_SUBLANES
    lf = jnp.concatenate([lp[r][...] for r in range(pages)], axis=0)
    ri = lax.broadcasted_iota(jnp.int32, (PAGE, PAGE), 0)
    ci = lax.broadcasted_iota(jnp.int32, (PAGE, PAGE), 1)
    upper = jnp.where(ri <= ci, 1.0, 0.0).astype(BF16)
    pi = lax.broadcasted_iota(jnp.int32, (n_rows_lf, n_rows_lf), 0)
    pj = lax.broadcasted_iota(jnp.int32, (n_rows_lf, n_rows_lf), 1)
    same_head = (pi & (V7X_SUBLANES - 1)) == (pj & (V7X_SUBLANES - 1))
    earlier_page = jnp.where(same_head, jnp.where(pj < (pi & -V7X_SUBLANES), 1.0, 0.0), 0.0).astype(BF16)
    any_page = jnp.where(same_head, 1.0, 0.0).astype(BF16)
    lf_pieces = [x.astype(BF16) for x in _split3(lf)]
    totals = sum(_dot(x, jnp.ones((PAGE, PAGE), BF16)) for x in lf_pieces)
    total_pieces = [x.astype(BF16) for x in _split3(totals)]
    carry = carry_ref[...]
    cum = (sum(_dot(x, upper) for x in lf_pieces) + sum(_dot(earlier_page, x) for x in total_pieces)) + carry
    carry = carry + sum(_dot(any_page, x) for x in total_pieces)
    key_gate = jnp.concatenate(
        [jnp.concatenate([jnp.broadcast_to(cum[r * V7X_SUBLANES + h:r * V7X_SUBLANES + h + 1, :], (t_new, PAGE))
                          for r in range(pages)], axis=1) for h in range(N_HEADS)], axis=0)

    def update(scores, vt, m, l, acc):
        m_new = jnp.maximum(m, jnp.max(scores, axis=1, keepdims=True))
        alpha = jnp.exp(m - m_new)
        p = jnp.exp(scores - m_new)
        return m_new, alpha * l + jnp.sum(p, axis=1, keepdims=True), alpha * acc + _dot_nt(p.astype(BF16), vt)

    kt = jnp.concatenate([_all_heads(kp[r]) for r in range(pages)], axis=1).astype(BF16)
    vt = jnp.concatenate([_all_heads(vp[r]) for r in range(pages)], axis=1).astype(BF16)
    m, l, acc = update(_dot(qbd_ref[...], kt) - key_gate, vt, m, l, acc)

    carry_ref[...] = carry
    m_ref[...] = jnp.broadcast_to(m, m_ref.shape)
    l_ref[...] = jnp.broadcast_to(l, l_ref.shape)
    acc_ref[...] = acc

    @pl.when(s == n_steps - 1)
    def _new_tokens():
        lane = lax.broadcasted_iota(jnp.int32, (n_rows, PAGE), 1)
        qi = lax.broadcasted_iota(jnp.int32, (n_rows, PAGE), 0) & (t_new - 1)
        cum_n = sum(_dot(x.astype(BF16), upper) for x in _split3(lfn_ref[0])) + carry[:V7X_SUBLANES]
        gate_n = jnp.concatenate([jnp.broadcast_to(cum_n[h:h + 1, :], (t_new, PAGE)) for h in range(N_HEADS)], axis=0)
        scores = jnp.where(lane <= qi, _dot(qbd_ref[...], _all_heads(knt_ref.at[0]).astype(BF16)) - gate_n, -jnp.inf)
        _, l_fin, acc_fin = update(scores, _all_heads(vnt_ref.at[0]).astype(BF16), m, l, acc)
        o_ref[0] = _own_head_columns(acc_fin / l_fin, t_new).astype(o_ref.dtype)


def _moba_sample_kernel(pt_ref, q_ref, knt_ref, vnt_ref, ck_ref, cv_ref, o_ref, qbd_ref, q32_ref, means_ref,
                        mb_ref, lb_ref, accb_ref, kbuf, vbuf, sems, *, layer, pages, n_steps, p_len):
    slot = _stream_pages(pt_ref, (ck_ref, cv_ref), (kbuf, vbuf), sems, layer, pages, n_steps)
    kp, vp = ([buf.at[slot, r] for r in range(pages)] for buf in (kbuf, vbuf))
    blk = MOBA_BLOCK
    nb = p_len // blk
    per_step = pages * PAGE // blk
    s = pl.program_id(1)
    t_new = q_ref.shape[2]
    n_rows = N_HEADS * t_new

    @pl.when(s == 0)
    def _init():
        q32_ref[...] = _block_diag_queries(q_ref[0])
        qbd_ref[...] = _block_diag_queries(q_ref[0] * ATTN_SCALE).astype(BF16)
        means_ref[...] = jnp.zeros(means_ref.shape, F32)
        mb_ref[...] = jnp.full(mb_ref.shape, -jnp.inf, F32)
        lb_ref[...] = jnp.zeros(lb_ref.shape, F32)

    means, mb, lb = means_ref[...], mb_ref[...], lb_ref[...]
    lane_q = lax.broadcasted_iota(jnp.int32, (n_rows, V7X_LANES), 1)
    lane_d = lax.broadcasted_iota(jnp.int32, (HEADS_W, V7X_LANES), 1)
    kcol = lax.broadcasted_iota(jnp.int32, (1, blk), 1).astype(F32)
    slope = _per_head_rows(ALIBI_SLOPES, t_new)

    k32 = jnp.concatenate([_all_heads(kp[r]) for r in range(pages)], axis=1)
    scores = _dot(qbd_ref[...], k32.astype(BF16))
    probs = []
    for bb in range(per_step):
        b = s * per_step + bb
        cols = slice(bb * blk, (bb + 1) * blk)
        means = jnp.where(lane_d == b, jnp.sum(k32[:, cols], axis=1, keepdims=True) * (1.0 / blk), means)
        sc = scores[:, cols] + slope * (kcol + (b * blk - p_len).astype(F32))
        mx = jnp.max(sc, axis=1, keepdims=True)
        p = jnp.exp(sc - mx)
        probs.append(p.astype(BF16))
        mb = jnp.where(lane_q == b, mx, mb)
        lb = jnp.where(lane_q == b, jnp.sum(p, axis=1, keepdims=True), lb)
    ppb = blk // PAGE
    for bb in range(per_step):
        vblk = jnp.concatenate([_all_heads(vp[bb * ppb + r]) for r in range(ppb)], axis=1).astype(BF16)
        accb_ref[s * per_step + bb] = _dot_nt(probs[bb], vblk)

    means_ref[...] = means
    mb_ref[...] = mb
    lb_ref[...] = lb

    @pl.when(s == n_steps - 1)
    def _select_and_combine():
        lane_f = lane_q.astype(F32)
        qi = lax.broadcasted_iota(jnp.int32, (n_rows, V7X_LANES), 0) & (t_new - 1)
        gate = jnp.where(lane_q < nb, _dot_exact(q32_ref[...], means), -jnp.inf)
        sel = jnp.zeros(gate.shape, F32)
        for _ in range(MOBA_TOPK):
            best = jnp.max(gate, axis=1, keepdims=True)
            first = jnp.min(jnp.where(gate == best, lane_f, float(V7X_LANES)), axis=1, keepdims=True)
            hit = lane_f == first
            sel = jnp.where(hit, jnp.where(best > -jnp.inf, 1.0, sel), sel)
            gate = jnp.where(hit, -jnp.inf, gate)
        sc = _dot(qbd_ref[...], _all_heads(knt_ref.at[0]).astype(BF16)) + slope * lane_f
        sc = jnp.where(lane_q <= qi, sc, -jnp.inf)
        m_own = jnp.max(sc, axis=1, keepdims=True)
        p_own = jnp.exp(sc - m_own)
        l_own = jnp.sum(p_own, axis=1, keepdims=True)
        acc_own = _dot_nt(p_own.astype(BF16), _all_heads(vnt_ref.at[0]).astype(BF16))
        picked = sel > 0.0
        m_all = jnp.maximum(m_own, jnp.max(jnp.where(picked, mb, -jnp.inf), axis=1, keepdims=True))
        w = jnp.where(picked, jnp.exp(mb - m_all), 0.0)
        w_own = jnp.exp(m_own - m_all)
        den = jnp.sum(w * lb, axis=1, keepdims=True) + w_own * l_own
        num = w_own * acc_own
        for b in range(nb):
            num = num + jnp.sum(jnp.where(lane_q == b, w, 0.0), axis=1, keepdims=True) * accb_ref[b]
        o_ref[0] = _own_head_columns(num / den, t_new).astype(o_ref.dtype)


def _sample_attention(kind, layer, page_table, q, knt, vnt, cache_kt, cache_vt, lfn=None, cache_lft=None):
    g, _, t_new, _ = q.shape
    n_pages = page_table.shape[1]
    pages = min(PAGES_PER_STEP, n_pages)
    n_steps = n_pages // pages
    n_rows = N_HEADS * t_new
    assert t_new & (t_new - 1) == 0 and pages * PAGE % MOBA_BLOCK == 0
    per_seq = lambda blk: pl.BlockSpec(blk, lambda b, s, pt: (b,) + (0,) * (len(blk) - 1))
    in_hbm = pl.BlockSpec(memory_space=pl.ANY)
    in_specs = [per_seq((1, N_HEADS, t_new, HEAD_DIM)), per_seq((1, N_HEADS, HEAD_DIM, PAGE)),
                per_seq((1, N_HEADS, HEAD_DIM, PAGE))]
    args = [q, knt, vnt]
    if kind == "fox":
        in_specs.append(per_seq((1, V7X_SUBLANES, PAGE)))
        args.append(lfn)
    in_specs += [in_hbm, in_hbm]
    args += [cache_kt, cache_vt]
    stat = pltpu.VMEM((n_rows, V7X_LANES), F32)
    q_bd = pltpu.VMEM((n_rows, HEADS_W), BF16)
    page_buf = pltpu.VMEM((2, pages, N_HEADS, HEAD_DIM, PAGE), F32)
    if kind == "fox":
        in_specs.append(in_hbm)
        args.append(cache_lft)
        body = functools.partial(_fox_sample_kernel, layer=layer, pages=pages, n_steps=n_steps)
        scratch = [q_bd, stat, stat, pltpu.VMEM((n_rows, HEADS_W), F32),
                   pltpu.VMEM((pages * V7X_SUBLANES, V7X_LANES), F32),
                   page_buf, page_buf, pltpu.VMEM((2, pages, V7X_SUBLANES, PAGE), F32),
                   pltpu.SemaphoreType.DMA((2, 3))]
    else:
        p_len = n_pages * PAGE
        assert p_len // MOBA_BLOCK <= V7X_LANES
        body = functools.partial(_moba_sample_kernel, layer=layer, pages=pages, n_steps=n_steps, p_len=p_len)
        scratch = [q_bd, pltpu.VMEM((n_rows, HEADS_W), F32), pltpu.VMEM((HEADS_W, V7X_LANES), F32), stat, stat,
                   pltpu.VMEM((p_len // MOBA_BLOCK, n_rows, HEADS_W), F32),
                   page_buf, page_buf, pltpu.SemaphoreType.DMA((2, 2))]
    return pl.pallas_call(
        body,
        out_shape=jax.ShapeDtypeStruct((g, t_new, HEADS_W), BF16),
        grid_spec=pltpu.PrefetchScalarGridSpec(
            num_scalar_prefetch=1,
            grid=(g, n_steps),
            in_specs=in_specs,
            out_specs=pl.BlockSpec((1, t_new, HEADS_W), lambda b, s, pt: (b, 0, 0)),
            scratch_shapes=scratch),
        compiler_params=_params("arbitrary", "arbitrary"),
        name=kind + "_sample",
    )(page_table, *args)


def _causal_shifts(u, fill, period):
    out = []
    for j in (1, 2):
        r = pltpu.roll(u, j, axis=0)
        f = pltpu.roll(fill, j, axis=0)
        if period is None:
            row = lax.broadcasted_iota(jnp.int32, f.shape, 0)
            out.append(jnp.concatenate([jnp.where(row >= j, r[:V7X_SUBLANES], f), r[V7X_SUBLANES:]], axis=0))
        else:
            t = lax.broadcasted_iota(jnp.int32, u.shape, 0) & (period - 1)
            out.append(jnp.where(t >= j, r, f))
    return out


def _conv3(u, fill, period, w_ref, cols):
    s1, s2 = _causal_shifts(u, fill, period)
    return w_ref[0:1, cols] * s2 + w_ref[1:2, cols] * s1 + w_ref[2:3, cols] * u


def _outproj_kernel(x_ref, pc_ref, yf_ref, ym_ref, gt_ref, cw_ref, wo_ref, g_ref, b_ref, *rest,
                    period, dn_alpha):
    if period is None:
        o_ref, us_ref, carry_ref = rest

        @pl.when(pl.program_id(1) == 0)
        def _fresh_sequence():
            carry_ref[...] = jnp.zeros(carry_ref.shape, F32)

        fill = carry_ref[...]
    else:
        fill_ref, o_ref, us_ref = rest
        fill = fill_ref[0]
    pc = pc_ref[0]
    u = pc[:, CONV_W:2 * CONV_W] * pc[:, 2 * CONV_W:]
    y_conv = pc[:, :CONV_W] * _conv3(u, fill, period, cw_ref, slice(None))
    mixed = _dot(jnp.concatenate([y_conv.astype(BF16), yf_ref[0], ym_ref[0]], axis=1), wo_ref[...])
    z = dn_alpha * x_ref[0] + (1.0 + gt_ref[0]) * mixed
    o_ref[0] = _layer_norm(z, g_ref[...], b_ref[...])
    if period is None:
        tail = u[u.shape[0] - V7X_SUBLANES:]
        carry_ref[...] = tail
        us_ref[0] = tail
    else:
        us_ref[0] = u


def _outproj(x3, pconv, yf, ym, gt, conv_w, wo, ln_g, ln_b, fill, tm, dn_alpha):
    gb, t, d = x3.shape
    period = None if fill is None else V7X_SUBLANES
    tok = lambda w: pl.BlockSpec((1, tm, w), lambda g, i: (g, i, 0))
    in_specs = [tok(d), tok(3 * CONV_W), tok(HEADS_W), tok(HEADS_W), _mod_spec(gt, tm),
                _const_spec(conv_w.shape), _const_spec(wo.shape), _const_spec(ln_g.shape), _const_spec(ln_b.shape)]
    args = [x3, pconv, yf, ym, gt, conv_w, wo, ln_g, ln_b]
    scratch = []
    if period is None:
        us_rows = V7X_SUBLANES
        us_spec = pl.BlockSpec((1, us_rows, CONV_W), lambda g, i: (g, 0, 0))
        scratch.append(pltpu.VMEM((V7X_SUBLANES, CONV_W), F32))
    else:
        us_rows = t
        us_spec = tok(CONV_W)
        in_specs.append(tok(CONV_W))
        args.append(fill)
    return pl.pallas_call(
        functools.partial(_outproj_kernel, period=period, dn_alpha=dn_alpha),
        out_shape=(jax.ShapeDtypeStruct((gb, t, d), F32), jax.ShapeDtypeStruct((gb, us_rows, CONV_W), F32)),
        grid=(gb, t // tm),
        in_specs=in_specs,
        out_specs=(tok(d), us_spec),
        scratch_shapes=scratch,
        compiler_params=_params("arbitrary", "arbitrary"),
        name="outproj",
    )(*args)


def _ffn_kernel(x_ref, sc_ref, sh_ref, gt_ref, wup_ref, cw_ref, cb_ref, wdn_ref, g_ref, b_ref, *rest,
                period, dn_alpha):
    if period is None:
        o_ref, us_ref, carry_ref = rest

        @pl.when(pl.program_id(1) == 0)
        def _fresh_sequence():
            carry_ref[...] = jnp.zeros(carry_ref.shape, F32)
    else:
        fill_ref, o_ref, us_ref = rest
    x = x_ref[0]
    tm = x.shape[0]
    d_ff = wdn_ref.shape[0]
    h = (x * (1.0 + sc_ref[0]) + sh_ref[0]).astype(BF16)
    halves = []
    for half in range(2):
        cols = slice(half * d_ff, (half + 1) * d_ff)
        u = _dot(h, wup_ref[:, cols])
        fill = carry_ref[:, cols] if period is None else fill_ref[0, :, cols]
        halves.append(_conv3(u, fill, period, cw_ref, cols) + cb_ref[:, cols])
        if period is None:
            tail = u[tm - V7X_SUBLANES:]
            carry_ref[:, cols] = tail
            us_ref[0, :, cols] = tail
        else:
            us_ref[0, :, cols] = u
    gate, val = halves
    y = _dot((gate * _sigmoid(gate) * val).astype(BF16), wdn_ref[...])
    z = dn_alpha * x + (1.0 + gt_ref[0]) * y
    o_ref[0] = _layer_norm(z, g_ref[...], b_ref[...])


def _ffn(x3, sc, sh, gt, wup, conv_w, conv_b, wdn, ln_g, ln_b, fill, tm, dn_alpha):
    gb, t, d = x3.shape
    n_up = wup.shape[1]
    period = None if fill is None else V7X_SUBLANES
    tok = lambda w: pl.BlockSpec((1, tm, w), lambda g, i: (g, i, 0))
    in_specs = [tok(d), _mod_spec(sc, tm), _mod_spec(sh, tm), _mod_spec(gt, tm),
                _const_spec(wup.shape), _const_spec(conv_w.shape), _const_spec(conv_b.shape),
                _const_spec(wdn.shape), _const_spec(ln_g.shape), _const_spec(ln_b.shape)]
    args = [x3, sc, sh, gt, wup, conv_w, conv_b, wdn, ln_g, ln_b]
    scratch = []
    if period is None:
        us_rows = V7X_SUBLANES
        us_spec = pl.BlockSpec((1, us_rows, n_up), lambda g, i: (g, 0, 0))
        scratch.append(pltpu.VMEM((V7X_SUBLANES, n_up), F32))
    else:
        us_rows = t
        us_spec = tok(n_up)
        in_specs.append(tok(n_up))
        args.append(fill)
    return pl.pallas_call(
        functools.partial(_ffn_kernel, period=period, dn_alpha=dn_alpha),
        out_shape=(jax.ShapeDtypeStruct((gb, t, d), F32), jax.ShapeDtypeStruct((gb, us_rows, n_up), F32)),
        grid=(gb, t // tm),
        in_specs=in_specs,
        out_specs=(tok(d), us_spec),
        scratch_shapes=scratch,
        compiler_params=_params("arbitrary", "arbitrary"),
        name="ffn",
    )(*args)


def _seq_fill(state):
    g, w, c = state.shape
    padded = jnp.pad(state, ((0, 0), (V7X_SUBLANES - w, 0), (0, 0)))
    return jnp.roll(padded, -1, axis=0).reshape(1, g * V7X_SUBLANES, c)


def _per_seq_transposed(xt, g, t_new):
    return xt.reshape(N_HEADS, HEAD_DIM, g, t_new).transpose(2, 0, 1, 3)


def _new_token_blocks(xt, g, t_new):
    return jnp.pad(_per_seq_transposed(xt, g, t_new), ((0, 0), (0, 0), (0, 0), (0, PAGE - t_new)))


def _layer(l, depth, x3, mods, weights, kv_states, dn_alpha, tm, sample=None):
    sh_a, sc_a, gt_a, sh_f, sc_f, gt_f = mods
    (wtok, wqkvt, bf, conv_w, wo, ln1_g, ln1_b, wup, ffn_w, ffn_b, wdn, ln2_g, ln2_b) = weights
    pconv, lf, fqt, mqt, *kv_states = _inproj(l, depth, x3, sc_a, sh_a, wtok, wqkvt, bf, kv_states, tm)
    fkt, fvt, mkt, mvt = kv_states
    if sample is None:
        y_fox = _prompt_attention("fox", l, fqt, fkt, fvt, lf)
        y_moba = _prompt_attention("moba", l, mqt, mkt, mvt)
        conv_fill = ffn_fill = None
    else:
        g, t_new = sample["g"], sample["t"]
        tok_q = lambda qt: _per_seq_transposed(qt[0], g, t_new).transpose(0, 1, 3, 2)
        new = lambda st: _new_token_blocks(st[l, 0], g, t_new)
        lfn = lf[0].reshape(g, t_new, V7X_LANES)[:, :, :V7X_SUBLANES].transpose(0, 2, 1)
        lfn = jnp.pad(lfn, ((0, 0), (0, 0), (0, PAGE - t_new)))
        y_fox = _sample_attention("fox", l, sample["page_table"], tok_q(fqt), new(fkt), new(fvt),
                                  sample["fox_kt"], sample["fox_vt"], lfn, sample["fox_lft"])
        y_moba = _sample_attention("moba", l, sample["page_table"], tok_q(mqt), new(mkt), new(mvt),
                                   sample["moba_kt"], sample["moba_vt"])
        y_fox = y_fox.reshape(1, g * t_new, HEADS_W)
        y_moba = y_moba.reshape(1, g * t_new, HEADS_W)
        conv_fill = _seq_fill(sample["state_conv"][l])
        ffn_fill = _seq_fill(sample["state_ffn"][l])
    x1, conv_tail = _outproj(x3, pconv, y_fox, y_moba, gt_a, conv_w, wo, ln1_g, ln1_b, conv_fill, tm, dn_alpha)
    x2, ffn_tail = _ffn(x1, sc_f, sh_f, gt_f, wup, ffn_w, ffn_b, wdn, ln2_g, ln2_b, ffn_fill, tm, dn_alpha)
    return x2, kv_states, (lf, conv_tail, ffn_tail)


def kernel(x_prompt, x_sample, cache_fox_k, cache_fox_v, cache_fox_logf, cache_moba_k, cache_moba_v, state_conv, state_ffn, page_table, c_prompt, c_sample, w_ada, b_ada, w_in, b_f, conv_w, w_o, ln1_g, ln1_b, w_up, ffn_conv_w, ffn_conv_b, w_down, ln2_g, ln2_b):
    depth, d, _ = w_ada.shape
    gp, tp, _ = x_prompt.shape
    gs, ts, _ = x_sample.shape
    dn_alpha = (2 * depth) ** 0.25
    n_conv, n_qkv = 3 * CONV_W, 3 * HEADS_W

    ada = _ada(jnp.concatenate([c_prompt, c_sample], axis=0), w_ada, b_ada)
    to_kt = lambda c: jnp.transpose(c, (0, 1, 3, 4, 2))
    lft_cache = jnp.pad(jnp.transpose(cache_fox_logf, (0, 1, 3, 2)),
                        ((0, 0), (0, 0), (0, V7X_SUBLANES - N_HEADS), (0, 0)))
    sample = dict(g=gs, t=ts, page_table=page_table, state_conv=state_conv, state_ffn=state_ffn,
                  fox_kt=to_kt(cache_fox_k), fox_vt=to_kt(cache_fox_v), fox_lft=lft_cache,
                  moba_kt=to_kt(cache_moba_k), moba_vt=to_kt(cache_moba_v))

    xp = x_prompt
    xs = x_sample.reshape(1, gs * ts, d)
    kv_p = kv_s = None
    small_p, small_s = [], []
    for l in range(depth):
        w = w_in[l]
        fox0, ff0, moba0 = n_conv, n_conv + n_qkv, n_conv + n_qkv + N_HEADS
        wtok = jnp.concatenate([w[:, :n_conv], jnp.pad(w[:, ff0:moba0], ((0, 0), (0, V7X_LANES - N_HEADS)))],
                               axis=1).astype(BF16)
        wqkvt = jnp.concatenate([w[:, fox0:ff0], w[:, moba0:]], axis=1).T.astype(BF16)
        bf = jnp.pad(b_f[l], (0, V7X_LANES - N_HEADS))[None]
        weights = (wtok, wqkvt, bf, conv_w[l], w_o[l].astype(BF16), ln1_g[l][None], ln1_b[l][None],
                   w_up[l].astype(BF16), ffn_conv_w[l], ffn_conv_b[l][None], w_down[l].astype(BF16),
                   ln2_g[l][None], ln2_b[l][None])
        mods_p = tuple(a[:, None, :] for a in jnp.split(ada[l, :gp], 6, axis=-1))
        mods_s = tuple(jnp.repeat(a, ts, axis=0)[None] for a in jnp.split(ada[l, gp:], 6, axis=-1))
        xp, kv_p, st_p = _layer(l, depth, xp, mods_p, weights, kv_p, dn_alpha, tm=512)
        xs, kv_s, st_s = _layer(l, depth, xs, mods_s, weights, kv_s, dn_alpha, tm=gs * ts, sample=sample)
        small_p.append(st_p)
        small_s.append(st_s)

    stack = lambda states, i: jnp.stack([s[i] for s in states])
    out = [xp, xs.reshape(gs, ts, d)]
    fk, fv, mk, mv = (jnp.transpose(a, (0, 1, 4, 2, 3)) for a in kv_p)
    out += [fk, fv, stack(small_p, 0)[..., :N_HEADS], mk, mv,
            stack(small_p, 1)[:, :, V7X_SUBLANES - 2:], stack(small_p, 2)[:, :, V7X_SUBLANES - 2:]]
    fk, fv, mk, mv = (jnp.transpose(a[:, 0].reshape(depth, N_HEADS, HEAD_DIM, gs, ts), (0, 3, 4, 1, 2))
                      for a in kv_s)
    tail_s = lambda i: stack(small_s, i).reshape(depth, gs, ts, -1)[:, :, ts - 2:]
    out += [fk, fv, stack(small_s, 0).reshape(depth, gs, ts, V7X_LANES)[..., :N_HEADS], mk, mv,
            tail_s(1), tail_s(2)]
    return tuple(out)
```

```python
import functools

import jax
import jax.numpy as jnp
from jax import lax
from jax.experimental import pallas as pl
from jax.experimental.pallas import tpu as pltpu

F32 = jnp.float32
BF16 = jnp.bfloat16
HIGHEST = lax.Precision.HIGHEST

HEAD_DIM = 64
N_HEADS = 6
HEADS_W = N_HEADS * HEAD_DIM
CONV_W = 256
MOBA_BLOCK = 256
MOBA_TOPK = 3
PAGE = 128
LN_EPS = 1e-5
ATTN_SCALE = HEAD_DIM ** -0.5
LOG2E = 1.4426950408889634
ALIBI_SLOPES = tuple(2.0 ** (-8.0 * (h + 1) / N_HEADS) for h in range(N_HEADS))
MASKED = -1e30

V7X_LANES = 128
V7X_SUBLANES = 8
V7X_VMEM_BYTES = 64 * 1024 * 1024
VMEM_LIMIT = 52 * 1024 * 1024
PAGES_PER_STEP = 16
AUG_KEY_TERM = HEAD_DIM
AUG_QUERY_TERM = HEAD_DIM + 3
AUG_END = HEAD_DIM + 6


def _dot(a, b):
    return jnp.dot(a, b, preferred_element_type=F32)


def _dot_nt(a, b):
    return lax.dot_general(a, b, (((1,), (1,)), ((), ())), preferred_element_type=F32)


def _dot_exact(a, b):
    return jnp.dot(a, b, precision=HIGHEST, preferred_element_type=F32)


def _sigmoid(x):
    return 1.0 / (1.0 + jnp.exp(-x))


def _log_sigmoid(x):
    return jnp.minimum(x, 0.0) - jnp.log(1.0 + jnp.exp(-jnp.abs(x)))


def _layer_norm(z, g, b):
    mu = jnp.mean(z, axis=-1, keepdims=True)
    zc = z - mu
    var = jnp.mean(zc * zc, axis=-1, keepdims=True)
    return zc * lax.rsqrt(var + LN_EPS) * g + b


def _split3(x):
    hi = x.astype(BF16).astype(F32)
    rest = x - hi
    mid = rest.astype(BF16).astype(F32)
    return hi, mid, rest - mid


def _params(*semantics):
    return pltpu.CompilerParams(dimension_semantics=semantics, vmem_limit_bytes=VMEM_LIMIT)


def _const_spec(shape):
    zeros = (0,) * len(shape)
    return pl.BlockSpec(shape, lambda *_: zeros)


def _mod_spec(arr, tm):
    d = arr.shape[-1]
    if arr.shape[1] == 1:
        return pl.BlockSpec((1, 1, d), lambda g, i: (g, 0, 0))
    return pl.BlockSpec((1, tm, d), lambda g, i: (g, i, 0))


def _ada_kernel(c_ref, w_ref, b_ref, o_ref):
    c = c_ref[...]
    o_ref[0] = _dot_exact(c * _sigmoid(c), w_ref[0]) + b_ref[0]


def _ada(c_all, w_ada, b_ada):
    depth, d, n = w_ada.shape
    g = c_all.shape[0]
    tn = 1536
    return pl.pallas_call(
        _ada_kernel,
        out_shape=jax.ShapeDtypeStruct((depth, g, n), F32),
        grid=(depth, n // tn),
        in_specs=[pl.BlockSpec((g, d), lambda l, j: (0, 0)),
                  pl.BlockSpec((1, d, tn), lambda l, j: (l, 0, j)),
                  pl.BlockSpec((1, 1, tn), lambda l, j: (l, 0, j))],
        out_specs=pl.BlockSpec((1, g, tn), lambda l, j: (l, 0, j)),
        compiler_params=_params("arbitrary", "arbitrary"),
        name="ada",
    )(c_all, w_ada, b_ada.reshape(depth, 1, n))


N_KV_STATES = 4


def _inproj_kernel(x_ref, sc_ref, sh_ref, wtok_ref, wqkvt_ref, bf_ref, *rest):
    conv_ref, lf_ref, fqt_ref, mqt_ref, fkt_ref, fvt_ref, mkt_ref, mvt_ref = rest[len(rest) - 8:]
    h = (x_ref[0] * (1.0 + sc_ref[0]) + sh_ref[0]).astype(BF16)
    conv_ref[0] = _dot(h, wtok_ref[:, 0:3 * CONV_W])
    lf_ref[0] = _log_sigmoid(_dot(h, wtok_ref[:, 3 * CONV_W:]) + bf_ref[...])
    for c, ref in enumerate((fqt_ref, fkt_ref, fvt_ref, mqt_ref, mkt_ref, mvt_ref)):
        t = _dot_nt(wqkvt_ref[c * HEADS_W:(c + 1) * HEADS_W, :], h)
        ref[0] = t.reshape(N_HEADS, HEAD_DIM, t.shape[-1])


def _inproj(layer, depth, x3, sc, sh, wtok, wqkvt, bf, kv_states, tm):
    gb, t, d = x3.shape
    tok = lambda w: pl.BlockSpec((1, tm, w), lambda g, i: (g, i, 0))
    qt = pl.BlockSpec((1, N_HEADS, HEAD_DIM, tm), lambda g, i: (g, 0, 0, i))
    kvt = pl.BlockSpec((None, 1, N_HEADS, HEAD_DIM, tm), lambda g, i: (layer, g, 0, 0, i))
    qt_shape = jax.ShapeDtypeStruct((gb, N_HEADS, HEAD_DIM, t), F32)
    kvt_shape = jax.ShapeDtypeStruct((depth, gb, N_HEADS, HEAD_DIM, t), F32)
    in_specs = [tok(d), _mod_spec(sc, tm), _mod_spec(sh, tm),
                _const_spec(wtok.shape), _const_spec(wqkvt.shape), _const_spec(bf.shape)]
    args = [x3, sc, sh, wtok, wqkvt, bf]
    aliases = {}
    if kv_states is not None:
        aliases = {len(args) + k: 4 + k for k in range(N_KV_STATES)}
        in_specs += [pl.BlockSpec(memory_space=pl.ANY)] * N_KV_STATES
        args += list(kv_states)
    return pl.pallas_call(
        _inproj_kernel,
        out_shape=(jax.ShapeDtypeStruct((gb, t, 3 * CONV_W), F32),
                   jax.ShapeDtypeStruct((gb, t, V7X_LANES), F32),
                   qt_shape, qt_shape, kvt_shape, kvt_shape, kvt_shape, kvt_shape),
        grid=(gb, t // tm),
        in_specs=in_specs,
        out_specs=(tok(3 * CONV_W), tok(V7X_LANES), qt, qt, kvt, kvt, kvt, kvt),
        input_output_aliases=aliases,
        compiler_params=_params("arbitrary", "arbitrary"),
        name="inproj",
    )(*args)


def _key_term_movers():
    row = lax.broadcasted_iota(jnp.int32, (V7X_LANES, V7X_LANES), 0)
    col = lax.broadcasted_iota(jnp.int32, (V7X_LANES, V7X_LANES), 1)
    movers = []
    for h in range(N_HEADS):
        m = jnp.zeros((V7X_LANES, V7X_LANES), F32)
        for k in range(3):
            m = jnp.where(row == h + V7X_SUBLANES * k, jnp.where(col == AUG_KEY_TERM + k, 1.0, m), m)
        movers.append(m.astype(BF16))
    return movers


def _pack_key_terms(term, lane):
    hi, mid, lo = _split3(jnp.where(lane < N_HEADS, term, 0.0))
    return (hi + pltpu.roll(mid, V7X_SUBLANES, axis=1) + pltpu.roll(lo, 2 * V7X_SUBLANES, axis=1)).astype(BF16)


def _augment_keys(k_tok, packed_terms, mover, lane):
    x = jnp.where(lane < HEAD_DIM, jnp.concatenate([k_tok, jnp.zeros_like(k_tok)], axis=1),
                  jnp.where(lane < AUG_QUERY_TERM, _dot(packed_terms, mover), jnp.where(lane < AUG_END, 1.0, 0.0)))
    return x.astype(BF16)


def _augment_queries(qt, query_term):
    row = lax.broadcasted_iota(jnp.int32, qt.shape, 0)
    tail = jnp.where(row < AUG_QUERY_TERM - HEAD_DIM, 1.0, 0.0)
    if query_term is not None:
        for k, piece in enumerate(_split3(query_term)):
            tail = jnp.where(row == AUG_QUERY_TERM - HEAD_DIM + k, piece, tail)
    return jnp.concatenate([qt, tail], axis=0).astype(BF16)


def _flash_heads(i, tq, ka_ref, vb_ref, qa_ref, s_ref, p_ref, acc_ref, past_bias):
    key = lax.broadcasted_iota(jnp.int32, (tq, tq), 0)
    qry = lax.broadcasted_iota(jnp.int32, (tq, tq), 1)
    causal = key <= qry

    def step(tiles, carry):
        for t, (j, diag) in enumerate(tiles):
            for h in range(N_HEADS):
                s = _dot(ka_ref[j, h], qa_ref[h])
                if diag:
                    s = jnp.where(causal, s, -jnp.inf)
                elif past_bias is not None:
                    s = s + past_bias(h, j)
                s_ref[t, h] = s
        out, alphas = [], []
        for h in range(N_HEADS):
            m, l = carry[h]
            scores = [s_ref[t, h] for t in range(len(tiles))]
            m_new = m
            for s in scores:
                m_new = jnp.maximum(m_new, jnp.max(s, axis=0, keepdims=True))
            alpha = jnp.exp2(m - m_new)
            l = alpha * l
            for t, s in enumerate(scores):
                p = jnp.exp2(s - m_new)
                p_ref[t, h] = p.astype(BF16)
                l = l + jnp.sum(p, axis=0, keepdims=True)
            out.append((m_new, l))
            alphas.append(alpha)
        for h in range(N_HEADS):
            acc = alphas[h] * acc_ref[h]
            for t, (j, _) in enumerate(tiles):
                acc = acc + _dot(vb_ref[j, h], p_ref[t, h])
            acc_ref[h] = acc
        return tuple(out)

    acc_ref[...] = jnp.zeros(acc_ref.shape, F32)
    init = tuple((jnp.full((1, tq), -jnp.inf, F32), jnp.zeros((1, tq), F32)) for _ in range(N_HEADS))
    carry = lax.fori_loop(0, i // 2, lambda jj, c: step([(2 * jj, False), (2 * jj + 1, False)], c), init)
    carry = lax.cond(i % 2 == 1, lambda c: step([(i - 1, False), (i, True)], c),
                     lambda c: step([(i, True)], c), carry)
    return [(acc_ref[h] / carry[h][1]).T for h in range(N_HEADS)]


def _fox_prompt_kernel(qt_ref, kt_ref, vt_ref, lf_ref, o_ref, ka_ref, vb_ref, *work, tq, nkt):
    i = pl.program_id(1)

    @pl.when(i == 0)
    def _prepare_sequence():
        rows = lax.broadcasted_iota(jnp.int32, (tq, tq), 0)
        cols = lax.broadcasted_iota(jnp.int32, (tq, tq), 1)
        lower = (cols <= rows).astype(F32)
        lane = lax.broadcasted_iota(jnp.int32, (tq, V7X_LANES), 1)
        movers = _key_term_movers()
        carry = jnp.zeros((1, V7X_LANES), F32)
        terms = []
        for c in range(nkt):
            sl = slice(c * tq, (c + 1) * tq)
            cum = _dot_exact(lower, lf_ref[0, sl, :]) + carry
            carry = cum[tq - 1:tq, :]
            vb_ref[c] = vt_ref[0, :, :, sl].astype(BF16)
            terms.append(_pack_key_terms(-LOG2E * cum, lane))
        for h in range(N_HEADS):
            for c in range(nkt):
                ka_ref[c, h] = _augment_keys(kt_ref[0, h, :, c * tq:(c + 1) * tq].T, terms[c], movers[h], lane)

    qt = qt_ref[0] * (ATTN_SCALE * LOG2E)
    qa_ref = work[0]
    for h in range(N_HEADS):
        qa_ref[h] = _augment_queries(qt[h], None)
    heads = _flash_heads(i, tq, ka_ref, vb_ref, *work, None)
    o_ref[0] = jnp.concatenate(heads, axis=1).astype(o_ref.dtype)


def _moba_prompt_kernel(qt_ref, kt_ref, vt_ref, o_ref, ka_ref, vb_ref, means_ref, *work, nb):
    blk = MOBA_BLOCK
    i = pl.program_id(1)

    @pl.when(i == 0)
    def _prepare_sequence():
        lane = lax.broadcasted_iota(jnp.int32, (blk, V7X_LANES), 1)
        pos = lax.broadcasted_iota(jnp.int32, (blk, V7X_LANES), 0)
        brow = lax.broadcasted_iota(jnp.int32, (V7X_SUBLANES, HEAD_DIM), 0)
        means = [jnp.zeros((V7X_SUBLANES, HEAD_DIM), F32) for _ in range(N_HEADS)]
        movers = _key_term_movers()
        slopes = jnp.zeros((1, V7X_LANES), F32)
        for h in range(N_HEADS):
            slopes = jnp.where(lane[0:1] == h, LOG2E * ALIBI_SLOPES[h], slopes)
        terms = [_pack_key_terms((pos + c * blk).astype(F32) * slopes, lane) for c in range(nb)]
        for c in range(nb):
            vb_ref[c] = vt_ref[0, :, :, c * blk:(c + 1) * blk].astype(BF16)
        for h in range(N_HEADS):
            for c in range(nb):
                k_tok = kt_ref[0, h, :, c * blk:(c + 1) * blk].T
                means[h] = jnp.where(brow == c, jnp.sum(k_tok, axis=0, keepdims=True) * (1.0 / blk), means[h])
                ka_ref[c, h] = _augment_keys(k_tok, terms[c], movers[h], lane)
        for h in range(N_HEADS):
            means_ref[h] = means[h]

    q_raw = qt_ref[0]
    brow = lax.broadcasted_iota(jnp.int32, (V7X_SUBLANES, blk), 0)
    brow_f = brow.astype(F32)
    qa_ref = work[0]
    biases = []
    for h in range(N_HEADS):
        gate = jnp.where(brow < i, _dot_exact(means_ref[h], q_raw[h]), -jnp.inf)
        bias = jnp.full(gate.shape, MASKED, F32)
        for _ in range(MOBA_TOPK):
            best = jnp.max(gate, axis=0, keepdims=True)
            first = jnp.min(jnp.where(gate == best, brow_f, float(V7X_SUBLANES)), axis=0, keepdims=True)
            hit = brow_f == first
            bias = jnp.where(hit, jnp.where(best > -jnp.inf, 0.0, bias), bias)
            gate = jnp.where(hit, -jnp.inf, gate)
        biases.append(bias)
        tile_term = jnp.full((1, blk), -LOG2E * ALIBI_SLOPES[h], F32) * (i * blk).astype(F32)
        qa_ref[h] = _augment_queries(q_raw[h] * (ATTN_SCALE * LOG2E), tile_term)

    def past_bias(h, j):
        return jnp.sum(jnp.where(brow == j, biases[h], 0.0), axis=0, keepdims=True)

    heads = _flash_heads(i, blk, ka_ref, vb_ref, *work, past_bias)
    o_ref[0] = jnp.concatenate(heads, axis=1).astype(o_ref.dtype)


def _prompt_attention(kind, layer, qt, kt, vt, lf=None):
    g, _, _, t = qt.shape
    tq = MOBA_BLOCK
    nkt = t // tq
    assert nkt <= V7X_SUBLANES, "block gates are kept on one sublane group"
    qspec = pl.BlockSpec((1, N_HEADS, HEAD_DIM, tq), lambda b, i: (b, 0, 0, i))
    kvspec = pl.BlockSpec((None, 1, N_HEADS, HEAD_DIM, t), lambda b, i: (layer, b, 0, 0, 0))
    scratch = [pltpu.VMEM((nkt, N_HEADS, tq, V7X_LANES), BF16), pltpu.VMEM((nkt, N_HEADS, HEAD_DIM, tq), BF16)]
    work = [pltpu.VMEM((N_HEADS, V7X_LANES, tq), BF16), pltpu.VMEM((2, N_HEADS, tq, tq), F32),
            pltpu.VMEM((2, N_HEADS, tq, tq), BF16), pltpu.VMEM((N_HEADS, HEAD_DIM, tq), F32)]
    if kind == "fox":
        body = functools.partial(_fox_prompt_kernel, tq=tq, nkt=nkt)
        in_specs = [qspec, kvspec, kvspec, pl.BlockSpec((1, t, V7X_LANES), lambda b, i: (b, 0, 0))]
        args = (qt, kt, vt, lf)
    else:
        body = functools.partial(_moba_prompt_kernel, nb=nkt)
        in_specs = [qspec, kvspec, kvspec]
        args = (qt, kt, vt)
        scratch.append(pltpu.VMEM((N_HEADS, V7X_SUBLANES, HEAD_DIM), F32))
    return pl.pallas_call(
        body,
        out_shape=jax.ShapeDtypeStruct((g, t, HEADS_W), BF16),
        grid=(g, nkt),
        in_specs=in_specs,
        out_specs=pl.BlockSpec((1, tq, HEADS_W), lambda b, i: (b, i, 0)),
        scratch_shapes=scratch + work,
        compiler_params=_params("arbitrary", "arbitrary"),
        name=kind + "_prompt",
    )(*args)


def _page_copies(pt_ref, caches, bufs, sems, layer, g, s, slot, pages):
    return [pltpu.make_async_copy(cache.at[layer, pt_ref[g, s * pages + r]], buf.at[slot, r], sems.at[slot, a])
            for r in range(pages) for a, (cache, buf) in enumerate(zip(caches, bufs))]


def _stream_pages(pt_ref, caches, bufs, sems, layer, pages, n_steps):
    g, s = pl.program_id(0), pl.program_id(1)
    step = g * n_steps + s
    slot = step & 1
    last = pl.num_programs(0) * n_steps - 1

    @pl.when(step == 0)
    def _first():
        for c in _page_copies(pt_ref, caches, bufs, sems, layer, g, s, slot, pages):
            c.start()

    @pl.when(step < last)
    def _prefetch():
        wrap = s == n_steps - 1
        g_next = jnp.where(wrap, g + 1, g)
        s_next = jnp.where(wrap, 0, s + 1)
        for c in _page_copies(pt_ref, caches, bufs, sems, layer, g_next, s_next, 1 - slot, pages):
            c.start()

    for c in _page_copies(pt_ref, caches, bufs, sems, layer, g, s, slot, pages):
        c.wait()
    return slot


def _all_heads(page_ref):
    return page_ref[...].reshape(HEADS_W, page_ref.shape[-1])


def _own_tokens(xt_ref, t_new):
    x = _all_heads(xt_ref)
    n = x.shape[1]
    return pltpu.roll(x, lax.rem(n - pl.program_id(0) * t_new, n), axis=1)[:, :PAGE]


def _block_diag_queries(qt_ref, t_new, scale):
    q_tok = (_own_tokens(qt_ref, t_new) * scale).T[:t_new]
    z = jnp.zeros((t_new, HEAD_DIM), F32)
    return jnp.concatenate(
        [jnp.concatenate([q_tok[:, h * HEAD_DIM:(h + 1) * HEAD_DIM] if c == h else z for c in range(N_HEADS)], axis=1)
         for h in range(N_HEADS)], axis=0)


def _own_head_columns(x, t):
    lane = lax.broadcasted_iota(jnp.int32, (t, HEADS_W), 1)
    out = jnp.zeros((t, HEADS_W), x.dtype)
    for h in range(N_HEADS):
        own = (lane >= h * HEAD_DIM) & (lane < (h + 1) * HEAD_DIM)
        out = jnp.where(own, x[h * t:(h + 1) * t], out)
    return out


def _per_head_rows(values, t):
    row = lax.broadcasted_iota(jnp.int32, (N_HEADS * t, 1), 0)
    out = jnp.zeros((N_HEADS * t, 1), F32)
    for h, v in enumerate(values):
        out = jnp.where((row >= h * t) & (row < (h + 1) * t), v, out)
    return out


def _fox_sample_kernel(pt_ref, q_ref, knt_ref, vnt_ref, lfn_ref, ck_ref, cv_ref, cl_ref, o_ref,
                       qbd_ref, m_ref, l_ref, acc_ref, carry_ref, kbuf, vbuf, lbuf, sems, *,
                       layer, pages, n_steps, t_new):
    slot = _stream_pages(pt_ref, (ck_ref, cv_ref, cl_ref), (kbuf, vbuf, lbuf), sems, layer, pages, n_steps)
    kp, vp, lp = ([buf.at[slot, r] for r in range(pages)] for buf in (kbuf, vbuf, lbuf))
    s = pl.program_id(1)
    n_rows = N_HEADS * t_new

    @pl.when(s == 0)
    def _init():
        qbd_ref[...] = _block_diag_queries(q_ref, t_new, ATTN_SCALE).astype(BF16)
        m_ref[...] = jnp.full(m_ref.shape, -jnp.inf, F32)
        l_ref[...] = jnp.zeros(l_ref.shape, F32)
        acc_ref[...] = jnp.zeros(acc_ref.shape, F32)
        carry_ref[...] = jnp.zeros(carry_ref.shape, F32)

    m, l, acc = m_ref[:, 0:1], l_ref[:, 0:1], acc_ref[...]

    n_rows_lf = pages * V7X_SUBLANES
    lf = jnp.concatenate([lp[r][...] for r in range(pages)], axis=0)
    ri = lax.broadcasted_iota(jnp.int32, (PAGE, PAGE), 0)
    ci = lax.broadcasted_iota(jnp.int32, (PAGE, PAGE), 1)
    upper = jnp.where(ri <= ci, 1.0, 0.0).astype(BF16)
    pi = lax.broadcasted_iota(jnp.int32, (n_rows_lf, n_rows_lf), 0)
    pj = lax.broadcasted_iota(jnp.int32, (n_rows_lf, n_rows_lf), 1)
    same_head = (pi & (V7X_SUBLANES - 1)) == (pj & (V7X_SUBLANES - 1))
    earlier_page = jnp.where(same_head, jnp.where(pj < (pi & -V7X_SUBLANES), 1.0, 0.0), 0.0).astype(BF16)
    any_page = jnp.where(same_head, 1.0, 0.0).astype(BF16)
    lf_pieces = [x.astype(BF16) for x in _split3(lf)]
    totals = sum(_dot(x, jnp.ones((PAGE, PAGE), BF16)) for x in lf_pieces)
    total_pieces = [x.astype(BF16) for x in _split3(totals)]
    carry = carry_ref[...]
    cum = (sum(_dot(x, upper) for x in lf_pieces) + sum(_dot(earlier_page, x) for x in total_pieces)) + carry
    carry = carry + sum(_dot(any_page, x) for x in total_pieces)
    key_gate = jnp.concatenate(
        [jnp.concatenate([jnp.broadcast_to(cum[r * V7X_SUBLANES + h:r * V7X_SUBLANES + h + 1, :], (t_new, PAGE))
                          for r in range(pages)], axis=1) for h in range(N_HEADS)], axis=0)

    def update(scores, vt, m, l, acc):
        m_new = jnp.maximum(m, jnp.max(scores, axis=1, keepdims=True))
        alpha = jnp.exp(m - m_new)
        p = jnp.exp(scores - m_new)
        return m_new, alpha * l + jnp.sum(p, axis=1, keepdims=True), alpha * acc + _dot_nt(p.astype(BF16), vt)

    kt = jnp.concatenate([_all_heads(kp[r]) for r in range(pages)], axis=1).astype(BF16)
    vt = jnp.concatenate([_all_heads(vp[r]) for r in range(pages)], axis=1).astype(BF16)
    m, l, acc = update(_dot(qbd_ref[...], kt) - key_gate, vt, m, l, acc)

    carry_ref[...] = carry
    m_ref[...] = jnp.broadcast_to(m, m_ref.shape)
    l_ref[...] = jnp.broadcast_to(l, l_ref.shape)
    acc_ref[...] = acc

    @pl.when(s == n_steps - 1)
    def _new_tokens():
        lane = lax.broadcasted_iota(jnp.int32, (n_rows, PAGE), 1)
        qi = lax.broadcasted_iota(jnp.int32, (n_rows, PAGE), 0) & (t_new - 1)
        cum_n = sum(_dot(x.astype(BF16), upper) for x in _split3(lfn_ref[0])) + carry[:V7X_SUBLANES]
        gate_n = jnp.concatenate([jnp.broadcast_to(cum_n[h:h + 1, :], (t_new, PAGE)) for h in range(N_HEADS)], axis=0)
        scores = jnp.where(lane <= qi, _dot(qbd_ref[...], _own_tokens(knt_ref, t_new).astype(BF16)) - gate_n, -jnp.inf)
        _, l_fin, acc_fin = update(scores, _own_tokens(vnt_ref, t_new).astype(BF16), m, l, acc)
        o_ref[0] = _own_head_columns(acc_fin / l_fin, t_new).astype(o_ref.dtype)


def _moba_sample_kernel(pt_ref, q_ref, knt_ref, vnt_ref, ck_ref, cv_ref, o_ref, qbd_ref, q32_ref, means_ref,
                        mb_ref, lb_ref, accb_ref, kbuf, vbuf, sems, *, layer, pages, n_steps, p_len, t_new):
    slot = _stream_pages(pt_ref, (ck_ref, cv_ref), (kbuf, vbuf), sems, layer, pages, n_steps)
    kp, vp = ([buf.at[slot, r] for r in range(pages)] for buf in (kbuf, vbuf))
    blk = MOBA_BLOCK
    nb = p_len // blk
    per_step = pages * PAGE // blk
    s = pl.program_id(1)
    n_rows = N_HEADS * t_new

    @pl.when(s == 0)
    def _init():
        q32_ref[...] = _block_diag_queries(q_ref, t_new, 1.0)
        qbd_ref[...] = _block_diag_queries(q_ref, t_new, ATTN_SCALE).astype(BF16)
        means_ref[...] = jnp.zeros(means_ref.shape, F32)
        mb_ref[...] = jnp.full(mb_ref.shape, -jnp.inf, F32)
        lb_ref[...] = jnp.zeros(lb_ref.shape, F32)

    means, mb, lb = means_ref[...], mb_ref[...], lb_ref[...]
    lane_q = lax.broadcasted_iota(jnp.int32, (n_rows, V7X_LANES), 1)
    lane_d = lax.broadcasted_iota(jnp.int32, (HEADS_W, V7X_LANES), 1)
    kcol = lax.broadcasted_iota(jnp.int32, (1, blk), 1).astype(F32)
    slope = _per_head_rows(ALIBI_SLOPES, t_new)

    k32 = jnp.concatenate([_all_heads(kp[r]) for r in range(pages)], axis=1)
    scores = _dot(qbd_ref[...], k32.astype(BF16))
    probs = []
    for bb in range(per_step):
        b = s * per_step + bb
        cols = slice(bb * blk, (bb + 1) * blk)
        means = jnp.where(lane_d == b, jnp.sum(k32[:, cols], axis=1, keepdims=True) * (1.0 / blk), means)
        sc = scores[:, cols] + slope * (kcol + (b * blk - p_len).astype(F32))
        mx = jnp.max(sc, axis=1, keepdims=True)
        p = jnp.exp(sc - mx)
        probs.append(p.astype(BF16))
        mb = jnp.where(lane_q == b, mx, mb)
        lb = jnp.where(lane_q == b, jnp.sum(p, axis=1, keepdims=True), lb)
    ppb = blk // PAGE
    for bb in range(per_step):
        vblk = jnp.concatenate([_all_heads(vp[bb * ppb + r]) for r in range(ppb)], axis=1).astype(BF16)
        accb_ref[s * per_step + bb] = _dot_nt(probs[bb], vblk)

    means_ref[...] = means
    mb_ref[...] = mb
    lb_ref[...] = lb

    @pl.when(s == n_steps - 1)
    def _select_and_combine():
        lane_f = lane_q.astype(F32)
        qi = lax.broadcasted_iota(jnp.int32, (n_rows, V7X_LANES), 0) & (t_new - 1)
        gate = jnp.where(lane_q < nb, _dot_exact(q32_ref[...], means), -jnp.inf)
        sel = jnp.zeros(gate.shape, F32)
        for _ in range(MOBA_TOPK):
            best = jnp.max(gate, axis=1, keepdims=True)
            first = jnp.min(jnp.where(gate == best, lane_f, float(V7X_LANES)), axis=1, keepdims=True)
            hit = lane_f == first
            sel = jnp.where(hit, jnp.where(best > -jnp.inf, 1.0, sel), sel)
            gate = jnp.where(hit, -jnp.inf, gate)
        sc = _dot(qbd_ref[...], _own_tokens(knt_ref, t_new).astype(BF16)) + slope * lane_f
        sc = jnp.where(lane_q <= qi, sc, -jnp.inf)
        m_own = jnp.max(sc, axis=1, keepdims=True)
        p_own = jnp.exp(sc - m_own)
        l_own = jnp.sum(p_own, axis=1, keepdims=True)
        acc_own = _dot_nt(p_own.astype(BF16), _own_tokens(vnt_ref, t_new).astype(BF16))
        picked = sel > 0.0
        m_all = jnp.maximum(m_own, jnp.max(jnp.where(picked, mb, -jnp.inf), axis=1, keepdims=True))
        w = jnp.where(picked, jnp.exp(mb - m_all), 0.0)
        w_own = jnp.exp(m_own - m_all)
        den = jnp.sum(w * lb, axis=1, keepdims=True) + w_own * l_own
        num = w_own * acc_own
        for b in range(nb):
            num = num + jnp.sum(jnp.where(lane_q == b, w, 0.0), axis=1, keepdims=True) * accb_ref[b]
        o_ref[0] = _own_head_columns(num / den, t_new).astype(o_ref.dtype)


def _sample_attention(kind, layer, g, t_new, page_table, qt, knt, vnt, cache_kt, cache_vt, lfn=None, cache_lft=None):
    n_pages = page_table.shape[1]
    pages = min(PAGES_PER_STEP, n_pages)
    n_steps = n_pages // pages
    n_rows = N_HEADS * t_new
    n_tok = g * t_new
    assert t_new & (t_new - 1) == 0 and pages * PAGE % MOBA_BLOCK == 0 and n_tok % V7X_LANES == 0
    per_seq = lambda blk: pl.BlockSpec(blk, lambda b, s, pt: (b,) + (0,) * (len(blk) - 1))
    in_hbm = pl.BlockSpec(memory_space=pl.ANY)
    new_kv = pl.BlockSpec((None, None, N_HEADS, HEAD_DIM, n_tok), lambda b, s, pt: (layer, 0, 0, 0, 0))
    in_specs = [pl.BlockSpec((None, N_HEADS, HEAD_DIM, n_tok), lambda b, s, pt: (0, 0, 0, 0)), new_kv, new_kv]
    args = [qt, knt, vnt]
    if kind == "fox":
        in_specs.append(per_seq((1, V7X_SUBLANES, PAGE)))
        args.append(lfn)
    in_specs += [in_hbm, in_hbm]
    args += [cache_kt, cache_vt]
    stat = pltpu.VMEM((n_rows, V7X_LANES), F32)
    q_bd = pltpu.VMEM((n_rows, HEADS_W), BF16)
    page_buf = pltpu.VMEM((2, pages, N_HEADS, HEAD_DIM, PAGE), F32)
    if kind == "fox":
        in_specs.append(in_hbm)
        args.append(cache_lft)
        body = functools.partial(_fox_sample_kernel, layer=layer, pages=pages, n_steps=n_steps, t_new=t_new)
        scratch = [q_bd, stat, stat, pltpu.VMEM((n_rows, HEADS_W), F32),
                   pltpu.VMEM((pages * V7X_SUBLANES, V7X_LANES), F32),
                   page_buf, page_buf, pltpu.VMEM((2, pages, V7X_SUBLANES, PAGE), F32),
                   pltpu.SemaphoreType.DMA((2, 3))]
    else:
        p_len = n_pages * PAGE
        assert p_len // MOBA_BLOCK <= V7X_LANES
        body = functools.partial(_moba_sample_kernel, layer=layer, pages=pages, n_steps=n_steps, p_len=p_len,
                                 t_new=t_new)
        scratch = [q_bd, pltpu.VMEM((n_rows, HEADS_W), F32), pltpu.VMEM((HEADS_W, V7X_LANES), F32), stat, stat,
                   pltpu.VMEM((p_len // MOBA_BLOCK, n_rows, HEADS_W), F32),
                   page_buf, page_buf, pltpu.SemaphoreType.DMA((2, 2))]
    return pl.pallas_call(
        body,
        out_shape=jax.ShapeDtypeStruct((g, t_new, HEADS_W), BF16),
        grid_spec=pltpu.PrefetchScalarGridSpec(
            num_scalar_prefetch=1,
            grid=(g, n_steps),
            in_specs=in_specs,
            out_specs=pl.BlockSpec((1, t_new, HEADS_W), lambda b, s, pt: (b, 0, 0)),
            scratch_shapes=scratch),
        compiler_params=_params("arbitrary", "arbitrary"),
        name=kind + "_sample",
    )(page_table, *args)


def _causal_shifts(u, fill, period):
    out = []
    for j in (1, 2):
        r = pltpu.roll(u, j, axis=0)
        f = pltpu.roll(fill, j, axis=0)
        if period is None:
            row = lax.broadcasted_iota(jnp.int32, f.shape, 0)
            out.append(jnp.concatenate([jnp.where(row >= j, r[:V7X_SUBLANES], f), r[V7X_SUBLANES:]], axis=0))
        else:
            t = lax.broadcasted_iota(jnp.int32, u.shape, 0) & (period - 1)
            out.append(jnp.where(t >= j, r, f))
    return out


def _conv3(u, fill, period, w_ref, cols):
    s1, s2 = _causal_shifts(u, fill, period)
    return w_ref[0:1, cols] * s2 + w_ref[1:2, cols] * s1 + w_ref[2:3, cols] * u


def _outproj_kernel(x_ref, pc_ref, yf_ref, ym_ref, gt_ref, cw_ref, wo_ref, g_ref, b_ref, *rest,
                    period, dn_alpha):
    if period is None:
        o_ref, us_ref, carry_ref = rest

        @pl.when(pl.program_id(1) == 0)
        def _fresh_sequence():
            carry_ref[...] = jnp.zeros(carry_ref.shape, F32)

        fill = carry_ref[...]
    else:
        fill_ref, o_ref, us_ref = rest
        fill = fill_ref[0]
    pc = pc_ref[0]
    u = pc[:, CONV_W:2 * CONV_W] * pc[:, 2 * CONV_W:]
    y_conv = pc[:, :CONV_W] * _conv3(u, fill, period, cw_ref, slice(None))
    mixed = _dot(jnp.concatenate([y_conv.astype(BF16), yf_ref[0], ym_ref[0]], axis=1), wo_ref[...])
    z = dn_alpha * x_ref[0] + (1.0 + gt_ref[0]) * mixed
    o_ref[0] = _layer_norm(z, g_ref[...], b_ref[...])
    if period is None:
        tail = u[u.shape[0] - V7X_SUBLANES:]
        carry_ref[...] = tail
        us_ref[0] = tail
    else:
        us_ref[0] = u


def _outproj(x3, pconv, yf, ym, gt, conv_w, wo, ln_g, ln_b, fill, tm, dn_alpha):
    gb, t, d = x3.shape
    period = None if fill is None else V7X_SUBLANES
    tok = lambda w: pl.BlockSpec((1, tm, w), lambda g, i: (g, i, 0))
    in_specs = [tok(d), tok(3 * CONV_W), tok(HEADS_W), tok(HEADS_W), _mod_spec(gt, tm),
                _const_spec(conv_w.shape), _const_spec(wo.shape), _const_spec(ln_g.shape), _const_spec(ln_b.shape)]
    args = [x3, pconv, yf, ym, gt, conv_w, wo, ln_g, ln_b]
    scratch = []
    if period is None:
        us_rows = V7X_SUBLANES
        us_spec = pl.BlockSpec((1, us_rows, CONV_W), lambda g, i: (g, 0, 0))
        scratch.append(pltpu.VMEM((V7X_SUBLANES, CONV_W), F32))
    else:
        us_rows = t
        us_spec = tok(CONV_W)
        in_specs.append(tok(CONV_W))
        args.append(fill)
    return pl.pallas_call(
        functools.partial(_outproj_kernel, period=period, dn_alpha=dn_alpha),
        out_shape=(jax.ShapeDtypeStruct((gb, t, d), F32), jax.ShapeDtypeStruct((gb, us_rows, CONV_W), F32)),
        grid=(gb, t // tm),
        in_specs=in_specs,
        out_specs=(tok(d), us_spec),
        scratch_shapes=scratch,
        compiler_params=_params("arbitrary", "arbitrary"),
        name="outproj",
    )(*args)


def _ffn_kernel(x_ref, sc_ref, sh_ref, gt_ref, wup_ref, cw_ref, cb_ref, wdn_ref, g_ref, b_ref, *rest,
                period, dn_alpha):
    if period is None:
        o_ref, us_ref, carry_ref = rest

        @pl.when(pl.program_id(1) == 0)
        def _fresh_sequence():
            carry_ref[...] = jnp.zeros(carry_ref.shape, F32)
    else:
        fill_ref, o_ref, us_ref = rest
    x = x_ref[0]
    tm = x.shape[0]
    d_ff = wdn_ref.shape[0]
    h = (x * (1.0 + sc_ref[0]) + sh_ref[0]).astype(BF16)
    halves = []
    for half in range(2):
        cols = slice(half * d_ff, (half + 1) * d_ff)
        u = _dot(h, wup_ref[:, cols])
        fill = carry_ref[:, cols] if period is None else fill_ref[0, :, cols]
        halves.append(_conv3(u, fill, period, cw_ref, cols) + cb_ref[:, cols])
        if period is None:
            tail = u[tm - V7X_SUBLANES:]
            carry_ref[:, cols] = tail
            us_ref[0, :, cols] = tail
        else:
            us_ref[0, :, cols] = u
    gate, val = halves
    y = _dot((gate * _sigmoid(gate) * val).astype(BF16), wdn_ref[...])
    z = dn_alpha * x + (1.0 + gt_ref[0]) * y
    o_ref[0] = _layer_norm(z, g_ref[...], b_ref[...])


def _ffn(x3, sc, sh, gt, wup, conv_w, conv_b, wdn, ln_g, ln_b, fill, tm, dn_alpha):
    gb, t, d = x3.shape
    n_up = wup.shape[1]
    period = None if fill is None else V7X_SUBLANES
    tok = lambda w: pl.BlockSpec((1, tm, w), lambda g, i: (g, i, 0))
    in_specs = [tok(d), _mod_spec(sc, tm), _mod_spec(sh, tm), _mod_spec(gt, tm),
                _const_spec(wup.shape), _const_spec(conv_w.shape), _const_spec(conv_b.shape),
                _const_spec(wdn.shape), _const_spec(ln_g.shape), _const_spec(ln_b.shape)]
    args = [x3, sc, sh, gt, wup, conv_w, conv_b, wdn, ln_g, ln_b]
    scratch = []
    if period is None:
        us_rows = V7X_SUBLANES
        us_spec = pl.BlockSpec((1, us_rows, n_up), lambda g, i: (g, 0, 0))
        scratch.append(pltpu.VMEM((V7X_SUBLANES, n_up), F32))
    else:
        us_rows = t
        us_spec = tok(n_up)
        in_specs.append(tok(n_up))
        args.append(fill)
    return pl.pallas_call(
        functools.partial(_ffn_kernel, period=period, dn_alpha=dn_alpha),
        out_shape=(jax.ShapeDtypeStruct((gb, t, d), F32), jax.ShapeDtypeStruct((gb, us_rows, n_up), F32)),
        grid=(gb, t // tm),
        in_specs=in_specs,
        out_specs=(tok(d), us_spec),
        scratch_shapes=scratch,
        compiler_params=_params("arbitrary", "arbitrary"),
        name="ffn",
    )(*args)


def _seq_fill(state):
    g, w, c = state.shape
    padded = jnp.pad(state, ((0, 0), (V7X_SUBLANES - w, 0), (0, 0)))
    return jnp.roll(padded, -1, axis=0).reshape(1, g * V7X_SUBLANES, c)


def _layer(l, depth, x3, mods, weights, kv_states, dn_alpha, tm, sample=None):
    sh_a, sc_a, gt_a, sh_f, sc_f, gt_f = mods
    (wtok, wqkvt, bf, conv_w, wo, ln1_g, ln1_b, wup, ffn_w, ffn_b, wdn, ln2_g, ln2_b) = weights
    pconv, lf, fqt, mqt, *kv_states = _inproj(l, depth, x3, sc_a, sh_a, wtok, wqkvt, bf, kv_states, tm)
    fkt, fvt, mkt, mvt = kv_states
    if sample is None:
        y_fox = _prompt_attention("fox", l, fqt, fkt, fvt, lf)
        y_moba = _prompt_attention("moba", l, mqt, mkt, mvt)
        conv_fill = ffn_fill = None
    else:
        g, t_new = sample["g"], sample["t"]
        lfn = lf[0].reshape(g, t_new, V7X_LANES)[:, :, :V7X_SUBLANES].transpose(0, 2, 1)
        lfn = jnp.pad(lfn, ((0, 0), (0, 0), (0, PAGE - t_new)))
        y_fox = _sample_attention("fox", l, g, t_new, sample["page_table"], fqt, fkt, fvt,
                                  sample["fox_kt"], sample["fox_vt"], lfn, sample["fox_lft"])
        y_moba = _sample_attention("moba", l, g, t_new, sample["page_table"], mqt, mkt, mvt,
                                   sample["moba_kt"], sample["moba_vt"])
        y_fox = y_fox.reshape(1, g * t_new, HEADS_W)
        y_moba = y_moba.reshape(1, g * t_new, HEADS_W)
        conv_fill = _seq_fill(sample["state_conv"][l])
        ffn_fill = _seq_fill(sample["state_ffn"][l])
    x1, conv_tail = _outproj(x3, pconv, y_fox, y_moba, gt_a, conv_w, wo, ln1_g, ln1_b, conv_fill, tm, dn_alpha)
    x2, ffn_tail = _ffn(x1, sc_f, sh_f, gt_f, wup, ffn_w, ffn_b, wdn, ln2_g, ln2_b, ffn_fill, tm, dn_alpha)
    return x2, kv_states, (lf, conv_tail, ffn_tail)


def kernel(x_prompt, x_sample, cache_fox_k, cache_fox_v, cache_fox_logf, cache_moba_k, cache_moba_v, state_conv, state_ffn, page_table, c_prompt, c_sample, w_ada, b_ada, w_in, b_f, conv_w, w_o, ln1_g, ln1_b, w_up, ffn_conv_w, ffn_conv_b, w_down, ln2_g, ln2_b):
    depth, d, _ = w_ada.shape
    gp, tp, _ = x_prompt.shape
    gs, ts, _ = x_sample.shape
    dn_alpha = (2 * depth) ** 0.25
    n_conv, n_qkv = 3 * CONV_W, 3 * HEADS_W

    ada = _ada(jnp.concatenate([c_prompt, c_sample], axis=0), w_ada, b_ada)
    to_kt = lambda c: jnp.transpose(c, (0, 1, 3, 4, 2))
    lft_cache = jnp.pad(jnp.transpose(cache_fox_logf, (0, 1, 3, 2)),
                        ((0, 0), (0, 0), (0, V7X_SUBLANES - N_HEADS), (0, 0)))
    sample = dict(g=gs, t=ts, page_table=page_table, state_conv=state_conv, state_ffn=state_ffn,
                  fox_kt=to_kt(cache_fox_k), fox_vt=to_kt(cache_fox_v), fox_lft=lft_cache,
                  moba_kt=to_kt(cache_moba_k), moba_vt=to_kt(cache_moba_v))

    xp = x_prompt
    xs = x_sample.reshape(1, gs * ts, d)
    kv_p = kv_s = None
    small_p, small_s = [], []
    for l in range(depth):
        w = w_in[l]
        fox0, ff0, moba0 = n_conv, n_conv + n_qkv, n_conv + n_qkv + N_HEADS
        wtok = jnp.concatenate([w[:, :n_conv], jnp.pad(w[:, ff0:moba0], ((0, 0), (0, V7X_LANES - N_HEADS)))],
                               axis=1).astype(BF16)
        wqkvt = jnp.concatenate([w[:, fox0:ff0], w[:, moba0:]], axis=1).T.astype(BF16)
        bf = jnp.pad(b_f[l], (0, V7X_LANES - N_HEADS))[None]
        weights = (wtok, wqkvt, bf, conv_w[l], w_o[l].astype(BF16), ln1_g[l][None], ln1_b[l][None],
                   w_up[l].astype(BF16), ffn_conv_w[l], ffn_conv_b[l][None], w_down[l].astype(BF16),
                   ln2_g[l][None], ln2_b[l][None])
        mods_p = tuple(a[:, None, :] for a in jnp.split(ada[l, :gp], 6, axis=-1))
        mods_s = tuple(jnp.repeat(a, ts, axis=0)[None] for a in jnp.split(ada[l, gp:], 6, axis=-1))
        xp, kv_p, st_p = _layer(l, depth, xp, mods_p, weights, kv_p, dn_alpha, tm=512)
        xs, kv_s, st_s = _layer(l, depth, xs, mods_s, weights, kv_s, dn_alpha, tm=gs * ts, sample=sample)
        small_p.append(st_p)
        small_s.append(st_s)

    stack = lambda states, i: jnp.stack([s[i] for s in states])
    out = [xp, xs.reshape(gs, ts, d)]
    fk, fv, mk, mv = (jnp.transpose(a, (0, 1, 4, 2, 3)) for a in kv_p)
    out += [fk, fv, stack(small_p, 0)[..., :N_HEADS], mk, mv,
            stack(small_p, 1)[:, :, V7X_SUBLANES - 2:], stack(small_p, 2)[:, :, V7X_SUBLANES - 2:]]
    fk, fv, mk, mv = (jnp.transpose(a[:, 0].reshape(depth, N_HEADS, HEAD_DIM, gs, ts), (0, 3, 4, 1, 2))
                      for a in kv_s)
    tail_s = lambda i: stack(small_s, i).reshape(depth, gs, ts, -1)[:, :, ts - 2:]
    out += [fk, fv, stack(small_s, 0).reshape(depth, gs, ts, V7X_LANES)[..., :N_HEADS], mk, mv,
            tail_s(1), tail_s(2)]
    return tuple(out)
```

```python
import functools

import jax
import jax.numpy as jnp
from jax import lax
from jax.experimental import pallas as pl
from jax.experimental.pallas import tpu as pltpu

F32 = jnp.float32
BF16 = jnp.bfloat16
HIGHEST = lax.Precision.HIGHEST

HEAD_DIM = 64
N_HEADS = 6
HEADS_W = N_HEADS * HEAD_DIM
CONV_W = 256
MOBA_BLOCK = 256
MOBA_TOPK = 3
PAGE = 128
LN_EPS = 1e-5
ATTN_SCALE = HEAD_DIM ** -0.5
LOG2E = 1.4426950408889634
ALIBI_SLOPES = tuple(2.0 ** (-8.0 * (h + 1) / N_HEADS) for h in range(N_HEADS))
MASKED = -1e30

V7X_LANES = 128
V7X_SUBLANES = 8
V7X_VMEM_BYTES = 64 * 1024 * 1024
VMEM_LIMIT = 52 * 1024 * 1024
PAGES_PER_STEP = 16
AUG_KEY_TERM = HEAD_DIM
AUG_QUERY_TERM = HEAD_DIM + 3
AUG_END = HEAD_DIM + 6


def _dot(a, b):
    return jnp.dot(a, b, preferred_element_type=F32)


def _dot_nt(a, b):
    return lax.dot_general(a, b, (((1,), (1,)), ((), ())), preferred_element_type=F32)


def _dot_exact(a, b):
    return jnp.dot(a, b, precision=HIGHEST, preferred_element_type=F32)


def _sigmoid(x):
    return 1.0 / (1.0 + jnp.exp(-x))


def _log_sigmoid(x):
    return jnp.minimum(x, 0.0) - jnp.log(1.0 + jnp.exp(-jnp.abs(x)))


def _layer_norm(z, g, b):
    mu = jnp.mean(z, axis=-1, keepdims=True)
    zc = z - mu
    var = jnp.mean(zc * zc, axis=-1, keepdims=True)
    return zc * lax.rsqrt(var + LN_EPS) * g + b


def _split3(x):
    hi = x.astype(BF16).astype(F32)
    rest = x - hi
    mid = rest.astype(BF16).astype(F32)
    return hi, mid, rest - mid


def _params(*semantics):
    return pltpu.CompilerParams(dimension_semantics=semantics, vmem_limit_bytes=VMEM_LIMIT)


def _const_spec(shape):
    zeros = (0,) * len(shape)
    return pl.BlockSpec(shape, lambda *_: zeros)


def _mod_spec(arr, tm):
    d = arr.shape[-1]
    if arr.shape[1] == 1:
        return pl.BlockSpec((1, 1, d), lambda g, i: (g, 0, 0))
    return pl.BlockSpec((1, tm, d), lambda g, i: (g, i, 0))


def _ada_kernel(c_ref, w_ref, b_ref, o_ref):
    c = c_ref[...]
    o_ref[0] = _dot_exact(c * _sigmoid(c), w_ref[0]) + b_ref[0]


def _ada(c_all, w_ada, b_ada):
    depth, d, n = w_ada.shape
    g = c_all.shape[0]
    tn = 1536
    return pl.pallas_call(
        _ada_kernel,
        out_shape=jax.ShapeDtypeStruct((depth, g, n), F32),
        grid=(depth, n // tn),
        in_specs=[pl.BlockSpec((g, d), lambda l, j: (0, 0)),
                  pl.BlockSpec((1, d, tn), lambda l, j: (l, 0, j)),
                  pl.BlockSpec((1, 1, tn), lambda l, j: (l, 0, j))],
        out_specs=pl.BlockSpec((1, g, tn), lambda l, j: (l, 0, j)),
        compiler_params=_params("arbitrary", "arbitrary"),
        name="ada",
    )(c_all, w_ada, b_ada.reshape(depth, 1, n))


N_KV_STATES = 4


def _inproj_kernel(x_ref, sc_ref, sh_ref, wtok_ref, wqkvt_ref, bf_ref, *rest):
    conv_ref, lf_ref, fqt_ref, mqt_ref, fkt_ref, fvt_ref, mkt_ref, mvt_ref = rest[len(rest) - 8:]
    h = (x_ref[0] * (1.0 + sc_ref[0]) + sh_ref[0]).astype(BF16)
    conv_ref[0] = _dot_nt(h, wtok_ref[0:3 * CONV_W, :])
    lf_ref[0] = _log_sigmoid(_dot_nt(h, wtok_ref[3 * CONV_W:, :]) + bf_ref[...])
    for c, ref in enumerate((fqt_ref, fkt_ref, fvt_ref, mqt_ref, mkt_ref, mvt_ref)):
        t = _dot_nt(wqkvt_ref[c * HEADS_W:(c + 1) * HEADS_W, :], h)
        ref[0] = t.reshape(N_HEADS, HEAD_DIM, t.shape[-1])


def _inproj(layer, depth, x3, sc, sh, wtok, wqkvt, bf, kv_states, tm):
    gb, t, d = x3.shape
    tok = lambda w: pl.BlockSpec((1, tm, w), lambda g, i: (g, i, 0))
    qt = pl.BlockSpec((1, N_HEADS, HEAD_DIM, tm), lambda g, i: (g, 0, 0, i))
    kvt = pl.BlockSpec((None, 1, N_HEADS, HEAD_DIM, tm), lambda g, i: (layer, g, 0, 0, i))
    qt_shape = jax.ShapeDtypeStruct((gb, N_HEADS, HEAD_DIM, t), F32)
    kvt_shape = jax.ShapeDtypeStruct((depth, gb, N_HEADS, HEAD_DIM, t), F32)
    in_specs = [tok(d), _mod_spec(sc, tm), _mod_spec(sh, tm),
                _const_spec(wtok.shape), _const_spec(wqkvt.shape), _const_spec(bf.shape)]
    args = [x3, sc, sh, wtok, wqkvt, bf]
    aliases = {}
    if kv_states is not None:
        aliases = {len(args) + k: 4 + k for k in range(N_KV_STATES)}
        in_specs += [pl.BlockSpec(memory_space=pl.ANY)] * N_KV_STATES
        args += list(kv_states)
    return pl.pallas_call(
        _inproj_kernel,
        out_shape=(jax.ShapeDtypeStruct((gb, t, 3 * CONV_W), F32),
                   jax.ShapeDtypeStruct((gb, t, V7X_LANES), F32),
                   qt_shape, qt_shape, kvt_shape, kvt_shape, kvt_shape, kvt_shape),
        grid=(gb, t // tm),
        in_specs=in_specs,
        out_specs=(tok(3 * CONV_W), tok(V7X_LANES), qt, qt, kvt, kvt, kvt, kvt),
        input_output_aliases=aliases,
        compiler_params=_params("arbitrary", "arbitrary"),
        name="inproj",
    )(*args)


def _key_term_movers():
    row = lax.broadcasted_iota(jnp.int32, (V7X_LANES, V7X_LANES), 0)
    col = lax.broadcasted_iota(jnp.int32, (V7X_LANES, V7X_LANES), 1)
    movers = []
    for h in range(N_HEADS):
        m = jnp.zeros((V7X_LANES, V7X_LANES), F32)
        for k in range(3):
            m = jnp.where(row == h + V7X_SUBLANES * k, jnp.where(col == AUG_KEY_TERM + k, 1.0, m), m)
        movers.append(m.astype(BF16))
    return movers


def _pack_key_terms(term, lane):
    hi, mid, lo = _split3(jnp.where(lane < N_HEADS, term, 0.0))
    return (hi + pltpu.roll(mid, V7X_SUBLANES, axis=1) + pltpu.roll(lo, 2 * V7X_SUBLANES, axis=1)).astype(BF16)


def _augment_keys(k_tok, packed_terms, mover, lane):
    x = jnp.where(lane < HEAD_DIM, jnp.concatenate([k_tok, jnp.zeros_like(k_tok)], axis=1),
                  jnp.where(lane < AUG_QUERY_TERM, _dot(packed_terms, mover), jnp.where(lane < AUG_END, 1.0, 0.0)))
    return x.astype(BF16)


def _augment_queries(qt, query_term):
    row = lax.broadcasted_iota(jnp.int32, qt.shape, 0)
    tail = jnp.where(row < AUG_QUERY_TERM - HEAD_DIM, 1.0, 0.0)
    if query_term is not None:
        for k, piece in enumerate(_split3(query_term)):
            tail = jnp.where(row == AUG_QUERY_TERM - HEAD_DIM + k, piece, tail)
    return jnp.concatenate([qt, tail], axis=0).astype(BF16)


def _flash_heads(i, tq, ka_ref, vb_ref, qa_ref, s_ref, p_ref, acc_ref, past_bias):
    key = lax.broadcasted_iota(jnp.int32, (tq, tq), 0)
    qry = lax.broadcasted_iota(jnp.int32, (tq, tq), 1)
    causal = key <= qry

    def step(tiles, carry):
        for t, (j, diag) in enumerate(tiles):
            for h in range(N_HEADS):
                s = _dot(ka_ref[j, h], qa_ref[h])
                if diag:
                    s = jnp.where(causal, s, -jnp.inf)
                elif past_bias is not None:
                    s = s + past_bias(h, j)
                s_ref[t, h] = s
        out, alphas = [], []
        for h in range(N_HEADS):
            m, l = carry[h]
            scores = [s_ref[t, h] for t in range(len(tiles))]
            m_new = m
            for s in scores:
                m_new = jnp.maximum(m_new, jnp.max(s, axis=0, keepdims=True))
            alpha = jnp.exp2(m - m_new)
            l = alpha * l
            for t, s in enumerate(scores):
                p = jnp.exp2(s - m_new)
                p_ref[t, h] = p.astype(BF16)
                l = l + jnp.sum(p, axis=0, keepdims=True)
            out.append((m_new, l))
            alphas.append(alpha)
        for h in range(N_HEADS):
            acc = alphas[h] * acc_ref[h]
            for t, (j, _) in enumerate(tiles):
                acc = acc + _dot(vb_ref[j, h], p_ref[t, h])
            acc_ref[h] = acc
        return tuple(out)

    acc_ref[...] = jnp.zeros(acc_ref.shape, F32)
    init = tuple((jnp.full((1, tq), -jnp.inf, F32), jnp.zeros((1, tq), F32)) for _ in range(N_HEADS))
    carry = lax.fori_loop(0, i // 2, lambda jj, c: step([(2 * jj, False), (2 * jj + 1, False)], c), init)
    carry = lax.cond(i % 2 == 1, lambda c: step([(i - 1, False), (i, True)], c),
                     lambda c: step([(i, True)], c), carry)
    return [(acc_ref[h] / carry[h][1]).T for h in range(N_HEADS)]


def _fox_prompt_kernel(qt_ref, kt_ref, vt_ref, lf_ref, o_ref, ka_ref, vb_ref, *work, tq, nkt):
    i = pl.program_id(1)

    @pl.when(i == 0)
    def _prepare_sequence():
        rows = lax.broadcasted_iota(jnp.int32, (tq, tq), 0)
        cols = lax.broadcasted_iota(jnp.int32, (tq, tq), 1)
        lower = (cols <= rows).astype(F32)
        lane = lax.broadcasted_iota(jnp.int32, (tq, V7X_LANES), 1)
        movers = _key_term_movers()
        carry = jnp.zeros((1, V7X_LANES), F32)
        terms = []
        for c in range(nkt):
            sl = slice(c * tq, (c + 1) * tq)
            cum = _dot_exact(lower, lf_ref[0, sl, :]) + carry
            carry = cum[tq - 1:tq, :]
            vb_ref[c] = vt_ref[0, :, :, sl].astype(BF16)
            terms.append(_pack_key_terms(-LOG2E * cum, lane))
        for h in range(N_HEADS):
            for c in range(nkt):
                ka_ref[c, h] = _augment_keys(kt_ref[0, h, :, c * tq:(c + 1) * tq].T, terms[c], movers[h], lane)

    qt = qt_ref[0] * (ATTN_SCALE * LOG2E)
    qa_ref = work[0]
    for h in range(N_HEADS):
        qa_ref[h] = _augment_queries(qt[h], None)
    heads = _flash_heads(i, tq, ka_ref, vb_ref, *work, None)
    o_ref[0] = jnp.concatenate(heads, axis=1).astype(o_ref.dtype)


def _moba_prompt_kernel(qt_ref, kt_ref, vt_ref, o_ref, ka_ref, vb_ref, means_ref, *work, nb):
    blk = MOBA_BLOCK
    i = pl.program_id(1)

    @pl.when(i == 0)
    def _prepare_sequence():
        lane = lax.broadcasted_iota(jnp.int32, (blk, V7X_LANES), 1)
        pos = lax.broadcasted_iota(jnp.int32, (blk, V7X_LANES), 0)
        brow = lax.broadcasted_iota(jnp.int32, (V7X_SUBLANES, HEAD_DIM), 0)
        means = [jnp.zeros((V7X_SUBLANES, HEAD_DIM), F32) for _ in range(N_HEADS)]
        movers = _key_term_movers()
        slopes = jnp.zeros((1, V7X_LANES), F32)
        for h in range(N_HEADS):
            slopes = jnp.where(lane[0:1] == h, LOG2E * ALIBI_SLOPES[h], slopes)
        terms = [_pack_key_terms((pos + c * blk).astype(F32) * slopes, lane) for c in range(nb)]
        for c in range(nb):
            vb_ref[c] = vt_ref[0, :, :, c * blk:(c + 1) * blk].astype(BF16)
        for h in range(N_HEADS):
            for c in range(nb):
                k_tok = kt_ref[0, h, :, c * blk:(c + 1) * blk].T
                means[h] = jnp.where(brow == c, jnp.sum(k_tok, axis=0, keepdims=True) * (1.0 / blk), means[h])
                ka_ref[c, h] = _augment_keys(k_tok, terms[c], movers[h], lane)
        for h in range(N_HEADS):
            means_ref[h] = means[h]

    q_raw = qt_ref[0]
    brow = lax.broadcasted_iota(jnp.int32, (V7X_SUBLANES, blk), 0)
    brow_f = brow.astype(F32)
    qa_ref = work[0]
    biases = []
    for h in range(N_HEADS):
        gate = jnp.where(brow < i, _dot_exact(means_ref[h], q_raw[h]), -jnp.inf)
        bias = jnp.full(gate.shape, MASKED, F32)
        for _ in range(MOBA_TOPK):
            best = jnp.max(gate, axis=0, keepdims=True)
            first = jnp.min(jnp.where(gate == best, brow_f, float(V7X_SUBLANES)), axis=0, keepdims=True)
            hit = brow_f == first
            bias = jnp.where(hit, jnp.where(best > -jnp.inf, 0.0, bias), bias)
            gate = jnp.where(hit, -jnp.inf, gate)
        biases.append(bias)
        tile_term = jnp.full((1, blk), -LOG2E * ALIBI_SLOPES[h], F32) * (i * blk).astype(F32)
        qa_ref[h] = _augment_queries(q_raw[h] * (ATTN_SCALE * LOG2E), tile_term)

    def past_bias(h, j):
        return jnp.sum(jnp.where(brow == j, biases[h], 0.0), axis=0, keepdims=True)

    heads = _flash_heads(i, blk, ka_ref, vb_ref, *work, past_bias)
    o_ref[0] = jnp.concatenate(heads, axis=1).astype(o_ref.dtype)


def _prompt_attention(kind, layer, qt, kt, vt, lf=None):
    g, _, _, t = qt.shape
    tq = MOBA_BLOCK
    nkt = t // tq
    assert nkt <= V7X_SUBLANES, "block gates are kept on one sublane group"
    qspec = pl.BlockSpec((1, N_HEADS, HEAD_DIM, tq), lambda b, i: (b, 0, 0, i))
    kvspec = pl.BlockSpec((None, 1, N_HEADS, HEAD_DIM, t), lambda b, i: (layer, b, 0, 0, 0))
    scratch = [pltpu.VMEM((nkt, N_HEADS, tq, V7X_LANES), BF16), pltpu.VMEM((nkt, N_HEADS, HEAD_DIM, tq), BF16)]
    work = [pltpu.VMEM((N_HEADS, V7X_LANES, tq), BF16), pltpu.VMEM((2, N_HEADS, tq, tq), F32),
            pltpu.VMEM((2, N_HEADS, tq, tq), BF16), pltpu.VMEM((N_HEADS, HEAD_DIM, tq), F32)]
    if kind == "fox":
        body = functools.partial(_fox_prompt_kernel, tq=tq, nkt=nkt)
        in_specs = [qspec, kvspec, kvspec, pl.BlockSpec((1, t, V7X_LANES), lambda b, i: (b, 0, 0))]
        args = (qt, kt, vt, lf)
    else:
        body = functools.partial(_moba_prompt_kernel, nb=nkt)
        in_specs = [qspec, kvspec, kvspec]
        args = (qt, kt, vt)
        scratch.append(pltpu.VMEM((N_HEADS, V7X_SUBLANES, HEAD_DIM), F32))
    return pl.pallas_call(
        body,
        out_shape=jax.ShapeDtypeStruct((g, t, HEADS_W), BF16),
        grid=(g, nkt),
        in_specs=in_specs,
        out_specs=pl.BlockSpec((1, tq, HEADS_W), lambda b, i: (b, i, 0)),
        scratch_shapes=scratch + work,
        compiler_params=_params("arbitrary", "arbitrary"),
        name=kind + "_prompt",
    )(*args)


def _page_copies(pt_ref, caches, bufs, sems, layer, g, s, slot, pages):
    return [(pltpu.make_async_copy(cache.at[layer, pt_ref[g, s * pages + r]], buf.at[slot, r], sems.at[slot, a]),
             a % 2)
            for r in range(pages) for a, (cache, buf) in enumerate(zip(caches, bufs))]


def _stream_pages(pt_ref, caches, bufs, sems, layer, pages, n_steps):
    g, s = pl.program_id(0), pl.program_id(1)
    step = g * n_steps + s
    slot = step & 1
    last = pl.num_programs(0) * n_steps - 1

    @pl.when(step == 0)
    def _first():
        for c, priority in _page_copies(pt_ref, caches, bufs, sems, layer, g, s, slot, pages):
            c.start(priority=priority)

    @pl.when(step < last)
    def _prefetch():
        wrap = s == n_steps - 1
        g_next = jnp.where(wrap, g + 1, g)
        s_next = jnp.where(wrap, 0, s + 1)
        for c, priority in _page_copies(pt_ref, caches, bufs, sems, layer, g_next, s_next, 1 - slot, pages):
            c.start(priority=priority)

    for c, _ in _page_copies(pt_ref, caches, bufs, sems, layer, g, s, slot, pages):
        c.wait()
    return slot


def _all_heads(page_ref):
    return page_ref[...].reshape(HEADS_W, page_ref.shape[-1])


def _own_tokens(xt_ref, t_new):
    x = _all_heads(xt_ref)
    n = x.shape[1]
    return pltpu.roll(x, lax.rem(n - pl.program_id(0) * t_new, n), axis=1)[:, :PAGE]


def _block_diag_queries(qt_ref, t_new, scale):
    q_tok = (_own_tokens(qt_ref, t_new) * scale).T[:t_new]
    z = jnp.zeros((t_new, HEAD_DIM), F32)
    return jnp.concatenate(
        [jnp.concatenate([q_tok[:, h * HEAD_DIM:(h + 1) * HEAD_DIM] if c == h else z for c in range(N_HEADS)], axis=1)
         for h in range(N_HEADS)], axis=0)


def _own_head_columns(x, t):
    lane = lax.broadcasted_iota(jnp.int32, (t, HEADS_W), 1)
    out = jnp.zeros((t, HEADS_W), x.dtype)
    for h in range(N_HEADS):
        own = (lane >= h * HEAD_DIM) & (lane < (h + 1) * HEAD_DIM)
        out = jnp.where(own, x[h * t:(h + 1) * t], out)
    return out


def _per_head_rows(values, t):
    row = lax.broadcasted_iota(jnp.int32, (N_HEADS * t, 1), 0)
    out = jnp.zeros((N_HEADS * t, 1), F32)
    for h, v in enumerate(values):
        out = jnp.where((row >= h * t) & (row < (h + 1) * t), v, out)
    return out


def _fox_sample_kernel(pt_ref, q_ref, knt_ref, vnt_ref, lfn_ref, ck_ref, cv_ref, cl_ref, o_ref,
                       qbd_ref, m_ref, l_ref, acc_ref, carry_ref, kbuf, vbuf, lbuf, sems, *,
                       layer, pages, n_steps, t_new):
    slot = _stream_pages(pt_ref, (ck_ref, cv_ref, cl_ref), (kbuf, vbuf, lbuf), sems, layer, pages, n_steps)
    kp, vp, lp = ([buf.at[slot, r] for r in range(pages)] for buf in (kbuf, vbuf, lbuf))
    s = pl.program_id(1)
    n_rows = N_HEADS * t_new

    @pl.when(s == 0)
    def _init():
        qbd_ref[...] = _block_diag_queries(q_ref, t_new, ATTN_SCALE).astype(BF16)
        m_ref[...] = jnp.full(m_ref.shape, -jnp.inf, F32)
        l_ref[...] = jnp.zeros(l_ref.shape, F32)
        acc_ref[...] = jnp.zeros(acc_ref.shape, F32)
        carry_ref[...] = jnp.zeros(carry_ref.shape, F32)

    m, l, acc = m_ref[:, 0:1], l_ref[:, 0:1], acc_ref[...]

    n_rows_lf = pages * V7X_SUBLANES
    lf = jnp.concatenate([lp[r][...] for r in range(pages)], axis=0)
    ri = lax.broadcasted_iota(jnp.int32, (PAGE, PAGE), 0)
    ci = lax.broadcasted_iota(jnp.int32, (PAGE, PAGE), 1)
    upper = jnp.where(ri <= ci, 1.0, 0.0).astype(BF16)
    pi = lax.broadcasted_iota(jnp.int32, (n_rows_lf, n_rows_lf), 0)
    pj = lax.broadcasted_iota(jnp.int32, (n_rows_lf, n_rows_lf), 1)
    same_head = (pi & (V7X_SUBLANES - 1)) == (pj & (V7X_SUBLANES - 1))
    earlier_page = jnp.where(same_head, jnp.where(pj < (pi & -V7X_SUBLANES), 1.0, 0.0), 0.0).astype(BF16)
    any_page = jnp.where(same_head, 1.0, 0.0).astype(BF16)
    lf_pieces = [x.astype(BF16) for x in _split3(lf)]
    totals = sum(_dot(x, jnp.ones((PAGE, PAGE), BF16)) for x in lf_pieces)
    total_pieces = [x.astype(BF16) for x in _split3(totals)]
    carry = carry_ref[...]
    cum = (sum(_dot(x, upper) for x in lf_pieces) + sum(_dot(earlier_page, x) for x in total_pieces)) + carry
    carry = carry + sum(_dot(any_page, x) for x in total_pieces)
    key_gate = jnp.concatenate(
        [jnp.concatenate([jnp.broadcast_to(cum[r * V7X_SUBLANES + h:r * V7X_SUBLANES + h + 1, :], (t_new, PAGE))
                          for r in range(pages)], axis=1) for h in range(N_HEADS)], axis=0)

    def update(scores, vt, m, l, acc):
        m_new = jnp.maximum(m, jnp.max(scores, axis=1, keepdims=True))
        alpha = jnp.exp(m - m_new)
        p = jnp.exp(scores - m_new)
        return m_new, alpha * l + jnp.sum(p, axis=1, keepdims=True), alpha * acc + _dot_nt(p.astype(BF16), vt)

    kt = jnp.concatenate([_all_heads(kp[r]) for r in range(pages)], axis=1).astype(BF16)
    vt = jnp.concatenate([_all_heads(vp[r]) for r in range(pages)], axis=1).astype(BF16)
    m, l, acc = update(_dot(qbd_ref[...], kt) - key_gate, vt, m, l, acc)

    carry_ref[...] = carry
    m_ref[...] = jnp.broadcast_to(m, m_ref.shape)
    l_ref[...] = jnp.broadcast_to(l, l_ref.shape)
    acc_ref[...] = acc

    @pl.when(s == n_steps - 1)
    def _new_tokens():
        lane = lax.broadcasted_iota(jnp.int32, (n_rows, PAGE), 1)
        qi = lax.broadcasted_iota(jnp.int32, (n_rows, PAGE), 0) & (t_new - 1)
        cum_n = sum(_dot(x.astype(BF16), upper) for x in _split3(lfn_ref[0])) + carry[:V7X_SUBLANES]
        gate_n = jnp.concatenate([jnp.broadcast_to(cum_n[h:h + 1, :], (t_new, PAGE)) for h in range(N_HEADS)], axis=0)
        scores = jnp.where(lane <= qi, _dot(qbd_ref[...], _own_tokens(knt_ref, t_new).astype(BF16)) - gate_n, -jnp.inf)
        _, l_fin, acc_fin = update(scores, _own_tokens(vnt_ref, t_new).astype(BF16), m, l, acc)
        o_ref[0] = _own_head_columns(acc_fin / l_fin, t_new).astype(o_ref.dtype)


def _moba_sample_kernel(pt_ref, q_ref, knt_ref, vnt_ref, ck_ref, cv_ref, o_ref, qbd_ref, q32_ref, means_ref,
                        mb_ref, lb_ref, accb_ref, kbuf, vbuf, sems, *, layer, pages, n_steps, p_len, t_new):
    slot = _stream_pages(pt_ref, (ck_ref, cv_ref), (kbuf, vbuf), sems, layer, pages, n_steps)
    kp, vp = ([buf.at[slot, r] for r in range(pages)] for buf in (kbuf, vbuf))
    blk = MOBA_BLOCK
    nb = p_len // blk
    per_step = pages * PAGE // blk
    s = pl.program_id(1)
    n_rows = N_HEADS * t_new

    @pl.when(s == 0)
    def _init():
        q32_ref[...] = _block_diag_queries(q_ref, t_new, 1.0)
        qbd_ref[...] = _block_diag_queries(q_ref, t_new, ATTN_SCALE).astype(BF16)
        means_ref[...] = jnp.zeros(means_ref.shape, F32)
        mb_ref[...] = jnp.full(mb_ref.shape, -jnp.inf, F32)
        lb_ref[...] = jnp.zeros(lb_ref.shape, F32)

    means, mb, lb = means_ref[...], mb_ref[...], lb_ref[...]
    lane_q = lax.broadcasted_iota(jnp.int32, (n_rows, V7X_LANES), 1)
    lane_d = lax.broadcasted_iota(jnp.int32, (HEADS_W, V7X_LANES), 1)
    kcol = lax.broadcasted_iota(jnp.int32, (1, blk), 1).astype(F32)
    slope = _per_head_rows(ALIBI_SLOPES, t_new)

    k32 = jnp.concatenate([_all_heads(kp[r]) for r in range(pages)], axis=1)
    scores = _dot(qbd_ref[...], k32.astype(BF16))
    probs = []
    for bb in range(per_step):
        b = s * per_step + bb
        cols = slice(bb * blk, (bb + 1) * blk)
        means = jnp.where(lane_d == b, jnp.sum(k32[:, cols], axis=1, keepdims=True) * (1.0 / blk), means)
        sc = scores[:, cols] + slope * (kcol + (b * blk - p_len).astype(F32))
        mx = jnp.max(sc, axis=1, keepdims=True)
        p = jnp.exp(sc - mx)
        probs.append(p.astype(BF16))
        mb = jnp.where(lane_q == b, mx, mb)
        lb = jnp.where(lane_q == b, jnp.sum(p, axis=1, keepdims=True), lb)
    ppb = blk // PAGE
    for bb in range(per_step):
        vblk = jnp.concatenate([_all_heads(vp[bb * ppb + r]) for r in range(ppb)], axis=1).astype(BF16)
        accb_ref[s * per_step + bb] = _dot_nt(probs[bb], vblk)

    means_ref[...] = means
    mb_ref[...] = mb
    lb_ref[...] = lb

    @pl.when(s == n_steps - 1)
    def _select_and_combine():
        lane_f = lane_q.astype(F32)
        qi = lax.broadcasted_iota(jnp.int32, (n_rows, V7X_LANES), 0) & (t_new - 1)
        gate = jnp.where(lane_q < nb, _dot_exact(q32_ref[...], means), -jnp.inf)
        sel = jnp.zeros(gate.shape, F32)
        for _ in range(MOBA_TOPK):
            best = jnp.max(gate, axis=1, keepdims=True)
            first = jnp.min(jnp.where(gate == best, lane_f, float(V7X_LANES)), axis=1, keepdims=True)
            hit = lane_f == first
            sel = jnp.where(hit, jnp.where(best > -jnp.inf, 1.0, sel), sel)
            gate = jnp.where(hit, -jnp.inf, gate)
        sc = _dot(qbd_ref[...], _own_tokens(knt_ref, t_new).astype(BF16)) + slope * lane_f
        sc = jnp.where(lane_q <= qi, sc, -jnp.inf)
        m_own = jnp.max(sc, axis=1, keepdims=True)
        p_own = jnp.exp(sc - m_own)
        l_own = jnp.sum(p_own, axis=1, keepdims=True)
        acc_own = _dot_nt(p_own.astype(BF16), _own_tokens(vnt_ref, t_new).astype(BF16))
        picked = sel > 0.0
        m_all = jnp.maximum(m_own, jnp.max(jnp.where(picked, mb, -jnp.inf), axis=1, keepdims=True))
        w = jnp.where(picked, jnp.exp(mb - m_all), 0.0)
        w_own = jnp.exp(m_own - m_all)
        den = jnp.sum(w * lb, axis=1, keepdims=True) + w_own * l_own
        num = w_own * acc_own
        for b in range(nb):
            num = num + jnp.sum(jnp.where(lane_q == b, w, 0.0), axis=1, keepdims=True) * accb_ref[b]
        o_ref[0] = _own_head_columns(num / den, t_new).astype(o_ref.dtype)


def _sample_attention(kind, layer, g, t_new, page_table, qt, knt, vnt, cache_kt, cache_vt, lfn=None, cache_lft=None):
    n_pages = page_table.shape[1]
    pages = min(PAGES_PER_STEP, n_pages)
    n_steps = n_pages // pages
    n_rows = N_HEADS * t_new
    n_tok = g * t_new
    assert t_new & (t_new - 1) == 0 and pages * PAGE % MOBA_BLOCK == 0 and n_tok % V7X_LANES == 0
    per_seq = lambda blk: pl.BlockSpec(blk, lambda b, s, pt: (b,) + (0,) * (len(blk) - 1))
    in_hbm = pl.BlockSpec(memory_space=pl.ANY)
    new_kv = pl.BlockSpec((None, None, N_HEADS, HEAD_DIM, n_tok), lambda b, s, pt: (layer, 0, 0, 0, 0))
    in_specs = [pl.BlockSpec((None, N_HEADS, HEAD_DIM, n_tok), lambda b, s, pt: (0, 0, 0, 0)), new_kv, new_kv]
    args = [qt, knt, vnt]
    if kind == "fox":
        in_specs.append(per_seq((1, V7X_SUBLANES, PAGE)))
        args.append(lfn)
    in_specs += [in_hbm, in_hbm]
    args += [cache_kt, cache_vt]
    stat = pltpu.VMEM((n_rows, V7X_LANES), F32)
    q_bd = pltpu.VMEM((n_rows, HEADS_W), BF16)
    page_buf = pltpu.VMEM((2, pages, N_HEADS, HEAD_DIM, PAGE), F32)
    if kind == "fox":
        in_specs.append(in_hbm)
        args.append(cache_lft)
        body = functools.partial(_fox_sample_kernel, layer=layer, pages=pages, n_steps=n_steps, t_new=t_new)
        scratch = [q_bd, stat, stat, pltpu.VMEM((n_rows, HEADS_W), F32),
                   pltpu.VMEM((pages * V7X_SUBLANES, V7X_LANES), F32),
                   page_buf, page_buf, pltpu.VMEM((2, pages, V7X_SUBLANES, PAGE), F32),
                   pltpu.SemaphoreType.DMA((2, 3))]
    else:
        p_len = n_pages * PAGE
        assert p_len // MOBA_BLOCK <= V7X_LANES
        body = functools.partial(_moba_sample_kernel, layer=layer, pages=pages, n_steps=n_steps, p_len=p_len,
                                 t_new=t_new)
        scratch = [q_bd, pltpu.VMEM((n_rows, HEADS_W), F32), pltpu.VMEM((HEADS_W, V7X_LANES), F32), stat, stat,
                   pltpu.VMEM((p_len // MOBA_BLOCK, n_rows, HEADS_W), F32),
                   page_buf, page_buf, pltpu.SemaphoreType.DMA((2, 2))]
    return pl.pallas_call(
        body,
        out_shape=jax.ShapeDtypeStruct((g, t_new, HEADS_W), BF16),
        grid_spec=pltpu.PrefetchScalarGridSpec(
            num_scalar_prefetch=1,
            grid=(g, n_steps),
            in_specs=in_specs,
            out_specs=pl.BlockSpec((1, t_new, HEADS_W), lambda b, s, pt: (b, 0, 0)),
            scratch_shapes=scratch),
        compiler_params=_params("arbitrary", "arbitrary"),
        name=kind + "_sample",
    )(page_table, *args)


def _causal_shifts(u, fill, period):
    out = []
    for j in (1, 2):
        r = pltpu.roll(u, j, axis=0)
        f = pltpu.roll(fill, j, axis=0)
        if period is None:
            row = lax.broadcasted_iota(jnp.int32, f.shape, 0)
            out.append(jnp.concatenate([jnp.where(row >= j, r[:V7X_SUBLANES], f), r[V7X_SUBLANES:]], axis=0))
        else:
            t = lax.broadcasted_iota(jnp.int32, u.shape, 0) & (period - 1)
            out.append(jnp.where(t >= j, r, f))
    return out


def _conv3(u, fill, period, w_ref, cols):
    s1, s2 = _causal_shifts(u, fill, period)
    return w_ref[0:1, cols] * s2 + w_ref[1:2, cols] * s1 + w_ref[2:3, cols] * u


def _outproj_kernel(x_ref, pc_ref, yf_ref, ym_ref, gt_ref, cw_ref, wo_ref, g_ref, b_ref, *rest,
                    period, dn_alpha):
    if period is None:
        o_ref, us_ref, carry_ref = rest

        @pl.when(pl.program_id(1) == 0)
        def _fresh_sequence():
            carry_ref[...] = jnp.zeros(carry_ref.shape, F32)

        fill = carry_ref[...]
    else:
        fill_ref, o_ref, us_ref = rest
        fill = fill_ref[0]
    pc = pc_ref[0]
    u = pc[:, CONV_W:2 * CONV_W] * pc[:, 2 * CONV_W:]
    y_conv = pc[:, :CONV_W] * _conv3(u, fill, period, cw_ref, slice(None))
    mixed = _dot(jnp.concatenate([y_conv.astype(BF16), yf_ref[0], ym_ref[0]], axis=1), wo_ref[...])
    z = dn_alpha * x_ref[0] + (1.0 + gt_ref[0]) * mixed
    o_ref[0] = _layer_norm(z, g_ref[...], b_ref[...])
    if period is None:
        tail = u[u.shape[0] - V7X_SUBLANES:]
        carry_ref[...] = tail
        us_ref[0] = tail
    else:
        us_ref[0] = u


def _outproj(x3, pconv, yf, ym, gt, conv_w, wo, ln_g, ln_b, fill, tm, dn_alpha):
    gb, t, d = x3.shape
    period = None if fill is None else V7X_SUBLANES
    tok = lambda w: pl.BlockSpec((1, tm, w), lambda g, i: (g, i, 0))
    in_specs = [tok(d), tok(3 * CONV_W), tok(HEADS_W), tok(HEADS_W), _mod_spec(gt, tm),
                _const_spec(conv_w.shape), _const_spec(wo.shape), _const_spec(ln_g.shape), _const_spec(ln_b.shape)]
    args = [x3, pconv, yf, ym, gt, conv_w, wo, ln_g, ln_b]
    scratch = []
    if period is None:
        us_rows = V7X_SUBLANES
        us_spec = pl.BlockSpec((1, us_rows, CONV_W), lambda g, i: (g, 0, 0))
        scratch.append(pltpu.VMEM((V7X_SUBLANES, CONV_W), F32))
    else:
        us_rows = t
        us_spec = tok(CONV_W)
        in_specs.append(tok(CONV_W))
        args.append(fill)
    return pl.pallas_call(
        functools.partial(_outproj_kernel, period=period, dn_alpha=dn_alpha),
        out_shape=(jax.ShapeDtypeStruct((gb, t, d), F32), jax.ShapeDtypeStruct((gb, us_rows, CONV_W), F32)),
        grid=(gb, t // tm),
        in_specs=in_specs,
        out_specs=(tok(d), us_spec),
        scratch_shapes=scratch,
        compiler_params=_params("arbitrary", "arbitrary"),
        name="outproj",
    )(*args)


def _ffn_kernel(x_ref, sc_ref, sh_ref, gt_ref, wup_ref, cw_ref, cb_ref, wdn_ref, g_ref, b_ref, *rest,
                period, dn_alpha):
    if period is None:
        o_ref, us_ref, carry_ref = rest

        @pl.when(pl.program_id(1) == 0)
        def _fresh_sequence():
            carry_ref[...] = jnp.zeros(carry_ref.shape, F32)
    else:
        fill_ref, o_ref, us_ref = rest
    x = x_ref[0]
    tm = x.shape[0]
    d_ff = wdn_ref.shape[0]
    h = (x * (1.0 + sc_ref[0]) + sh_ref[0]).astype(BF16)
    halves = []
    for half in range(2):
        cols = slice(half * d_ff, (half + 1) * d_ff)
        u = _dot(h, wup_ref[:, cols])
        fill = carry_ref[:, cols] if period is None else fill_ref[0, :, cols]
        halves.append(_conv3(u, fill, period, cw_ref, cols) + cb_ref[:, cols])
        if period is None:
            tail = u[tm - V7X_SUBLANES:]
            carry_ref[:, cols] = tail
            us_ref[0, :, cols] = tail
        else:
            us_ref[0, :, cols] = u
    gate, val = halves
    y = _dot((gate * _sigmoid(gate) * val).astype(BF16), wdn_ref[...])
    z = dn_alpha * x + (1.0 + gt_ref[0]) * y
    o_ref[0] = _layer_norm(z, g_ref[...], b_ref[...])


def _ffn(x3, sc, sh, gt, wup, conv_w, conv_b, wdn, ln_g, ln_b, fill, tm, dn_alpha):
    gb, t, d = x3.shape
    n_up = wup.shape[1]
    period = None if fill is None else V7X_SUBLANES
    tok = lambda w: pl.BlockSpec((1, tm, w), lambda g, i: (g, i, 0))
    in_specs = [tok(d), _mod_spec(sc, tm), _mod_spec(sh, tm), _mod_spec(gt, tm),
                _const_spec(wup.shape), _const_spec(conv_w.shape), _const_spec(conv_b.shape),
                _const_spec(wdn.shape), _const_spec(ln_g.shape), _const_spec(ln_b.shape)]
    args = [x3, sc, sh, gt, wup, conv_w, conv_b, wdn, ln_g, ln_b]
    scratch = []
    if period is None:
        us_rows = V7X_SUBLANES
        us_spec = pl.BlockSpec((1, us_rows, n_up), lambda g, i: (g, 0, 0))
        scratch.append(pltpu.VMEM((V7X_SUBLANES, n_up), F32))
    else:
        us_rows = t
        us_spec = tok(n_up)
        in_specs.append(tok(n_up))
        args.append(fill)
    return pl.pallas_call(
        functools.partial(_ffn_kernel, period=period, dn_alpha=dn_alpha),
        out_shape=(jax.ShapeDtypeStruct((gb, t, d), F32), jax.ShapeDtypeStruct((gb, us_rows, n_up), F32)),
        grid=(gb, t // tm),
        in_specs=in_specs,
        out_specs=(tok(d), us_spec),
        scratch_shapes=scratch,
        compiler_params=_params("arbitrary", "arbitrary"),
        name="ffn",
    )(*args)


def _seq_fill(state):
    g, w, c = state.shape
    padded = jnp.pad(state, ((0, 0), (V7X_SUBLANES - w, 0), (0, 0)))
    return jnp.roll(padded, -1, axis=0).reshape(1, g * V7X_SUBLANES, c)


def _layer(l, depth, x3, mods, weights, kv_states, dn_alpha, tm, sample=None):
    sh_a, sc_a, gt_a, sh_f, sc_f, gt_f = mods
    (wtok, wqkvt, bf, conv_w, wo, ln1_g, ln1_b, wup, ffn_w, ffn_b, wdn, ln2_g, ln2_b) = weights
    pconv, lf, fqt, mqt, *kv_states = _inproj(l, depth, x3, sc_a, sh_a, wtok, wqkvt, bf, kv_states, tm)
    fkt, fvt, mkt, mvt = kv_states
    if sample is None:
        y_fox = _prompt_attention("fox", l, fqt, fkt, fvt, lf)
        y_moba = _prompt_attention("moba", l, mqt, mkt, mvt)
        conv_fill = ffn_fill = None
    else:
        g, t_new = sample["g"], sample["t"]
        lfn = lf[0].reshape(g, t_new, V7X_LANES)[:, :, :V7X_SUBLANES].transpose(0, 2, 1)
        lfn = jnp.pad(lfn, ((0, 0), (0, 0), (0, PAGE - t_new)))
        y_fox = _sample_attention("fox", l, g, t_new, sample["page_table"], fqt, fkt, fvt,
                                  sample["fox_kt"], sample["fox_vt"], lfn, sample["fox_lft"])
        y_moba = _sample_attention("moba", l, g, t_new, sample["page_table"], mqt, mkt, mvt,
                                   sample["moba_kt"], sample["moba_vt"])
        y_fox = y_fox.reshape(1, g * t_new, HEADS_W)
        y_moba = y_moba.reshape(1, g * t_new, HEADS_W)
        conv_fill = _seq_fill(sample["state_conv"][l])
        ffn_fill = _seq_fill(sample["state_ffn"][l])
    x1, conv_tail = _outproj(x3, pconv, y_fox, y_moba, gt_a, conv_w, wo, ln1_g, ln1_b, conv_fill, tm, dn_alpha)
    x2, ffn_tail = _ffn(x1, sc_f, sh_f, gt_f, wup, ffn_w, ffn_b, wdn, ln2_g, ln2_b, ffn_fill, tm, dn_alpha)
    return x2, kv_states, (lf, conv_tail, ffn_tail)


def kernel(x_prompt, x_sample, cache_fox_k, cache_fox_v, cache_fox_logf, cache_moba_k, cache_moba_v, state_conv, state_ffn, page_table, c_prompt, c_sample, w_ada, b_ada, w_in, b_f, conv_w, w_o, ln1_g, ln1_b, w_up, ffn_conv_w, ffn_conv_b, w_down, ln2_g, ln2_b):
    depth, d, _ = w_ada.shape
    gp, tp, _ = x_prompt.shape
    gs, ts, _ = x_sample.shape
    dn_alpha = (2 * depth) ** 0.25
    n_conv, n_qkv = 3 * CONV_W, 3 * HEADS_W

    ada = _ada(jnp.concatenate([c_prompt, c_sample], axis=0), w_ada, b_ada)
    to_kt = lambda c: jnp.transpose(c, (0, 1, 3, 4, 2))
    lft_cache = jnp.pad(jnp.transpose(cache_fox_logf, (0, 1, 3, 2)),
                        ((0, 0), (0, 0), (0, V7X_SUBLANES - N_HEADS), (0, 0)))
    sample = dict(g=gs, t=ts, page_table=page_table, state_conv=state_conv, state_ffn=state_ffn,
                  fox_kt=to_kt(cache_fox_k), fox_vt=to_kt(cache_fox_v), fox_lft=lft_cache,
                  moba_kt=to_kt(cache_moba_k), moba_vt=to_kt(cache_moba_v))

    xp = x_prompt
    xs = x_sample.reshape(1, gs * ts, d)
    kv_p = kv_s = None
    small_p, small_s = [], []
    for l in range(depth):
        wt = jnp.transpose(w_in, (2, 0, 1))[:, l]
        fox0, ff0, moba0 = n_conv, n_conv + n_qkv, n_conv + n_qkv + N_HEADS
        wtok = jnp.concatenate([wt[:n_conv], jnp.pad(wt[ff0:moba0], ((0, V7X_LANES - N_HEADS), (0, 0)))],
                               axis=0).astype(BF16)
        wqkvt = jnp.concatenate([wt[fox0:ff0], wt[moba0:]], axis=0).astype(BF16)
        bf = jnp.pad(b_f[l], (0, V7X_LANES - N_HEADS))[None]
        weights = (wtok, wqkvt, bf, conv_w[l], w_o[l].astype(BF16), ln1_g[l][None], ln1_b[l][None],
                   w_up[l].astype(BF16), ffn_conv_w[l], ffn_conv_b[l][None], w_down[l].astype(BF16),
                   ln2_g[l][None], ln2_b[l][None])
        mods_p = tuple(a[:, None, :] for a in jnp.split(ada[l, :gp], 6, axis=-1))
        mods_s = tuple(jnp.repeat(a, ts, axis=0)[None] for a in jnp.split(ada[l, gp:], 6, axis=-1))
        xp, kv_p, st_p = _layer(l, depth, xp, mods_p, weights, kv_p, dn_alpha, tm=512)
        xs, kv_s, st_s = _layer(l, depth, xs, mods_s, weights, kv_s, dn_alpha, tm=gs * ts, sample=sample)
        small_p.append(st_p)
        small_s.append(st_s)

    stack = lambda states, i: jnp.stack([s[i] for s in states])
    out = [xp, xs.reshape(gs, ts, d)]
    fk, fv, mk, mv = (jnp.transpose(a, (0, 1, 4, 2, 3)) for a in kv_p)
    out += [fk, fv, stack(small_p, 0)[..., :N_HEADS], mk, mv,
            stack(small_p, 1)[:, :, V7X_SUBLANES - 2:], stack(small_p, 2)[:, :, V7X_SUBLANES - 2:]]
    fk, fv, mk, mv = (jnp.transpose(a[:, 0].reshape(depth, N_HEADS, HEAD_DIM, gs, ts), (0, 3, 4, 1, 2))
                      for a in kv_s)
    tail_s = lambda i: stack(small_s, i).reshape(depth, gs, ts, -1)[:, :, ts - 2:]
    out += [fk, fv, stack(small_s, 0).reshape(depth, gs, ts, V7X_LANES)[..., :N_HEADS], mk, mv,
            tail_s(1), tail_s(2)]
    return tuple(out)
```

```python
import functools

import jax
import jax.numpy as jnp
from jax import lax
from jax.experimental import pallas as pl
from jax.experimental.pallas import tpu as pltpu

F32 = jnp.float32
BF16 = jnp.bfloat16
HIGHEST = lax.Precision.HIGHEST

HEAD_DIM = 64
N_HEADS = 6
HEADS_W = N_HEADS * HEAD_DIM
CONV_W = 256
MOBA_BLOCK = 256
MOBA_TOPK = 3
PAGE = 128
LN_EPS = 1e-5
ATTN_SCALE = HEAD_DIM ** -0.5
LOG2E = 1.4426950408889634
ALIBI_SLOPES = tuple(2.0 ** (-8.0 * (h + 1) / N_HEADS) for h in range(N_HEADS))
MASKED = -1e30

V7X_LANES = 128
V7X_SUBLANES = 8
V7X_VMEM_BYTES = 64 * 1024 * 1024
VMEM_LIMIT = 52 * 1024 * 1024
PAGES_PER_STEP = 32
AUG_KEY_TERM = HEAD_DIM
AUG_QUERY_TERM = HEAD_DIM + 3
AUG_END = HEAD_DIM + 6


def _dot(a, b):
    return jnp.dot(a, b, preferred_element_type=F32)


def _dot_nt(a, b):
    return lax.dot_general(a, b, (((1,), (1,)), ((), ())), preferred_element_type=F32)


def _dot_exact(a, b):
    return jnp.dot(a, b, precision=HIGHEST, preferred_element_type=F32)


def _sigmoid(x):
    return 1.0 / (1.0 + jnp.exp(-x))


def _log_sigmoid(x):
    return jnp.minimum(x, 0.0) - jnp.log(1.0 + jnp.exp(-jnp.abs(x)))


def _layer_norm(z, g, b):
    mu = jnp.mean(z, axis=-1, keepdims=True)
    zc = z - mu
    var = jnp.mean(zc * zc, axis=-1, keepdims=True)
    return zc * lax.rsqrt(var + LN_EPS) * g + b


def _split3(x):
    hi = x.astype(BF16).astype(F32)
    rest = x - hi
    mid = rest.astype(BF16).astype(F32)
    return hi, mid, rest - mid


def _params(*semantics):
    return pltpu.CompilerParams(dimension_semantics=semantics, vmem_limit_bytes=VMEM_LIMIT)


def _const_spec(shape):
    zeros = (0,) * len(shape)
    return pl.BlockSpec(shape, lambda *_: zeros)


def _mod_spec(arr, tm):
    d = arr.shape[-1]
    if arr.shape[1] == 1:
        return pl.BlockSpec((1, 1, d), lambda g, i: (g, 0, 0))
    return pl.BlockSpec((1, tm, d), lambda g, i: (g, i, 0))


def _ada_kernel(c_ref, w_ref, b_ref, o_ref):
    c = c_ref[...]
    o_ref[0] = _dot_exact(c * _sigmoid(c), w_ref[0]) + b_ref[0]


def _ada(c_all, w_ada, b_ada):
    depth, d, n = w_ada.shape
    g = c_all.shape[0]
    tn = 1536
    return pl.pallas_call(
        _ada_kernel,
        out_shape=jax.ShapeDtypeStruct((depth, g, n), F32),
        grid=(depth, n // tn),
        in_specs=[pl.BlockSpec((g, d), lambda l, j: (0, 0)),
                  pl.BlockSpec((1, d, tn), lambda l, j: (l, 0, j)),
                  pl.BlockSpec((1, 1, tn), lambda l, j: (l, 0, j))],
        out_specs=pl.BlockSpec((1, g, tn), lambda l, j: (l, 0, j)),
        compiler_params=_params("arbitrary", "arbitrary"),
        name="ada",
    )(c_all, w_ada, b_ada.reshape(depth, 1, n))


N_KV_STATES = 4


def _inproj_kernel(x_ref, sc_ref, sh_ref, wtok_ref, wqkvt_ref, bf_ref, *rest):
    conv_ref, lf_ref, fqt_ref, mqt_ref, fkt_ref, fvt_ref, mkt_ref, mvt_ref = rest[len(rest) - 8:]
    h = (x_ref[0] * (1.0 + sc_ref[0]) + sh_ref[0]).astype(BF16)
    conv_ref[0] = _dot_nt(h, wtok_ref[0:3 * CONV_W, :])
    lf_ref[0] = _log_sigmoid(_dot_nt(h, wtok_ref[3 * CONV_W:, :]) + bf_ref[...])
    for c, ref in enumerate((fqt_ref, fkt_ref, fvt_ref, mqt_ref, mkt_ref, mvt_ref)):
        t = _dot_nt(wqkvt_ref[c * HEADS_W:(c + 1) * HEADS_W, :], h)
        ref[0] = t.reshape(N_HEADS, HEAD_DIM, t.shape[-1])


def _inproj(layer, depth, x3, sc, sh, wtok, wqkvt, bf, kv_states, tm):
    gb, t, d = x3.shape
    tok = lambda w: pl.BlockSpec((1, tm, w), lambda g, i: (g, i, 0))
    qt = pl.BlockSpec((1, N_HEADS, HEAD_DIM, tm), lambda g, i: (g, 0, 0, i))
    kvt = pl.BlockSpec((None, 1, N_HEADS, HEAD_DIM, tm), lambda g, i: (layer, g, 0, 0, i))
    qt_shape = jax.ShapeDtypeStruct((gb, N_HEADS, HEAD_DIM, t), F32)
    kvt_shape = jax.ShapeDtypeStruct((depth, gb, N_HEADS, HEAD_DIM, t), F32)
    in_specs = [tok(d), _mod_spec(sc, tm), _mod_spec(sh, tm),
                _const_spec(wtok.shape), _const_spec(wqkvt.shape), _const_spec(bf.shape)]
    args = [x3, sc, sh, wtok, wqkvt, bf]
    aliases = {}
    if kv_states is not None:
        aliases = {len(args) + k: 4 + k for k in range(N_KV_STATES)}
        in_specs += [pl.BlockSpec(memory_space=pl.ANY)] * N_KV_STATES
        args += list(kv_states)
    return pl.pallas_call(
        _inproj_kernel,
        out_shape=(jax.ShapeDtypeStruct((gb, t, 3 * CONV_W), F32),
                   jax.ShapeDtypeStruct((gb, t, V7X_LANES), F32),
                   qt_shape, qt_shape, kvt_shape, kvt_shape, kvt_shape, kvt_shape),
        grid=(gb, t // tm),
        in_specs=in_specs,
        out_specs=(tok(3 * CONV_W), tok(V7X_LANES), qt, qt, kvt, kvt, kvt, kvt),
        input_output_aliases=aliases,
        compiler_params=_params("arbitrary", "arbitrary"),
        name="inproj",
    )(*args)


def _key_term_movers():
    row = lax.broadcasted_iota(jnp.int32, (V7X_LANES, V7X_LANES), 0)
    col = lax.broadcasted_iota(jnp.int32, (V7X_LANES, V7X_LANES), 1)
    movers = []
    for h in range(N_HEADS):
        m = jnp.zeros((V7X_LANES, V7X_LANES), F32)
        for k in range(3):
            m = jnp.where(row == h + V7X_SUBLANES * k, jnp.where(col == AUG_KEY_TERM + k, 1.0, m), m)
        movers.append(m.astype(BF16))
    return movers


def _pack_key_terms(term, lane):
    hi, mid, lo = _split3(jnp.where(lane < N_HEADS, term, 0.0))
    return (hi + pltpu.roll(mid, V7X_SUBLANES, axis=1) + pltpu.roll(lo, 2 * V7X_SUBLANES, axis=1)).astype(BF16)


def _augment_keys(k_tok, packed_terms, mover, lane):
    x = jnp.where(lane < HEAD_DIM, jnp.concatenate([k_tok, jnp.zeros_like(k_tok)], axis=1),
                  jnp.where(lane < AUG_QUERY_TERM, _dot(packed_terms, mover), jnp.where(lane < AUG_END, 1.0, 0.0)))
    return x.astype(BF16)


def _augment_queries(qt, query_term):
    row = lax.broadcasted_iota(jnp.int32, qt.shape, 0)
    tail = jnp.where(row < AUG_QUERY_TERM - HEAD_DIM, 1.0, 0.0)
    if query_term is not None:
        for k, piece in enumerate(_split3(query_term)):
            tail = jnp.where(row == AUG_QUERY_TERM - HEAD_DIM + k, piece, tail)
    return jnp.concatenate([qt, tail], axis=0).astype(BF16)


def _flash_heads(i, tq, ka_ref, vb_ref, qa_ref, s_ref, p_ref, acc_ref, past_bias):
    key = lax.broadcasted_iota(jnp.int32, (tq, tq), 0)
    qry = lax.broadcasted_iota(jnp.int32, (tq, tq), 1)
    causal = key <= qry

    def step(tiles, carry):
        for t, (j, diag) in enumerate(tiles):
            for h in range(N_HEADS):
                s = _dot(ka_ref[j, h], qa_ref[h])
                if diag:
                    s = jnp.where(causal, s, -jnp.inf)
                elif past_bias is not None:
                    s = s + past_bias(h, j)
                s_ref[t, h] = s
        out, alphas = [], []
        for h in range(N_HEADS):
            m, l = carry[h]
            scores = [s_ref[t, h] for t in range(len(tiles))]
            m_new = m
            for s in scores:
                m_new = jnp.maximum(m_new, jnp.max(s, axis=0, keepdims=True))
            alpha = jnp.exp2(m - m_new)
            l = alpha * l
            for t, s in enumerate(scores):
                p = jnp.exp2(s - m_new)
                p_ref[t, h] = p.astype(BF16)
                l = l + jnp.sum(p, axis=0, keepdims=True)
            out.append((m_new, l))
            alphas.append(alpha)
        for h in range(N_HEADS):
            acc = alphas[h] * acc_ref[h]
            for t, (j, _) in enumerate(tiles):
                acc = acc + _dot(vb_ref[j, h], p_ref[t, h])
            acc_ref[h] = acc
        return tuple(out)

    acc_ref[...] = jnp.zeros(acc_ref.shape, F32)
    init = tuple((jnp.full((1, tq), -jnp.inf, F32), jnp.zeros((1, tq), F32)) for _ in range(N_HEADS))
    carry = lax.fori_loop(0, i // 2, lambda jj, c: step([(2 * jj, False), (2 * jj + 1, False)], c), init)
    carry = lax.cond(i % 2 == 1, lambda c: step([(i - 1, False), (i, True)], c),
                     lambda c: step([(i, True)], c), carry)
    return [(acc_ref[h] / carry[h][1]).T for h in range(N_HEADS)]


def _fox_prompt_kernel(qt_ref, kt_ref, vt_ref, lf_ref, o_ref, ka_ref, vb_ref, *work, tq, nkt):
    i = pl.program_id(1)

    @pl.when(i == 0)
    def _prepare_sequence():
        rows = lax.broadcasted_iota(jnp.int32, (tq, tq), 0)
        cols = lax.broadcasted_iota(jnp.int32, (tq, tq), 1)
        lower = (cols <= rows).astype(F32)
        lane = lax.broadcasted_iota(jnp.int32, (tq, V7X_LANES), 1)
        movers = _key_term_movers()
        carry = jnp.zeros((1, V7X_LANES), F32)
        terms = []
        for c in range(nkt):
            sl = slice(c * tq, (c + 1) * tq)
            cum = _dot_exact(lower, lf_ref[0, sl, :]) + carry
            carry = cum[tq - 1:tq, :]
            vb_ref[c] = vt_ref[0, :, :, sl].astype(BF16)
            terms.append(_pack_key_terms(-LOG2E * cum, lane))
        for h in range(N_HEADS):
            for c in range(nkt):
                ka_ref[c, h] = _augment_keys(kt_ref[0, h, :, c * tq:(c + 1) * tq].T, terms[c], movers[h], lane)

    qt = qt_ref[0] * (ATTN_SCALE * LOG2E)
    qa_ref = work[0]
    for h in range(N_HEADS):
        qa_ref[h] = _augment_queries(qt[h], None)
    heads = _flash_heads(i, tq, ka_ref, vb_ref, *work, None)
    o_ref[0] = jnp.concatenate(heads, axis=1).astype(o_ref.dtype)


def _moba_prompt_kernel(qt_ref, kt_ref, vt_ref, o_ref, ka_ref, vb_ref, means_ref, *work, nb):
    blk = MOBA_BLOCK
    i = pl.program_id(1)

    @pl.when(i == 0)
    def _prepare_sequence():
        lane = lax.broadcasted_iota(jnp.int32, (blk, V7X_LANES), 1)
        pos = lax.broadcasted_iota(jnp.int32, (blk, V7X_LANES), 0)
        brow = lax.broadcasted_iota(jnp.int32, (V7X_SUBLANES, HEAD_DIM), 0)
        means = [jnp.zeros((V7X_SUBLANES, HEAD_DIM), F32) for _ in range(N_HEADS)]
        movers = _key_term_movers()
        slopes = jnp.zeros((1, V7X_LANES), F32)
        for h in range(N_HEADS):
            slopes = jnp.where(lane[0:1] == h, LOG2E * ALIBI_SLOPES[h], slopes)
        terms = [_pack_key_terms((pos + c * blk).astype(F32) * slopes, lane) for c in range(nb)]
        for c in range(nb):
            vb_ref[c] = vt_ref[0, :, :, c * blk:(c + 1) * blk].astype(BF16)
        for h in range(N_HEADS):
            for c in range(nb):
                k_tok = kt_ref[0, h, :, c * blk:(c + 1) * blk].T
                means[h] = jnp.where(brow == c, jnp.sum(k_tok, axis=0, keepdims=True) * (1.0 / blk), means[h])
                ka_ref[c, h] = _augment_keys(k_tok, terms[c], movers[h], lane)
        for h in range(N_HEADS):
            means_ref[h] = means[h]

    q_raw = qt_ref[0]
    brow = lax.broadcasted_iota(jnp.int32, (V7X_SUBLANES, blk), 0)
    brow_f = brow.astype(F32)
    qa_ref = work[0]
    biases = []
    for h in range(N_HEADS):
        gate = jnp.where(brow < i, _dot_exact(means_ref[h], q_raw[h]), -jnp.inf)
        bias = jnp.full(gate.shape, MASKED, F32)
        for _ in range(MOBA_TOPK):
            best = jnp.max(gate, axis=0, keepdims=True)
            first = jnp.min(jnp.where(gate == best, brow_f, float(V7X_SUBLANES)), axis=0, keepdims=True)
            hit = brow_f == first
            bias = jnp.where(hit, jnp.where(best > -jnp.inf, 0.0, bias), bias)
            gate = jnp.where(hit, -jnp.inf, gate)
        biases.append(bias)
        tile_term = jnp.full((1, blk), -LOG2E * ALIBI_SLOPES[h], F32) * (i * blk).astype(F32)
        qa_ref[h] = _augment_queries(q_raw[h] * (ATTN_SCALE * LOG2E), tile_term)

    def past_bias(h, j):
        return jnp.sum(jnp.where(brow == j, biases[h], 0.0), axis=0, keepdims=True)

    heads = _flash_heads(i, blk, ka_ref, vb_ref, *work, past_bias)
    o_ref[0] = jnp.concatenate(heads, axis=1).astype(o_ref.dtype)


def _prompt_attention(kind, layer, qt, kt, vt, lf=None):
    g, _, _, t = qt.shape
    tq = MOBA_BLOCK
    nkt = t // tq
    assert nkt <= V7X_SUBLANES, "block gates are kept on one sublane group"
    qspec = pl.BlockSpec((1, N_HEADS, HEAD_DIM, tq), lambda b, i: (b, 0, 0, i))
    kvspec = pl.BlockSpec((None, 1, N_HEADS, HEAD_DIM, t), lambda b, i: (layer, b, 0, 0, 0))
    scratch = [pltpu.VMEM((nkt, N_HEADS, tq, V7X_LANES), BF16), pltpu.VMEM((nkt, N_HEADS, HEAD_DIM, tq), BF16)]
    work = [pltpu.VMEM((N_HEADS, V7X_LANES, tq), BF16), pltpu.VMEM((2, N_HEADS, tq, tq), F32),
            pltpu.VMEM((2, N_HEADS, tq, tq), BF16), pltpu.VMEM((N_HEADS, HEAD_DIM, tq), F32)]
    if kind == "fox":
        body = functools.partial(_fox_prompt_kernel, tq=tq, nkt=nkt)
        in_specs = [qspec, kvspec, kvspec, pl.BlockSpec((1, t, V7X_LANES), lambda b, i: (b, 0, 0))]
        args = (qt, kt, vt, lf)
    else:
        body = functools.partial(_moba_prompt_kernel, nb=nkt)
        in_specs = [qspec, kvspec, kvspec]
        args = (qt, kt, vt)
        scratch.append(pltpu.VMEM((N_HEADS, V7X_SUBLANES, HEAD_DIM), F32))
    return pl.pallas_call(
        body,
        out_shape=jax.ShapeDtypeStruct((g, t, HEADS_W), BF16),
        grid=(g, nkt),
        in_specs=in_specs,
        out_specs=pl.BlockSpec((1, tq, HEADS_W), lambda b, i: (b, i, 0)),
        scratch_shapes=scratch + work,
        compiler_params=_params("arbitrary", "arbitrary"),
        name=kind + "_prompt",
    )(*args)


def _page_copies(pt_ref, caches, bufs, sems, layer, g, s, slot, pages):
    return [pltpu.make_async_copy(cache.at[layer, pt_ref[g, s * pages + r]], buf.at[slot, r], sems.at[slot, a])
            for r in range(pages) for a, (cache, buf) in enumerate(zip(caches, bufs))]


def _stream_pages(pt_ref, caches, bufs, sems, layer, pages, n_steps):
    g, s = pl.program_id(0), pl.program_id(1)
    step = g * n_steps + s
    slot = step & 1
    last = pl.num_programs(0) * n_steps - 1

    @pl.when(step == 0)
    def _first():
        for c in _page_copies(pt_ref, caches, bufs, sems, layer, g, s, slot, pages):
            c.start()

    @pl.when(step < last)
    def _prefetch():
        wrap = s == n_steps - 1
        g_next = jnp.where(wrap, g + 1, g)
        s_next = jnp.where(wrap, 0, s + 1)
        for c in _page_copies(pt_ref, caches, bufs, sems, layer, g_next, s_next, 1 - slot, pages):
            c.start()

    for c in _page_copies(pt_ref, caches, bufs, sems, layer, g, s, slot, pages):
        c.wait()
    return slot


def _all_heads(page_ref):
    return page_ref[...].reshape(HEADS_W, page_ref.shape[-1])


def _own_tokens(xt_ref, t_new):
    x = _all_heads(xt_ref)
    n = x.shape[1]
    return pltpu.roll(x, lax.rem(n - pl.program_id(0) * t_new, n), axis=1)[:, :PAGE]


def _block_diag_queries(qt_ref, t_new, scale):
    q_tok = (_own_tokens(qt_ref, t_new) * scale).T[:t_new]
    z = jnp.zeros((t_new, HEAD_DIM), F32)
    return jnp.concatenate(
        [jnp.concatenate([q_tok[:, h * HEAD_DIM:(h + 1) * HEAD_DIM] if c == h else z for c in range(N_HEADS)], axis=1)
         for h in range(N_HEADS)], axis=0)


def _own_head_columns(x, t):
    lane = lax.broadcasted_iota(jnp.int32, (t, HEADS_W), 1)
    out = jnp.zeros((t, HEADS_W), x.dtype)
    for h in range(N_HEADS):
        own = (lane >= h * HEAD_DIM) & (lane < (h + 1) * HEAD_DIM)
        out = jnp.where(own, x[h * t:(h + 1) * t], out)
    return out


def _per_head_rows(values, t):
    row = lax.broadcasted_iota(jnp.int32, (N_HEADS * t, 1), 0)
    out = jnp.zeros((N_HEADS * t, 1), F32)
    for h, v in enumerate(values):
        out = jnp.where((row >= h * t) & (row < (h + 1) * t), v, out)
    return out


def _fox_sample_kernel(pt_ref, q_ref, knt_ref, vnt_ref, lfn_ref, ck_ref, cv_ref, cl_ref, o_ref,
                       qbd_ref, m_ref, l_ref, acc_ref, carry_ref, kbuf, vbuf, lbuf, sems, *,
                       layer, pages, n_steps, t_new):
    slot = _stream_pages(pt_ref, (ck_ref, cv_ref, cl_ref), (kbuf, vbuf, lbuf), sems, layer, pages, n_steps)
    kp, vp, lp = ([buf.at[slot, r] for r in range(pages)] for buf in (kbuf, vbuf, lbuf))
    s = pl.program_id(1)
    n_rows = N_HEADS * t_new

    @pl.when(s == 0)
    def _init():
        qbd_ref[...] = _block_diag_queries(q_ref, t_new, ATTN_SCALE).astype(BF16)
        m_ref[...] = jnp.full(m_ref.shape, -jnp.inf, F32)
        l_ref[...] = jnp.zeros(l_ref.shape, F32)
        acc_ref[...] = jnp.zeros(acc_ref.shape, F32)
        carry_ref[...] = jnp.zeros(carry_ref.shape, F32)

    m, l, acc = m_ref[:, 0:1], l_ref[:, 0:1], acc_ref[...]

    n_rows_lf = pages * V7X_SUBLANES
    lf = jnp.concatenate([lp[r][...] for r in range(pages)], axis=0)
    ri = lax.broadcasted_iota(jnp.int32, (PAGE, PAGE), 0)
    ci = lax.broadcasted_iota(jnp.int32, (PAGE, PAGE), 1)
    upper = jnp.where(ri <= ci, 1.0, 0.0).astype(BF16)
    pi = lax.broadcasted_iota(jnp.int32, (n_rows_lf, n_rows_lf), 0)
    pj = lax.broadcasted_iota(jnp.int32, (n_rows_lf, n_rows_lf), 1)
    same_head = (pi & (V7X_SUBLANES - 1)) == (pj & (V7X_SUBLANES - 1))
    earlier_page = jnp.where(same_head, jnp.where(pj < (pi & -V7X_SUBLANES), 1.0, 0.0), 0.0).astype(BF16)
    any_page = jnp.where(same_head, 1.0, 0.0).astype(BF16)
    lf_pieces = [x.astype(BF16) for x in _split3(lf)]
    totals = sum(_dot(x, jnp.ones((PAGE, PAGE), BF16)) for x in lf_pieces)
    total_pieces = [x.astype(BF16) for x in _split3(totals)]
    carry = carry_ref[...]
    cum = (sum(_dot(x, upper) for x in lf_pieces) + sum(_dot(earlier_page, x) for x in total_pieces)) + carry
    carry = carry + sum(_dot(any_page, x) for x in total_pieces)
    key_gate = jnp.concatenate(
        [jnp.concatenate([jnp.broadcast_to(cum[r * V7X_SUBLANES + h:r * V7X_SUBLANES + h + 1, :], (t_new, PAGE))
                          for r in range(pages)], axis=1) for h in range(N_HEADS)], axis=0)

    def update(scores, vt, m, l, acc):
        m_new = jnp.maximum(m, jnp.max(scores, axis=1, keepdims=True))
        alpha = jnp.exp(m - m_new)
        p = jnp.exp(scores - m_new)
        return m_new, alpha * l + jnp.sum(p, axis=1, keepdims=True), alpha * acc + _dot_nt(p.astype(BF16), vt)

    kt = jnp.concatenate([_all_heads(kp[r]) for r in range(pages)], axis=1).astype(BF16)
    vt = jnp.concatenate([_all_heads(vp[r]) for r in range(pages)], axis=1).astype(BF16)
    m, l, acc = update(_dot(qbd_ref[...], kt) - key_gate, vt, m, l, acc)

    carry_ref[...] = carry
    m_ref[...] = jnp.broadcast_to(m, m_ref.shape)
    l_ref[...] = jnp.broadcast_to(l, l_ref.shape)
    acc_ref[...] = acc

    @pl.when(s == n_steps - 1)
    def _new_tokens():
        lane = lax.broadcasted_iota(jnp.int32, (n_rows, PAGE), 1)
        qi = lax.broadcasted_iota(jnp.int32, (n_rows, PAGE), 0) & (t_new - 1)
        cum_n = sum(_dot(x.astype(BF16), upper) for x in _split3(lfn_ref[0])) + carry[:V7X_SUBLANES]
        gate_n = jnp.concatenate([jnp.broadcast_to(cum_n[h:h + 1, :], (t_new, PAGE)) for h in range(N_HEADS)], axis=0)
        scores = jnp.where(lane <= qi, _dot(qbd_ref[...], _own_tokens(knt_ref, t_new).astype(BF16)) - gate_n, -jnp.inf)
        _, l_fin, acc_fin = update(scores, _own_tokens(vnt_ref, t_new).astype(BF16), m, l, acc)
        o_ref[0] = _own_head_columns(acc_fin / l_fin, t_new).astype(o_ref.dtype)


def _moba_sample_kernel(pt_ref, q_ref, knt_ref, vnt_ref, ck_ref, cv_ref, o_ref, qbd_ref, q32_ref, means_ref,
                        mb_ref, lb_ref, accb_ref, kbuf, vbuf, sems, *, layer, pages, n_steps, p_len, t_new):
    slot = _stream_pages(pt_ref, (ck_ref, cv_ref), (kbuf, vbuf), sems, layer, pages, n_steps)
    kp, vp = ([buf.at[slot, r] for r in range(pages)] for buf in (kbuf, vbuf))
    blk = MOBA_BLOCK
    nb = p_len // blk
    per_step = pages * PAGE // blk
    s = pl.program_id(1)
    n_rows = N_HEADS * t_new

    @pl.when(s == 0)
    def _init():
        q32_ref[...] = _block_diag_queries(q_ref, t_new, 1.0)
        qbd_ref[...] = _block_diag_queries(q_ref, t_new, ATTN_SCALE).astype(BF16)
        means_ref[...] = jnp.zeros(means_ref.shape, F32)
        mb_ref[...] = jnp.full(mb_ref.shape, -jnp.inf, F32)
        lb_ref[...] = jnp.zeros(lb_ref.shape, F32)

    means, mb, lb = means_ref[...], mb_ref[...], lb_ref[...]
    lane_q = lax.broadcasted_iota(jnp.int32, (n_rows, V7X_LANES), 1)
    lane_d = lax.broadcasted_iota(jnp.int32, (HEADS_W, V7X_LANES), 1)
    kcol = lax.broadcasted_iota(jnp.int32, (1, blk), 1).astype(F32)
    slope = _per_head_rows(ALIBI_SLOPES, t_new)

    k32 = jnp.concatenate([_all_heads(kp[r]) for r in range(pages)], axis=1)
    scores = _dot(qbd_ref[...], k32.astype(BF16))
    probs = []
    for bb in range(per_step):
        b = s * per_step + bb
        cols = slice(bb * blk, (bb + 1) * blk)
        means = jnp.where(lane_d == b, jnp.sum(k32[:, cols], axis=1, keepdims=True) * (1.0 / blk), means)
        sc = scores[:, cols] + slope * (kcol + (b * blk - p_len).astype(F32))
        mx = jnp.max(sc, axis=1, keepdims=True)
        p = jnp.exp(sc - mx)
        probs.append(p.astype(BF16))
        mb = jnp.where(lane_q == b, mx, mb)
        lb = jnp.where(lane_q == b, jnp.sum(p, axis=1, keepdims=True), lb)
    ppb = blk // PAGE
    for bb in range(per_step):
        vblk = jnp.concatenate([_all_heads(vp[bb * ppb + r]) for r in range(ppb)], axis=1).astype(BF16)
        accb_ref[s * per_step + bb] = _dot_nt(probs[bb], vblk)

    means_ref[...] = means
    mb_ref[...] = mb
    lb_ref[...] = lb

    @pl.when(s == n_steps - 1)
    def _select_and_combine():
        lane_f = lane_q.astype(F32)
        qi = lax.broadcasted_iota(jnp.int32, (n_rows, V7X_LANES), 0) & (t_new - 1)
        gate = jnp.where(lane_q < nb, _dot_exact(q32_ref[...], means), -jnp.inf)
        sel = jnp.zeros(gate.shape, F32)
        for _ in range(MOBA_TOPK):
            best = jnp.max(gate, axis=1, keepdims=True)
            first = jnp.min(jnp.where(gate == best, lane_f, float(V7X_LANES)), axis=1, keepdims=True)
            hit = lane_f == first
            sel = jnp.where(hit, jnp.where(best > -jnp.inf, 1.0, sel), sel)
            gate = jnp.where(hit, -jnp.inf, gate)
        sc = _dot(qbd_ref[...], _own_tokens(knt_ref, t_new).astype(BF16)) + slope * lane_f
        sc = jnp.where(lane_q <= qi, sc, -jnp.inf)
        m_own = jnp.max(sc, axis=1, keepdims=True)
        p_own = jnp.exp(sc - m_own)
        l_own = jnp.sum(p_own, axis=1, keepdims=True)
        acc_own = _dot_nt(p_own.astype(BF16), _own_tokens(vnt_ref, t_new).astype(BF16))
        picked = sel > 0.0
        m_all = jnp.maximum(m_own, jnp.max(jnp.where(picked, mb, -jnp.inf), axis=1, keepdims=True))
        w = jnp.where(picked, jnp.exp(mb - m_all), 0.0)
        w_own = jnp.exp(m_own - m_all)
        den = jnp.sum(w * lb, axis=1, keepdims=True) + w_own * l_own
        num = w_own * acc_own
        for b in range(nb):
            num = num + jnp.sum(jnp.where(lane_q == b, w, 0.0), axis=1, keepdims=True) * accb_ref[b]
        o_ref[0] = _own_head_columns(num / den, t_new).astype(o_ref.dtype)


def _sample_attention(kind, layer, g, t_new, page_table, qt, knt, vnt, cache_kt, cache_vt, lfn=None, cache_lft=None):
    n_pages = page_table.shape[1]
    pages = min(PAGES_PER_STEP, n_pages)
    n_steps = n_pages // pages
    n_rows = N_HEADS * t_new
    n_tok = g * t_new
    assert t_new & (t_new - 1) == 0 and pages * PAGE % MOBA_BLOCK == 0 and n_tok % V7X_LANES == 0
    per_seq = lambda blk: pl.BlockSpec(blk, lambda b, s, pt: (b,) + (0,) * (len(blk) - 1))
    in_hbm = pl.BlockSpec(memory_space=pl.ANY)
    new_kv = pl.BlockSpec((None, None, N_HEADS, HEAD_DIM, n_tok), lambda b, s, pt: (layer, 0, 0, 0, 0))
    in_specs = [pl.BlockSpec((None, N_HEADS, HEAD_DIM, n_tok), lambda b, s, pt: (0, 0, 0, 0)), new_kv, new_kv]
    args = [qt, knt, vnt]
    if kind == "fox":
        in_specs.append(per_seq((1, V7X_SUBLANES, PAGE)))
        args.append(lfn)
    in_specs += [in_hbm, in_hbm]
    args += [cache_kt, cache_vt]
    stat = pltpu.VMEM((n_rows, V7X_LANES), F32)
    q_bd = pltpu.VMEM((n_rows, HEADS_W), BF16)
    page_buf = pltpu.VMEM((2, pages, N_HEADS, HEAD_DIM, PAGE), F32)
    if kind == "fox":
        in_specs.append(in_hbm)
        args.append(cache_lft)
        body = functools.partial(_fox_sample_kernel, layer=layer, pages=pages, n_steps=n_steps, t_new=t_new)
        scratch = [q_bd, stat, stat, pltpu.VMEM((n_rows, HEADS_W), F32),
                   pltpu.VMEM((pages * V7X_SUBLANES, V7X_LANES), F32),
                   page_buf, page_buf, pltpu.VMEM((2, pages, V7X_SUBLANES, PAGE), F32),
                   pltpu.SemaphoreType.DMA((2, 3))]
    else:
        p_len = n_pages * PAGE
        assert p_len // MOBA_BLOCK <= V7X_LANES
        body = functools.partial(_moba_sample_kernel, layer=layer, pages=pages, n_steps=n_steps, p_len=p_len,
                                 t_new=t_new)
        scratch = [q_bd, pltpu.VMEM((n_rows, HEADS_W), F32), pltpu.VMEM((HEADS_W, V7X_LANES), F32), stat, stat,
                   pltpu.VMEM((p_len // MOBA_BLOCK, n_rows, HEADS_W), F32),
                   page_buf, page_buf, pltpu.SemaphoreType.DMA((2, 2))]
    return pl.pallas_call(
        body,
        out_shape=jax.ShapeDtypeStruct((g, t_new, HEADS_W), BF16),
        grid_spec=pltpu.PrefetchScalarGridSpec(
            num_scalar_prefetch=1,
            grid=(g, n_steps),
            in_specs=in_specs,
            out_specs=pl.BlockSpec((1, t_new, HEADS_W), lambda b, s, pt: (b, 0, 0)),
            scratch_shapes=scratch),
        compiler_params=_params("arbitrary", "arbitrary"),
        name=kind + "_sample",
    )(page_table, *args)


def _causal_shifts(u, fill, period):
    out = []
    for j in (1, 2):
        r = pltpu.roll(u, j, axis=0)
        f = pltpu.roll(fill, j, axis=0)
        if period is None:
            row = lax.broadcasted_iota(jnp.int32, f.shape, 0)
            out.append(jnp.concatenate([jnp.where(row >= j, r[:V7X_SUBLANES], f), r[V7X_SUBLANES:]], axis=0))
        else:
            t = lax.broadcasted_iota(jnp.int32, u.shape, 0) & (period - 1)
            out.append(jnp.where(t >= j, r, f))
    return out


def _conv3(u, fill, period, w_ref, cols):
    s1, s2 = _causal_shifts(u, fill, period)
    return w_ref[0:1, cols] * s2 + w_ref[1:2, cols] * s1 + w_ref[2:3, cols] * u


def _outproj_kernel(x_ref, pc_ref, yf_ref, ym_ref, gt_ref, cw_ref, wo_ref, g_ref, b_ref, *rest,
                    period, dn_alpha):
    if period is None:
        o_ref, us_ref, carry_ref = rest

        @pl.when(pl.program_id(1) == 0)
        def _fresh_sequence():
            carry_ref[...] = jnp.zeros(carry_ref.shape, F32)

        fill = carry_ref[...]
    else:
        fill_ref, o_ref, us_ref = rest
        fill = fill_ref[0]
    pc = pc_ref[0]
    u = pc[:, CONV_W:2 * CONV_W] * pc[:, 2 * CONV_W:]
    y_conv = pc[:, :CONV_W] * _conv3(u, fill, period, cw_ref, slice(None))
    mixed = _dot(jnp.concatenate([y_conv.astype(BF16), yf_ref[0], ym_ref[0]], axis=1), wo_ref[...])
    z = dn_alpha * x_ref[0] + (1.0 + gt_ref[0]) * mixed
    o_ref[0] = _layer_norm(z, g_ref[...], b_ref[...])
    if period is None:
        tail = u[u.shape[0] - V7X_SUBLANES:]
        carry_ref[...] = tail
        us_ref[0] = tail
    else:
        us_ref[0] = u


def _outproj(x3, pconv, yf, ym, gt, conv_w, wo, ln_g, ln_b, fill, tm, dn_alpha):
    gb, t, d = x3.shape
    period = None if fill is None else V7X_SUBLANES
    tok = lambda w: pl.BlockSpec((1, tm, w), lambda g, i: (g, i, 0))
    in_specs = [tok(d), tok(3 * CONV_W), tok(HEADS_W), tok(HEADS_W), _mod_spec(gt, tm),
                _const_spec(conv_w.shape), _const_spec(wo.shape), _const_spec(ln_g.shape), _const_spec(ln_b.shape)]
    args = [x3, pconv, yf, ym, gt, conv_w, wo, ln_g, ln_b]
    scratch = []
    if period is None:
        us_rows = V7X_SUBLANES
        us_spec = pl.BlockSpec((1, us_rows, CONV_W), lambda g, i: (g, 0, 0))
        scratch.append(pltpu.VMEM((V7X_SUBLANES, CONV_W), F32))
    else:
        us_rows = t
        us_spec = tok(CONV_W)
        in_specs.append(tok(CONV_W))
        args.append(fill)
    return pl.pallas_call(
        functools.partial(_outproj_kernel, period=period, dn_alpha=dn_alpha),
        out_shape=(jax.ShapeDtypeStruct((gb, t, d), F32), jax.ShapeDtypeStruct((gb, us_rows, CONV_W), F32)),
        grid=(gb, t // tm),
        in_specs=in_specs,
        out_specs=(tok(d), us_spec),
        scratch_shapes=scratch,
        compiler_params=_params("arbitrary", "arbitrary"),
        name="outproj",
    )(*args)


def _ffn_kernel(x_ref, sc_ref, sh_ref, gt_ref, wup_ref, cw_ref, cb_ref, wdn_ref, g_ref, b_ref, *rest,
                period, dn_alpha):
    if period is None:
        o_ref, us_ref, carry_ref = rest

        @pl.when(pl.program_id(1) == 0)
        def _fresh_sequence():
            carry_ref[...] = jnp.zeros(carry_ref.shape, F32)
    else:
        fill_ref, o_ref, us_ref = rest
    x = x_ref[0]
    tm = x.shape[0]
    d_ff = wdn_ref.shape[0]
    h = (x * (1.0 + sc_ref[0]) + sh_ref[0]).astype(BF16)
    halves = []
    for half in range(2):
        cols = slice(half * d_ff, (half + 1) * d_ff)
        u = _dot(h, wup_ref[:, cols])
        fill = carry_ref[:, cols] if period is None else fill_ref[0, :, cols]
        halves.append(_conv3(u, fill, period, cw_ref, cols) + cb_ref[:, cols])
        if period is None:
            tail = u[tm - V7X_SUBLANES:]
            carry_ref[:, cols] = tail
            us_ref[0, :, cols] = tail
        else:
            us_ref[0, :, cols] = u
    gate, val = halves
    y = _dot((gate * _sigmoid(gate) * val).astype(BF16), wdn_ref[...])
    z = dn_alpha * x + (1.0 + gt_ref[0]) * y
    o_ref[0] = _layer_norm(z, g_ref[...], b_ref[...])


def _ffn(x3, sc, sh, gt, wup, conv_w, conv_b, wdn, ln_g, ln_b, fill, tm, dn_alpha):
    gb, t, d = x3.shape
    n_up = wup.shape[1]
    period = None if fill is None else V7X_SUBLANES
    tok = lambda w: pl.BlockSpec((1, tm, w), lambda g, i: (g, i, 0))
    in_specs = [tok(d), _mod_spec(sc, tm), _mod_spec(sh, tm), _mod_spec(gt, tm),
                _const_spec(wup.shape), _const_spec(conv_w.shape), _const_spec(conv_b.shape),
                _const_spec(wdn.shape), _const_spec(ln_g.shape), _const_spec(ln_b.shape)]
    args = [x3, sc, sh, gt, wup, conv_w, conv_b, wdn, ln_g, ln_b]
    scratch = []
    if period is None:
        us_rows = V7X_SUBLANES
        us_spec = pl.BlockSpec((1, us_rows, n_up), lambda g, i: (g, 0, 0))
        scratch.append(pltpu.VMEM((V7X_SUBLANES, n_up), F32))
    else:
        us_rows = t
        us_spec = tok(n_up)
        in_specs.append(tok(n_up))
        args.append(fill)
    return pl.pallas_call(
        functools.partial(_ffn_kernel, period=period, dn_alpha=dn_alpha),
        out_shape=(jax.ShapeDtypeStruct((gb, t, d), F32), jax.ShapeDtypeStruct((gb, us_rows, n_up), F32)),
        grid=(gb, t // tm),
        in_specs=in_specs,
        out_specs=(tok(d), us_spec),
        scratch_shapes=scratch,
        compiler_params=_params("arbitrary", "arbitrary"),
        name="ffn",
    )(*args)


def _seq_fill(state):
    g, w, c = state.shape
    padded = jnp.pad(state, ((0, 0), (V7X_SUBLANES - w, 0), (0, 0)))
    return jnp.roll(padded, -1, axis=0).reshape(1, g * V7X_SUBLANES, c)


def _layer(l, depth, x3, mods, weights, kv_states, dn_alpha, tm, sample=None):
    sh_a, sc_a, gt_a, sh_f, sc_f, gt_f = mods
    (wtok, wqkvt, bf, conv_w, wo, ln1_g, ln1_b, wup, ffn_w, ffn_b, wdn, ln2_g, ln2_b) = weights
    pconv, lf, fqt, mqt, *kv_states = _inproj(l, depth, x3, sc_a, sh_a, wtok, wqkvt, bf, kv_states, tm)
    fkt, fvt, mkt, mvt = kv_states
    if sample is None:
        y_fox = _prompt_attention("fox", l, fqt, fkt, fvt, lf)
        y_moba = _prompt_attention("moba", l, mqt, mkt, mvt)
        conv_fill = ffn_fill = None
    else:
        g, t_new = sample["g"], sample["t"]
        lfn = lf[0].reshape(g, t_new, V7X_LANES)[:, :, :V7X_SUBLANES].transpose(0, 2, 1)
        lfn = jnp.pad(lfn, ((0, 0), (0, 0), (0, PAGE - t_new)))
        y_fox = _sample_attention("fox", l, g, t_new, sample["page_table"], fqt, fkt, fvt,
                                  sample["fox_kt"], sample["fox_vt"], lfn, sample["fox_lft"])
        y_moba = _sample_attention("moba", l, g, t_new, sample["page_table"], mqt, mkt, mvt,
                                   sample["moba_kt"], sample["moba_vt"])
        y_fox = y_fox.reshape(1, g * t_new, HEADS_W)
        y_moba = y_moba.reshape(1, g * t_new, HEADS_W)
        conv_fill = _seq_fill(sample["state_conv"][l])
        ffn_fill = _seq_fill(sample["state_ffn"][l])
    x1, conv_tail = _outproj(x3, pconv, y_fox, y_moba, gt_a, conv_w, wo, ln1_g, ln1_b, conv_fill, tm, dn_alpha)
    x2, ffn_tail = _ffn(x1, sc_f, sh_f, gt_f, wup, ffn_w, ffn_b, wdn, ln2_g, ln2_b, ffn_fill, tm, dn_alpha)
    return x2, kv_states, (lf, conv_tail, ffn_tail)


def kernel(x_prompt, x_sample, cache_fox_k, cache_fox_v, cache_fox_logf, cache_moba_k, cache_moba_v, state_conv, state_ffn, page_table, c_prompt, c_sample, w_ada, b_ada, w_in, b_f, conv_w, w_o, ln1_g, ln1_b, w_up, ffn_conv_w, ffn_conv_b, w_down, ln2_g, ln2_b):
    depth, d, _ = w_ada.shape
    gp, tp, _ = x_prompt.shape
    gs, ts, _ = x_sample.shape
    dn_alpha = (2 * depth) ** 0.25
    n_conv, n_qkv = 3 * CONV_W, 3 * HEADS_W

    ada = _ada(jnp.concatenate([c_prompt, c_sample], axis=0), w_ada, b_ada)
    to_kt = lambda c: jnp.transpose(c, (0, 1, 3, 4, 2))
    lft_cache = jnp.pad(jnp.transpose(cache_fox_logf, (0, 1, 3, 2)),
                        ((0, 0), (0, 0), (0, V7X_SUBLANES - N_HEADS), (0, 0)))
    sample = dict(g=gs, t=ts, page_table=page_table, state_conv=state_conv, state_ffn=state_ffn,
                  fox_kt=to_kt(cache_fox_k), fox_vt=to_kt(cache_fox_v), fox_lft=lft_cache,
                  moba_kt=to_kt(cache_moba_k), moba_vt=to_kt(cache_moba_v))

    xp = x_prompt
    xs = x_sample.reshape(1, gs * ts, d)
    kv_p = kv_s = None
    small_p, small_s = [], []
    for l in range(depth):
        wt = jnp.transpose(w_in, (2, 0, 1))[:, l]
        fox0, ff0, moba0 = n_conv, n_conv + n_qkv, n_conv + n_qkv + N_HEADS
        wtok = jnp.concatenate([wt[:n_conv], jnp.pad(wt[ff0:moba0], ((0, V7X_LANES - N_HEADS), (0, 0)))],
                               axis=0).astype(BF16)
        wqkvt = jnp.concatenate([wt[fox0:ff0], wt[moba0:]], axis=0).astype(BF16)
        bf = jnp.pad(b_f[l], (0, V7X_LANES - N_HEADS))[None]
        weights = (wtok, wqkvt, bf, conv_w[l], w_o[l].astype(BF16), ln1_g[l][None], ln1_b[l][None],
                   w_up[l].astype(BF16), ffn_conv_w[l], ffn_conv_b[l][None], w_down[l].astype(BF16),
                   ln2_g[l][None], ln2_b[l][None])
        mods_p = tuple(a[:, None, :] for a in jnp.split(ada[l, :gp], 6, axis=-1))
        mods_s = tuple(jnp.repeat(a, ts, axis=0)[None] for a in jnp.split(ada[l, gp:], 6, axis=-1))
        xp, kv_p, st_p = _layer(l, depth, xp, mods_p, weights, kv_p, dn_alpha, tm=512)
        xs, kv_s, st_s = _layer(l, depth, xs, mods_s, weights, kv_s, dn_alpha, tm=gs * ts, sample=sample)
        small_p.append(st_p)
        small_s.append(st_s)

    stack = lambda states, i: jnp.stack([s[i] for s in states])
    out = [xp, xs.reshape(gs, ts, d)]
    fk, fv, mk, mv = (jnp.transpose(a, (0, 1, 4, 2, 3)) for a in kv_p)
    out += [fk, fv, stack(small_p, 0)[..., :N_HEADS], mk, mv,
            stack(small_p, 1)[:, :, V7X_SUBLANES - 2:], stack(small_p, 2)[:, :, V7X_SUBLANES - 2:]]
    fk, fv, mk, mv = (jnp.transpose(a[:, 0].reshape(depth, N_HEADS, HEAD_DIM, gs, ts), (0, 3, 4, 1, 2))
                      for a in kv_s)
    tail_s = lambda i: stack(small_s, i).reshape(depth, gs, ts, -1)[:, :, ts - 2:]
    out += [fk, fv, stack(small_s, 0).reshape(depth, gs, ts, V7X_LANES)[..., :N_HEADS], mk, mv,
            tail_s(1), tail_s(2)]
    return tuple(out)
```

```python
import functools

import jax
import jax.numpy as jnp
from jax import lax
from jax.experimental import pallas as pl
from jax.experimental.pallas import tpu as pltpu

F32 = jnp.float32
BF16 = jnp.bfloat16
HIGHEST = lax.Precision.HIGHEST

HEAD_DIM = 64
N_HEADS = 6
HEADS_W = N_HEADS * HEAD_DIM
CONV_W = 256
MOBA_BLOCK = 256
MOBA_TOPK = 3
PAGE = 128
LN_EPS = 1e-5
ATTN_SCALE = HEAD_DIM ** -0.5
LOG2E = 1.4426950408889634
ALIBI_SLOPES = tuple(2.0 ** (-8.0 * (h + 1) / N_HEADS) for h in range(N_HEADS))
MASKED = -1e30

V7X_LANES = 128
V7X_SUBLANES = 8
V7X_VMEM_BYTES = 64 * 1024 * 1024
VMEM_LIMIT = 58 * 1024 * 1024
PAGES_PER_STEP = 64
AUG_KEY_TERM = HEAD_DIM
AUG_QUERY_TERM = HEAD_DIM + 3
AUG_END = HEAD_DIM + 6


def _dot(a, b):
    return jnp.dot(a, b, preferred_element_type=F32)


def _dot_nt(a, b):
    return lax.dot_general(a, b, (((1,), (1,)), ((), ())), preferred_element_type=F32)


def _dot_exact(a, b):
    return jnp.dot(a, b, precision=HIGHEST, preferred_element_type=F32)


def _sigmoid(x):
    return 1.0 / (1.0 + jnp.exp(-x))


def _log_sigmoid(x):
    return jnp.minimum(x, 0.0) - jnp.log(1.0 + jnp.exp(-jnp.abs(x)))


def _layer_norm(z, g, b):
    mu = jnp.mean(z, axis=-1, keepdims=True)
    zc = z - mu
    var = jnp.mean(zc * zc, axis=-1, keepdims=True)
    return zc * lax.rsqrt(var + LN_EPS) * g + b


def _split3(x):
    hi = x.astype(BF16).astype(F32)
    rest = x - hi
    mid = rest.astype(BF16).astype(F32)
    return hi, mid, rest - mid


def _params(*semantics):
    return pltpu.CompilerParams(dimension_semantics=semantics, vmem_limit_bytes=VMEM_LIMIT)


def _const_spec(shape):
    zeros = (0,) * len(shape)
    return pl.BlockSpec(shape, lambda *_: zeros)


def _mod_spec(arr, tm):
    d = arr.shape[-1]
    if arr.shape[1] == 1:
        return pl.BlockSpec((1, 1, d), lambda g, i: (g, 0, 0))
    return pl.BlockSpec((1, tm, d), lambda g, i: (g, i, 0))


def _ada_kernel(c_ref, w_ref, b_ref, o_ref):
    c = c_ref[...]
    o_ref[0] = _dot_exact(c * _sigmoid(c), w_ref[0]) + b_ref[0]


def _ada(c_all, w_ada, b_ada):
    depth, d, n = w_ada.shape
    g = c_all.shape[0]
    tn = 1536
    return pl.pallas_call(
        _ada_kernel,
        out_shape=jax.ShapeDtypeStruct((depth, g, n), F32),
        grid=(depth, n // tn),
        in_specs=[pl.BlockSpec((g, d), lambda l, j: (0, 0)),
                  pl.BlockSpec((1, d, tn), lambda l, j: (l, 0, j)),
                  pl.BlockSpec((1, 1, tn), lambda l, j: (l, 0, j))],
        out_specs=pl.BlockSpec((1, g, tn), lambda l, j: (l, 0, j)),
        compiler_params=_params("arbitrary", "arbitrary"),
        name="ada",
    )(c_all, w_ada, b_ada.reshape(depth, 1, n))


N_KV_STATES = 4


def _inproj_kernel(x_ref, sc_ref, sh_ref, wtok_ref, wqkvt_ref, bf_ref, *rest):
    conv_ref, lf_ref, fqt_ref, mqt_ref, fkt_ref, fvt_ref, mkt_ref, mvt_ref = rest[len(rest) - 8:]
    h = (x_ref[0] * (1.0 + sc_ref[0]) + sh_ref[0]).astype(BF16)
    conv_ref[0] = _dot_nt(h, wtok_ref[0:3 * CONV_W, :])
    lf_ref[0] = _log_sigmoid(_dot_nt(h, wtok_ref[3 * CONV_W:, :]) + bf_ref[...])
    for c, ref in enumerate((fqt_ref, fkt_ref, fvt_ref, mqt_ref, mkt_ref, mvt_ref)):
        t = _dot_nt(wqkvt_ref[c * HEADS_W:(c + 1) * HEADS_W, :], h)
        ref[0] = t.reshape(N_HEADS, HEAD_DIM, t.shape[-1])


def _inproj(layer, depth, x3, sc, sh, wtok, wqkvt, bf, kv_states, tm):
    gb, t, d = x3.shape
    tok = lambda w: pl.BlockSpec((1, tm, w), lambda g, i: (g, i, 0))
    qt = pl.BlockSpec((1, N_HEADS, HEAD_DIM, tm), lambda g, i: (g, 0, 0, i))
    kvt = pl.BlockSpec((None, 1, N_HEADS, HEAD_DIM, tm), lambda g, i: (layer, g, 0, 0, i))
    qt_shape = jax.ShapeDtypeStruct((gb, N_HEADS, HEAD_DIM, t), F32)
    kvt_shape = jax.ShapeDtypeStruct((depth, gb, N_HEADS, HEAD_DIM, t), F32)
    in_specs = [tok(d), _mod_spec(sc, tm), _mod_spec(sh, tm),
                _const_spec(wtok.shape), _const_spec(wqkvt.shape), _const_spec(bf.shape)]
    args = [x3, sc, sh, wtok, wqkvt, bf]
    aliases = {}
    if kv_states is not None:
        aliases = {len(args) + k: 4 + k for k in range(N_KV_STATES)}
        in_specs += [pl.BlockSpec(memory_space=pl.ANY)] * N_KV_STATES
        args += list(kv_states)
    return pl.pallas_call(
        _inproj_kernel,
        out_shape=(jax.ShapeDtypeStruct((gb, t, 3 * CONV_W), F32),
                   jax.ShapeDtypeStruct((gb, t, V7X_LANES), F32),
                   qt_shape, qt_shape, kvt_shape, kvt_shape, kvt_shape, kvt_shape),
        grid=(gb, t // tm),
        in_specs=in_specs,
        out_specs=(tok(3 * CONV_W), tok(V7X_LANES), qt, qt, kvt, kvt, kvt, kvt),
        input_output_aliases=aliases,
        compiler_params=_params("arbitrary", "arbitrary"),
        name="inproj",
    )(*args)


def _key_term_movers():
    row = lax.broadcasted_iota(jnp.int32, (V7X_LANES, V7X_LANES), 0)
    col = lax.broadcasted_iota(jnp.int32, (V7X_LANES, V7X_LANES), 1)
    movers = []
    for h in range(N_HEADS):
        m = jnp.zeros((V7X_LANES, V7X_LANES), F32)
        for k in range(3):
            m = jnp.where(row == h + V7X_SUBLANES * k, jnp.where(col == AUG_KEY_TERM + k, 1.0, m), m)
        movers.append(m.astype(BF16))
    return movers


def _pack_key_terms(term, lane):
    hi, mid, lo = _split3(jnp.where(lane < N_HEADS, term, 0.0))
    return (hi + pltpu.roll(mid, V7X_SUBLANES, axis=1) + pltpu.roll(lo, 2 * V7X_SUBLANES, axis=1)).astype(BF16)


def _augment_keys(k_tok, packed_terms, mover, lane):
    x = jnp.where(lane < HEAD_DIM, jnp.concatenate([k_tok, jnp.zeros_like(k_tok)], axis=1),
                  jnp.where(lane < AUG_QUERY_TERM, _dot(packed_terms, mover), jnp.where(lane < AUG_END, 1.0, 0.0)))
    return x.astype(BF16)


def _augment_queries(qt, query_term):
    row = lax.broadcasted_iota(jnp.int32, qt.shape, 0)
    tail = jnp.where(row < AUG_QUERY_TERM - HEAD_DIM, 1.0, 0.0)
    if query_term is not None:
        for k, piece in enumerate(_split3(query_term)):
            tail = jnp.where(row == AUG_QUERY_TERM - HEAD_DIM + k, piece, tail)
    return jnp.concatenate([qt, tail], axis=0).astype(BF16)


def _flash_heads(i, tq, ka_ref, vb_ref, qa_ref, s_ref, p_ref, acc_ref, past_bias):
    key = lax.broadcasted_iota(jnp.int32, (tq, tq), 0)
    qry = lax.broadcasted_iota(jnp.int32, (tq, tq), 1)
    causal = key <= qry

    def step(tiles, carry):
        for t, (j, diag) in enumerate(tiles):
            for h in range(N_HEADS):
                s = _dot(ka_ref[j, h], qa_ref[h])
                if diag:
                    s = jnp.where(causal, s, -jnp.inf)
                elif past_bias is not None:
                    s = s + past_bias(h, j)
                s_ref[t, h] = s
        out, alphas = [], []
        for h in range(N_HEADS):
            m, l = carry[h]
            scores = [s_ref[t, h] for t in range(len(tiles))]
            m_new = m
            for s in scores:
                m_new = jnp.maximum(m_new, jnp.max(s, axis=0, keepdims=True))
            alpha = jnp.exp2(m - m_new)
            l = alpha * l
            for t, s in enumerate(scores):
                p = jnp.exp2(s - m_new)
                p_ref[t, h] = p.astype(BF16)
                l = l + jnp.sum(p, axis=0, keepdims=True)
            out.append((m_new, l))
            alphas.append(alpha)
        for h in range(N_HEADS):
            acc = alphas[h] * acc_ref[h]
            for t, (j, _) in enumerate(tiles):
                acc = acc + _dot(vb_ref[j, h], p_ref[t, h])
            acc_ref[h] = acc
        return tuple(out)

    acc_ref[...] = jnp.zeros(acc_ref.shape, F32)
    init = tuple((jnp.full((1, tq), -jnp.inf, F32), jnp.zeros((1, tq), F32)) for _ in range(N_HEADS))
    carry = lax.fori_loop(0, i // 2, lambda jj, c: step([(2 * jj, False), (2 * jj + 1, False)], c), init)
    carry = lax.cond(i % 2 == 1, lambda c: step([(i - 1, False), (i, True)], c),
                     lambda c: step([(i, True)], c), carry)
    return [(acc_ref[h] / carry[h][1]).T for h in range(N_HEADS)]


def _fox_prompt_kernel(qt_ref, kt_ref, vt_ref, lf_ref, o_ref, ka_ref, vb_ref, *work, tq, nkt):
    i = pl.program_id(1)

    @pl.when(i == 0)
    def _prepare_sequence():
        rows = lax.broadcasted_iota(jnp.int32, (tq, tq), 0)
        cols = lax.broadcasted_iota(jnp.int32, (tq, tq), 1)
        lower = (cols <= rows).astype(F32)
        lane = lax.broadcasted_iota(jnp.int32, (tq, V7X_LANES), 1)
        movers = _key_term_movers()
        carry = jnp.zeros((1, V7X_LANES), F32)
        terms = []
        for c in range(nkt):
            sl = slice(c * tq, (c + 1) * tq)
            cum = _dot_exact(lower, lf_ref[0, sl, :]) + carry
            carry = cum[tq - 1:tq, :]
            vb_ref[c] = vt_ref[0, :, :, sl].astype(BF16)
            terms.append(_pack_key_terms(-LOG2E * cum, lane))
        for h in range(N_HEADS):
            for c in range(nkt):
                ka_ref[c, h] = _augment_keys(kt_ref[0, h, :, c * tq:(c + 1) * tq].T, terms[c], movers[h], lane)

    qt = qt_ref[0] * (ATTN_SCALE * LOG2E)
    qa_ref = work[0]
    for h in range(N_HEADS):
        qa_ref[h] = _augment_queries(qt[h], None)
    heads = _flash_heads(i, tq, ka_ref, vb_ref, *work, None)
    o_ref[0] = jnp.concatenate(heads, axis=1).astype(o_ref.dtype)


def _moba_prompt_kernel(qt_ref, kt_ref, vt_ref, o_ref, ka_ref, vb_ref, means_ref, *work, nb):
    blk = MOBA_BLOCK
    i = pl.program_id(1)

    @pl.when(i == 0)
    def _prepare_sequence():
        lane = lax.broadcasted_iota(jnp.int32, (blk, V7X_LANES), 1)
        pos = lax.broadcasted_iota(jnp.int32, (blk, V7X_LANES), 0)
        brow = lax.broadcasted_iota(jnp.int32, (V7X_SUBLANES, HEAD_DIM), 0)
        means = [jnp.zeros((V7X_SUBLANES, HEAD_DIM), F32) for _ in range(N_HEADS)]
        movers = _key_term_movers()
        slopes = jnp.zeros((1, V7X_LANES), F32)
        for h in range(N_HEADS):
            slopes = jnp.where(lane[0:1] == h, LOG2E * ALIBI_SLOPES[h], slopes)
        terms = [_pack_key_terms((pos + c * blk).astype(F32) * slopes, lane) for c in range(nb)]
        for c in range(nb):
            vb_ref[c] = vt_ref[0, :, :, c * blk:(c + 1) * blk].astype(BF16)
        for h in range(N_HEADS):
            for c in range(nb):
                k_tok = kt_ref[0, h, :, c * blk:(c + 1) * blk].T
                means[h] = jnp.where(brow == c, jnp.sum(k_tok, axis=0, keepdims=True) * (1.0 / blk), means[h])
                ka_ref[c, h] = _augment_keys(k_tok, terms[c], movers[h], lane)
        for h in range(N_HEADS):
            means_ref[h] = means[h]

    q_raw = qt_ref[0]
    brow = lax.broadcasted_iota(jnp.int32, (V7X_SUBLANES, blk), 0)
    brow_f = brow.astype(F32)
    qa_ref = work[0]
    biases = []
    for h in range(N_HEADS):
        gate = jnp.where(brow < i, _dot_exact(means_ref[h], q_raw[h]), -jnp.inf)
        bias = jnp.full(gate.shape, MASKED, F32)
        for _ in range(MOBA_TOPK):
            best = jnp.max(gate, axis=0, keepdims=True)
            first = jnp.min(jnp.where(gate == best, brow_f, float(V7X_SUBLANES)), axis=0, keepdims=True)
            hit = brow_f == first
            bias = jnp.where(hit, jnp.where(best > -jnp.inf, 0.0, bias), bias)
            gate = jnp.where(hit, -jnp.inf, gate)
        biases.append(bias)
        tile_term = jnp.full((1, blk), -LOG2E * ALIBI_SLOPES[h], F32) * (i * blk).astype(F32)
        qa_ref[h] = _augment_queries(q_raw[h] * (ATTN_SCALE * LOG2E), tile_term)

    def past_bias(h, j):
        return jnp.sum(jnp.where(brow == j, biases[h], 0.0), axis=0, keepdims=True)

    heads = _flash_heads(i, blk, ka_ref, vb_ref, *work, past_bias)
    o_ref[0] = jnp.concatenate(heads, axis=1).astype(o_ref.dtype)


def _prompt_attention(kind, layer, qt, kt, vt, lf=None):
    g, _, _, t = qt.shape
    tq = MOBA_BLOCK
    nkt = t // tq
    assert nkt <= V7X_SUBLANES, "block gates are kept on one sublane group"
    qspec = pl.BlockSpec((1, N_HEADS, HEAD_DIM, tq), lambda b, i: (b, 0, 0, i))
    kvspec = pl.BlockSpec((None, 1, N_HEADS, HEAD_DIM, t), lambda b, i: (layer, b, 0, 0, 0))
    scratch = [pltpu.VMEM((nkt, N_HEADS, tq, V7X_LANES), BF16), pltpu.VMEM((nkt, N_HEADS, HEAD_DIM, tq), BF16)]
    work = [pltpu.VMEM((N_HEADS, V7X_LANES, tq), BF16), pltpu.VMEM((2, N_HEADS, tq, tq), F32),
            pltpu.VMEM((2, N_HEADS, tq, tq), BF16), pltpu.VMEM((N_HEADS, HEAD_DIM, tq), F32)]
    if kind == "fox":
        body = functools.partial(_fox_prompt_kernel, tq=tq, nkt=nkt)
        in_specs = [qspec, kvspec, kvspec, pl.BlockSpec((1, t, V7X_LANES), lambda b, i: (b, 0, 0))]
        args = (qt, kt, vt, lf)
    else:
        body = functools.partial(_moba_prompt_kernel, nb=nkt)
        in_specs = [qspec, kvspec, kvspec]
        args = (qt, kt, vt)
        scratch.append(pltpu.VMEM((N_HEADS, V7X_SUBLANES, HEAD_DIM), F32))
    return pl.pallas_call(
        body,
        out_shape=jax.ShapeDtypeStruct((g, t, HEADS_W), BF16),
        grid=(g, nkt),
        in_specs=in_specs,
        out_specs=pl.BlockSpec((1, tq, HEADS_W), lambda b, i: (b, i, 0)),
        scratch_shapes=scratch + work,
        compiler_params=_params("arbitrary", "arbitrary"),
        name=kind + "_prompt",
    )(*args)


def _page_copies(pt_ref, caches, bufs, sems, layer, g, s, slot, pages):
    return [pltpu.make_async_copy(cache.at[layer, pt_ref[g, s * pages + r]], buf.at[slot, r], sems.at[slot, a])
            for r in range(pages) for a, (cache, buf) in enumerate(zip(caches, bufs))]


def _stream_pages(pt_ref, caches, bufs, sems, layer, pages, n_steps):
    g, s = pl.program_id(0), pl.program_id(1)
    step = g * n_steps + s
    slot = step & 1
    last = pl.num_programs(0) * n_steps - 1

    @pl.when(step == 0)
    def _first():
        for c in _page_copies(pt_ref, caches, bufs, sems, layer, g, s, slot, pages):
            c.start()

    @pl.when(step < last)
    def _prefetch():
        wrap = s == n_steps - 1
        g_next = jnp.where(wrap, g + 1, g)
        s_next = jnp.where(wrap, 0, s + 1)
        for c in _page_copies(pt_ref, caches, bufs, sems, layer, g_next, s_next, 1 - slot, pages):
            c.start()

    for c in _page_copies(pt_ref, caches, bufs, sems, layer, g, s, slot, pages):
        c.wait()
    return slot


def _all_heads(page_ref):
    return page_ref[...].reshape(HEADS_W, page_ref.shape[-1])


def _own_tokens(xt_ref, t_new):
    x = _all_heads(xt_ref)
    n = x.shape[1]
    return pltpu.roll(x, lax.rem(n - pl.program_id(0) * t_new, n), axis=1)[:, :PAGE]


def _block_diag_queries(qt_ref, t_new, scale):
    q_tok = (_own_tokens(qt_ref, t_new) * scale).T[:t_new]
    z = jnp.zeros((t_new, HEAD_DIM), F32)
    return jnp.concatenate(
        [jnp.concatenate([q_tok[:, h * HEAD_DIM:(h + 1) * HEAD_DIM] if c == h else z for c in range(N_HEADS)], axis=1)
         for h in range(N_HEADS)], axis=0)


def _own_head_columns(x, t):
    lane = lax.broadcasted_iota(jnp.int32, (t, HEADS_W), 1)
    out = jnp.zeros((t, HEADS_W), x.dtype)
    for h in range(N_HEADS):
        own = (lane >= h * HEAD_DIM) & (lane < (h + 1) * HEAD_DIM)
        out = jnp.where(own, x[h * t:(h + 1) * t], out)
    return out


def _per_head_rows(values, t):
    row = lax.broadcasted_iota(jnp.int32, (N_HEADS * t, 1), 0)
    out = jnp.zeros((N_HEADS * t, 1), F32)
    for h, v in enumerate(values):
        out = jnp.where((row >= h * t) & (row < (h + 1) * t), v, out)
    return out


def _fox_sample_kernel(pt_ref, q_ref, knt_ref, vnt_ref, lfn_ref, ck_ref, cv_ref, cl_ref, o_ref,
                       qbd_ref, m_ref, l_ref, acc_ref, carry_ref, kbuf, vbuf, lbuf, sems, *,
                       layer, pages, n_steps, t_new):
    slot = _stream_pages(pt_ref, (ck_ref, cv_ref, cl_ref), (kbuf, vbuf, lbuf), sems, layer, pages, n_steps)
    kp, vp, lp = ([buf.at[slot, r] for r in range(pages)] for buf in (kbuf, vbuf, lbuf))
    s = pl.program_id(1)
    n_rows = N_HEADS * t_new

    @pl.when(s == 0)
    def _init():
        qbd_ref[...] = _block_diag_queries(q_ref, t_new, ATTN_SCALE).astype(BF16)
        m_ref[...] = jnp.full(m_ref.shape, -jnp.inf, F32)
        l_ref[...] = jnp.zeros(l_ref.shape, F32)
        acc_ref[...] = jnp.zeros(acc_ref.shape, F32)
        carry_ref[...] = jnp.zeros(carry_ref.shape, F32)

    m, l, acc = m_ref[:, 0:1], l_ref[:, 0:1], acc_ref[...]

    n_rows_lf = pages * V7X_SUBLANES
    lf = jnp.concatenate([lp[r][...] for r in range(pages)], axis=0)
    ri = lax.broadcasted_iota(jnp.int32, (PAGE, PAGE), 0)
    ci = lax.broadcasted_iota(jnp.int32, (PAGE, PAGE), 1)
    upper = jnp.where(ri <= ci, 1.0, 0.0).astype(BF16)
    pi = lax.broadcasted_iota(jnp.int32, (n_rows_lf, n_rows_lf), 0)
    pj = lax.broadcasted_iota(jnp.int32, (n_rows_lf, n_rows_lf), 1)
    same_head = (pi & (V7X_SUBLANES - 1)) == (pj & (V7X_SUBLANES - 1))
    earlier_page = jnp.where(same_head, jnp.where(pj < (pi & -V7X_SUBLANES), 1.0, 0.0), 0.0).astype(BF16)
    any_page = jnp.where(same_head, 1.0, 0.0).astype(BF16)
    lf_pieces = [x.astype(BF16) for x in _split3(lf)]
    totals = sum(_dot(x, jnp.ones((PAGE, PAGE), BF16)) for x in lf_pieces)
    total_pieces = [x.astype(BF16) for x in _split3(totals)]
    carry = carry_ref[...]
    cum = (sum(_dot(x, upper) for x in lf_pieces) + sum(_dot(earlier_page, x) for x in total_pieces)) + carry
    carry = carry + sum(_dot(any_page, x) for x in total_pieces)
    key_gate = jnp.concatenate(
        [jnp.concatenate([jnp.broadcast_to(cum[r * V7X_SUBLANES + h:r * V7X_SUBLANES + h + 1, :], (t_new, PAGE))
                          for r in range(pages)], axis=1) for h in range(N_HEADS)], axis=0)

    def update(scores, vt, m, l, acc):
        m_new = jnp.maximum(m, jnp.max(scores, axis=1, keepdims=True))
        alpha = jnp.exp(m - m_new)
        p = jnp.exp(scores - m_new)
        return m_new, alpha * l + jnp.sum(p, axis=1, keepdims=True), alpha * acc + _dot_nt(p.astype(BF16), vt)

    kt = jnp.concatenate([_all_heads(kp[r]) for r in range(pages)], axis=1).astype(BF16)
    vt = jnp.concatenate([_all_heads(vp[r]) for r in range(pages)], axis=1).astype(BF16)
    m, l, acc = update(_dot(qbd_ref[...], kt) - key_gate, vt, m, l, acc)

    carry_ref[...] = carry
    m_ref[...] = jnp.broadcast_to(m, m_ref.shape)
    l_ref[...] = jnp.broadcast_to(l, l_ref.shape)
    acc_ref[...] = acc

    @pl.when(s == n_steps - 1)
    def _new_tokens():
        lane = lax.broadcasted_iota(jnp.int32, (n_rows, PAGE), 1)
        qi = lax.broadcasted_iota(jnp.int32, (n_rows, PAGE), 0) & (t_new - 1)
        cum_n = sum(_dot(x.astype(BF16), upper) for x in _split3(lfn_ref[0])) + carry[:V7X_SUBLANES]
        gate_n = jnp.concatenate([jnp.broadcast_to(cum_n[h:h + 1, :], (t_new, PAGE)) for h in range(N_HEADS)], axis=0)
        scores = jnp.where(lane <= qi, _dot(qbd_ref[...], _own_tokens(knt_ref, t_new).astype(BF16)) - gate_n, -jnp.inf)
        _, l_fin, acc_fin = update(scores, _own_tokens(vnt_ref, t_new).astype(BF16), m, l, acc)
        o_ref[0] = _own_head_columns(acc_fin / l_fin, t_new).astype(o_ref.dtype)


def _moba_sample_kernel(pt_ref, q_ref, knt_ref, vnt_ref, ck_ref, cv_ref, o_ref, qbd_ref, q32_ref, means_ref,
                        mb_ref, lb_ref, accb_ref, kbuf, vbuf, sems, *, layer, pages, n_steps, p_len, t_new):
    slot = _stream_pages(pt_ref, (ck_ref, cv_ref), (kbuf, vbuf), sems, layer, pages, n_steps)
    kp, vp = ([buf.at[slot, r] for r in range(pages)] for buf in (kbuf, vbuf))
    blk = MOBA_BLOCK
    nb = p_len // blk
    per_step = pages * PAGE // blk
    s = pl.program_id(1)
    n_rows = N_HEADS * t_new

    @pl.when(s == 0)
    def _init():
        q32_ref[...] = _block_diag_queries(q_ref, t_new, 1.0)
        qbd_ref[...] = _block_diag_queries(q_ref, t_new, ATTN_SCALE).astype(BF16)
        means_ref[...] = jnp.zeros(means_ref.shape, F32)
        mb_ref[...] = jnp.full(mb_ref.shape, -jnp.inf, F32)
        lb_ref[...] = jnp.zeros(lb_ref.shape, F32)

    means, mb, lb = means_ref[...], mb_ref[...], lb_ref[...]
    lane_q = lax.broadcasted_iota(jnp.int32, (n_rows, V7X_LANES), 1)
    lane_d = lax.broadcasted_iota(jnp.int32, (HEADS_W, V7X_LANES), 1)
    kcol = lax.broadcasted_iota(jnp.int32, (1, blk), 1).astype(F32)
    slope = _per_head_rows(ALIBI_SLOPES, t_new)

    k32 = jnp.concatenate([_all_heads(kp[r]) for r in range(pages)], axis=1)
    scores = _dot(qbd_ref[...], k32.astype(BF16))
    probs = []
    for bb in range(per_step):
        b = s * per_step + bb
        cols = slice(bb * blk, (bb + 1) * blk)
        means = jnp.where(lane_d == b, jnp.sum(k32[:, cols], axis=1, keepdims=True) * (1.0 / blk), means)
        sc = scores[:, cols] + slope * (kcol + (b * blk - p_len).astype(F32))
        mx = jnp.max(sc, axis=1, keepdims=True)
        p = jnp.exp(sc - mx)
        probs.append(p.astype(BF16))
        mb = jnp.where(lane_q == b, mx, mb)
        lb = jnp.where(lane_q == b, jnp.sum(p, axis=1, keepdims=True), lb)
    ppb = blk // PAGE
    for bb in range(per_step):
        vblk = jnp.concatenate([_all_heads(vp[bb * ppb + r]) for r in range(ppb)], axis=1).astype(BF16)
        accb_ref[s * per_step + bb] = _dot_nt(probs[bb], vblk)

    means_ref[...] = means
    mb_ref[...] = mb
    lb_ref[...] = lb

    @pl.when(s == n_steps - 1)
    def _select_and_combine():
        lane_f = lane_q.astype(F32)
        qi = lax.broadcasted_iota(jnp.int32, (n_rows, V7X_LANES), 0) & (t_new - 1)
        gate = jnp.where(lane_q < nb, _dot_exact(q32_ref[...], means), -jnp.inf)
        sel = jnp.zeros(gate.shape, F32)
        for _ in range(MOBA_TOPK):
            best = jnp.max(gate, axis=1, keepdims=True)
            first = jnp.min(jnp.where(gate == best, lane_f, float(V7X_LANES)), axis=1, keepdims=True)
            hit = lane_f == first
            sel = jnp.where(hit, jnp.where(best > -jnp.inf, 1.0, sel), sel)
            gate = jnp.where(hit, -jnp.inf, gate)
        sc = _dot(qbd_ref[...], _own_tokens(knt_ref, t_new).astype(BF16)) + slope * lane_f
        sc = jnp.where(lane_q <= qi, sc, -jnp.inf)
        m_own = jnp.max(sc, axis=1, keepdims=True)
        p_own = jnp.exp(sc - m_own)
        l_own = jnp.sum(p_own, axis=1, keepdims=True)
        acc_own = _dot_nt(p_own.astype(BF16), _own_tokens(vnt_ref, t_new).astype(BF16))
        picked = sel > 0.0
        m_all = jnp.maximum(m_own, jnp.max(jnp.where(picked, mb, -jnp.inf), axis=1, keepdims=True))
        w = jnp.where(picked, jnp.exp(mb - m_all), 0.0)
        w_own = jnp.exp(m_own - m_all)
        den = jnp.sum(w * lb, axis=1, keepdims=True) + w_own * l_own
        num = w_own * acc_own
        for b in range(nb):
            num = num + jnp.sum(jnp.where(lane_q == b, w, 0.0), axis=1, keepdims=True) * accb_ref[b]
        o_ref[0] = _own_head_columns(num / den, t_new).astype(o_ref.dtype)


def _sample_attention(kind, layer, g, t_new, page_table, qt, knt, vnt, cache_kt, cache_vt, lfn=None, cache_lft=None):
    n_pages = page_table.shape[1]
    pages = min(PAGES_PER_STEP, n_pages)
    n_steps = n_pages // pages
    n_rows = N_HEADS * t_new
    n_tok = g * t_new
    assert t_new & (t_new - 1) == 0 and pages * PAGE % MOBA_BLOCK == 0 and n_tok % V7X_LANES == 0
    per_seq = lambda blk: pl.BlockSpec(blk, lambda b, s, pt: (b,) + (0,) * (len(blk) - 1))
    in_hbm = pl.BlockSpec(memory_space=pl.ANY)
    new_kv = pl.BlockSpec((None, None, N_HEADS, HEAD_DIM, n_tok), lambda b, s, pt: (layer, 0, 0, 0, 0))
    in_specs = [pl.BlockSpec((None, N_HEADS, HEAD_DIM, n_tok), lambda b, s, pt: (0, 0, 0, 0)), new_kv, new_kv]
    args = [qt, knt, vnt]
    if kind == "fox":
        in_specs.append(per_seq((1, V7X_SUBLANES, PAGE)))
        args.append(lfn)
    in_specs += [in_hbm, in_hbm]
    args += [cache_kt, cache_vt]
    stat = pltpu.VMEM((n_rows, V7X_LANES), F32)
    q_bd = pltpu.VMEM((n_rows, HEADS_W), BF16)
    page_buf = pltpu.VMEM((2, pages, N_HEADS, HEAD_DIM, PAGE), F32)
    if kind == "fox":
        in_specs.append(in_hbm)
        args.append(cache_lft)
        body = functools.partial(_fox_sample_kernel, layer=layer, pages=pages, n_steps=n_steps, t_new=t_new)
        scratch = [q_bd, stat, stat, pltpu.VMEM((n_rows, HEADS_W), F32),
                   pltpu.VMEM((pages * V7X_SUBLANES, V7X_LANES), F32),
                   page_buf, page_buf, pltpu.VMEM((2, pages, V7X_SUBLANES, PAGE), F32),
                   pltpu.SemaphoreType.DMA((2, 3))]
    else:
        p_len = n_pages * PAGE
        assert p_len // MOBA_BLOCK <= V7X_LANES
        body = functools.partial(_moba_sample_kernel, layer=layer, pages=pages, n_steps=n_steps, p_len=p_len,
                                 t_new=t_new)
        scratch = [q_bd, pltpu.VMEM((n_rows, HEADS_W), F32), pltpu.VMEM((HEADS_W, V7X_LANES), F32), stat, stat,
                   pltpu.VMEM((p_len // MOBA_BLOCK, n_rows, HEADS_W), F32),
                   page_buf, page_buf, pltpu.SemaphoreType.DMA((2, 2))]
    return pl.pallas_call(
        body,
        out_shape=jax.ShapeDtypeStruct((g, t_new, HEADS_W), BF16),
        grid_spec=pltpu.PrefetchScalarGridSpec(
            num_scalar_prefetch=1,
            grid=(g, n_steps),
            in_specs=in_specs,
            out_specs=pl.BlockSpec((1, t_new, HEADS_W), lambda b, s, pt: (b, 0, 0)),
            scratch_shapes=scratch),
        compiler_params=_params("arbitrary", "arbitrary"),
        name=kind + "_sample",
    )(page_table, *args)


def _causal_shifts(u, fill, period):
    out = []
    for j in (1, 2):
        r = pltpu.roll(u, j, axis=0)
        f = pltpu.roll(fill, j, axis=0)
        if period is None:
            row = lax.broadcasted_iota(jnp.int32, f.shape, 0)
            out.append(jnp.concatenate([jnp.where(row >= j, r[:V7X_SUBLANES], f), r[V7X_SUBLANES:]], axis=0))
        else:
            t = lax.broadcasted_iota(jnp.int32, u.shape, 0) & (period - 1)
            out.append(jnp.where(t >= j, r, f))
    return out


def _conv3(u, fill, period, w_ref, cols):
    s1, s2 = _causal_shifts(u, fill, period)
    return w_ref[0:1, cols] * s2 + w_ref[1:2, cols] * s1 + w_ref[2:3, cols] * u


def _outproj_kernel(x_ref, pc_ref, yf_ref, ym_ref, gt_ref, cw_ref, wo_ref, g_ref, b_ref, *rest,
                    period, dn_alpha):
    if period is None:
        o_ref, us_ref, carry_ref = rest

        @pl.when(pl.program_id(1) == 0)
        def _fresh_sequence():
            carry_ref[...] = jnp.zeros(carry_ref.shape, F32)

        fill = carry_ref[...]
    else:
        fill_ref, o_ref, us_ref = rest
        fill = fill_ref[0]
    pc = pc_ref[0]
    u = pc[:, CONV_W:2 * CONV_W] * pc[:, 2 * CONV_W:]
    y_conv = pc[:, :CONV_W] * _conv3(u, fill, period, cw_ref, slice(None))
    mixed = _dot(jnp.concatenate([y_conv.astype(BF16), yf_ref[0], ym_ref[0]], axis=1), wo_ref[...])
    z = dn_alpha * x_ref[0] + (1.0 + gt_ref[0]) * mixed
    o_ref[0] = _layer_norm(z, g_ref[...], b_ref[...])
    if period is None:
        tail = u[u.shape[0] - V7X_SUBLANES:]
        carry_ref[...] = tail
        us_ref[0] = tail
    else:
        us_ref[0] = u


def _outproj(x3, pconv, yf, ym, gt, conv_w, wo, ln_g, ln_b, fill, tm, dn_alpha):
    gb, t, d = x3.shape
    period = None if fill is None else V7X_SUBLANES
    tok = lambda w: pl.BlockSpec((1, tm, w), lambda g, i: (g, i, 0))
    in_specs = [tok(d), tok(3 * CONV_W), tok(HEADS_W), tok(HEADS_W), _mod_spec(gt, tm),
                _const_spec(conv_w.shape), _const_spec(wo.shape), _const_spec(ln_g.shape), _const_spec(ln_b.shape)]
    args = [x3, pconv, yf, ym, gt, conv_w, wo, ln_g, ln_b]
    scratch = []
    if period is None:
        us_rows = V7X_SUBLANES
        us_spec = pl.BlockSpec((1, us_rows, CONV_W), lambda g, i: (g, 0, 0))
        scratch.append(pltpu.VMEM((V7X_SUBLANES, CONV_W), F32))
    else:
        us_rows = t
        us_spec = tok(CONV_W)
        in_specs.append(tok(CONV_W))
        args.append(fill)
    return pl.pallas_call(
        functools.partial(_outproj_kernel, period=period, dn_alpha=dn_alpha),
        out_shape=(jax.ShapeDtypeStruct((gb, t, d), F32), jax.ShapeDtypeStruct((gb, us_rows, CONV_W), F32)),
        grid=(gb, t // tm),
        in_specs=in_specs,
        out_specs=(tok(d), us_spec),
        scratch_shapes=scratch,
        compiler_params=_params("arbitrary", "arbitrary"),
        name="outproj",
    )(*args)


def _ffn_kernel(x_ref, sc_ref, sh_ref, gt_ref, wup_ref, cw_ref, cb_ref, wdn_ref, g_ref, b_ref, *rest,
                period, dn_alpha):
    if period is None:
        o_ref, us_ref, carry_ref = rest

        @pl.when(pl.program_id(1) == 0)
        def _fresh_sequence():
            carry_ref[...] = jnp.zeros(carry_ref.shape, F32)
    else:
        fill_ref, o_ref, us_ref = rest
    x = x_ref[0]
    tm = x.shape[0]
    d_ff = wdn_ref.shape[0]
    h = (x * (1.0 + sc_ref[0]) + sh_ref[0]).astype(BF16)
    halves = []
    for half in range(2):
        cols = slice(half * d_ff, (half + 1) * d_ff)
        u = _dot(h, wup_ref[:, cols])
        fill = carry_ref[:, cols] if period is None else fill_ref[0, :, cols]
        halves.append(_conv3(u, fill, period, cw_ref, cols) + cb_ref[:, cols])
        if period is None:
            tail = u[tm - V7X_SUBLANES:]
            carry_ref[:, cols] = tail
            us_ref[0, :, cols] = tail
        else:
            us_ref[0, :, cols] = u
    gate, val = halves
    y = _dot((gate * _sigmoid(gate) * val).astype(BF16), wdn_ref[...])
    z = dn_alpha * x + (1.0 + gt_ref[0]) * y
    o_ref[0] = _layer_norm(z, g_ref[...], b_ref[...])


def _ffn(x3, sc, sh, gt, wup, conv_w, conv_b, wdn, ln_g, ln_b, fill, tm, dn_alpha):
    gb, t, d = x3.shape
    n_up = wup.shape[1]
    period = None if fill is None else V7X_SUBLANES
    tok = lambda w: pl.BlockSpec((1, tm, w), lambda g, i: (g, i, 0))
    in_specs = [tok(d), _mod_spec(sc, tm), _mod_spec(sh, tm), _mod_spec(gt, tm),
                _const_spec(wup.shape), _const_spec(conv_w.shape), _const_spec(conv_b.shape),
                _const_spec(wdn.shape), _const_spec(ln_g.shape), _const_spec(ln_b.shape)]
    args = [x3, sc, sh, gt, wup, conv_w, conv_b, wdn, ln_g, ln_b]
    scratch = []
    if period is None:
        us_rows = V7X_SUBLANES
        us_spec = pl.BlockSpec((1, us_rows, n_up), lambda g, i: (g, 0, 0))
        scratch.append(pltpu.VMEM((V7X_SUBLANES, n_up), F32))
    else:
        us_rows = t
        us_spec = tok(n_up)
        in_specs.append(tok(n_up))
        args.append(fill)
    return pl.pallas_call(
        functools.partial(_ffn_kernel, period=period, dn_alpha=dn_alpha),
        out_shape=(jax.ShapeDtypeStruct((gb, t, d), F32), jax.ShapeDtypeStruct((gb, us_rows, n_up), F32)),
        grid=(gb, t // tm),
        in_specs=in_specs,
        out_specs=(tok(d), us_spec),
        scratch_shapes=scratch,
        compiler_params=_params("arbitrary", "arbitrary"),
        name="ffn",
    )(*args)


def _seq_fill(state):
    g, w, c = state.shape
    padded = jnp.pad(state, ((0, 0), (V7X_SUBLANES - w, 0), (0, 0)))
    return jnp.roll(padded, -1, axis=0).reshape(1, g * V7X_SUBLANES, c)


def _layer(l, depth, x3, mods, weights, kv_states, dn_alpha, tm, sample=None):
    sh_a, sc_a, gt_a, sh_f, sc_f, gt_f = mods
    (wtok, wqkvt, bf, conv_w, wo, ln1_g, ln1_b, wup, ffn_w, ffn_b, wdn, ln2_g, ln2_b) = weights
    pconv, lf, fqt, mqt, *kv_states = _inproj(l, depth, x3, sc_a, sh_a, wtok, wqkvt, bf, kv_states, tm)
    fkt, fvt, mkt, mvt = kv_states
    if sample is None:
        y_fox = _prompt_attention("fox", l, fqt, fkt, fvt, lf)
        y_moba = _prompt_attention("moba", l, mqt, mkt, mvt)
        conv_fill = ffn_fill = None
    else:
        g, t_new = sample["g"], sample["t"]
        lfn = lf[0].reshape(g, t_new, V7X_LANES)[:, :, :V7X_SUBLANES].transpose(0, 2, 1)
        lfn = jnp.pad(lfn, ((0, 0), (0, 0), (0, PAGE - t_new)))
        y_fox = _sample_attention("fox", l, g, t_new, sample["page_table"], fqt, fkt, fvt,
                                  sample["fox_kt"], sample["fox_vt"], lfn, sample["fox_lft"])
        y_moba = _sample_attention("moba", l, g, t_new, sample["page_table"], mqt, mkt, mvt,
                                   sample["moba_kt"], sample["moba_vt"])
        y_fox = y_fox.reshape(1, g * t_new, HEADS_W)
        y_moba = y_moba.reshape(1, g * t_new, HEADS_W)
        conv_fill = _seq_fill(sample["state_conv"][l])
        ffn_fill = _seq_fill(sample["state_ffn"][l])
    x1, conv_tail = _outproj(x3, pconv, y_fox, y_moba, gt_a, conv_w, wo, ln1_g, ln1_b, conv_fill, tm, dn_alpha)
    x2, ffn_tail = _ffn(x1, sc_f, sh_f, gt_f, wup, ffn_w, ffn_b, wdn, ln2_g, ln2_b, ffn_fill, tm, dn_alpha)
    return x2, kv_states, (lf, conv_tail, ffn_tail)


def kernel(x_prompt, x_sample, cache_fox_k, cache_fox_v, cache_fox_logf, cache_moba_k, cache_moba_v, state_conv, state_ffn, page_table, c_prompt, c_sample, w_ada, b_ada, w_in, b_f, conv_w, w_o, ln1_g, ln1_b, w_up, ffn_conv_w, ffn_conv_b, w_down, ln2_g, ln2_b):
    depth, d, _ = w_ada.shape
    gp, tp, _ = x_prompt.shape
    gs, ts, _ = x_sample.shape
    dn_alpha = (2 * depth) ** 0.25
    n_conv, n_qkv = 3 * CONV_W, 3 * HEADS_W

    ada = _ada(jnp.concatenate([c_prompt, c_sample], axis=0), w_ada, b_ada)
    to_kt = lambda c: jnp.transpose(c, (0, 1, 3, 4, 2))
    lft_cache = jnp.pad(jnp.transpose(cache_fox_logf, (0, 1, 3, 2)),
                        ((0, 0), (0, 0), (0, V7X_SUBLANES - N_HEADS), (0, 0)))
    sample = dict(g=gs, t=ts, page_table=page_table, state_conv=state_conv, state_ffn=state_ffn,
                  fox_kt=to_kt(cache_fox_k), fox_vt=to_kt(cache_fox_v), fox_lft=lft_cache,
                  moba_kt=to_kt(cache_moba_k), moba_vt=to_kt(cache_moba_v))

    xp = x_prompt
    xs = x_sample.reshape(1, gs * ts, d)
    kv_p = kv_s = None
    small_p, small_s = [], []
    for l in range(depth):
        wt = jnp.transpose(w_in, (2, 0, 1))[:, l]
        fox0, ff0, moba0 = n_conv, n_conv + n_qkv, n_conv + n_qkv + N_HEADS
        wtok = jnp.concatenate([wt[:n_conv], jnp.pad(wt[ff0:moba0], ((0, V7X_LANES - N_HEADS), (0, 0)))],
                               axis=0).astype(BF16)
        wqkvt = jnp.concatenate([wt[fox0:ff0], wt[moba0:]], axis=0).astype(BF16)
        bf = jnp.pad(b_f[l], (0, V7X_LANES - N_HEADS))[None]
        weights = (wtok, wqkvt, bf, conv_w[l], w_o[l].astype(BF16), ln1_g[l][None], ln1_b[l][None],
                   w_up[l].astype(BF16), ffn_conv_w[l], ffn_conv_b[l][None], w_down[l].astype(BF16),
                   ln2_g[l][None], ln2_b[l][None])
        mods_p = tuple(a[:, None, :] for a in jnp.split(ada[l, :gp], 6, axis=-1))
        mods_s = tuple(jnp.repeat(a, ts, axis=0)[None] for a in jnp.split(ada[l, gp:], 6, axis=-1))
        xp, kv_p, st_p = _layer(l, depth, xp, mods_p, weights, kv_p, dn_alpha, tm=512)
        xs, kv_s, st_s = _layer(l, depth, xs, mods_s, weights, kv_s, dn_alpha, tm=gs * ts, sample=sample)
        small_p.append(st_p)
        small_s.append(st_s)

    stack = lambda states, i: jnp.stack([s[i] for s in states])
    out = [xp, xs.reshape(gs, ts, d)]
    fk, fv, mk, mv = (jnp.transpose(a, (0, 1, 4, 2, 3)) for a in kv_p)
    out += [fk, fv, stack(small_p, 0)[..., :N_HEADS], mk, mv,
            stack(small_p, 1)[:, :, V7X_SUBLANES - 2:], stack(small_p, 2)[:, :, V7X_SUBLANES - 2:]]
    fk, fv, mk, mv = (jnp.transpose(a[:, 0].reshape(depth, N_HEADS, HEAD_DIM, gs, ts), (0, 3, 4, 1, 2))
                      for a in kv_s)
    tail_s = lambda i: stack(small_s, i).reshape(depth, gs, ts, -1)[:, :, ts - 2:]
    out += [fk, fv, stack(small_s, 0).reshape(depth, gs, ts, V7X_LANES)[..., :N_HEADS], mk, mv,
            tail_s(1), tail_s(2)]
    return tuple(out)
```

```python
import functools

import jax
import jax.numpy as jnp
from jax import lax
from jax.experimental import pallas as pl
from jax.experimental.pallas import tpu as pltpu

F32 = jnp.float32
BF16 = jnp.bfloat16
HIGHEST = lax.Precision.HIGHEST

HEAD_DIM = 64
N_HEADS = 6
HEADS_W = N_HEADS * HEAD_DIM
CONV_W = 256
MOBA_BLOCK = 256
MOBA_TOPK = 3
PAGE = 128
LN_EPS = 1e-5
ATTN_SCALE = HEAD_DIM ** -0.5
LOG2E = 1.4426950408889634
ALIBI_SLOPES = tuple(2.0 ** (-8.0 * (h + 1) / N_HEADS) for h in range(N_HEADS))
MASKED = -1e30

V7X_LANES = 128
V7X_SUBLANES = 8
V7X_VMEM_BYTES = 64 * 1024 * 1024
VMEM_LIMIT = 58 * 1024 * 1024
PAGES_PER_STEP = 64
AUG_KEY_TERM = HEAD_DIM
AUG_QUERY_TERM = HEAD_DIM + 3
AUG_END = HEAD_DIM + 6


def _dot(a, b):
    return jnp.dot(a, b, preferred_element_type=F32)


def _dot_nt(a, b):
    return lax.dot_general(a, b, (((1,), (1,)), ((), ())), preferred_element_type=F32)


def _dot_exact(a, b):
    return jnp.dot(a, b, precision=HIGHEST, preferred_element_type=F32)


def _sigmoid(x):
    return 1.0 / (1.0 + jnp.exp(-x))


def _log_sigmoid(x):
    return jnp.minimum(x, 0.0) - jnp.log(1.0 + jnp.exp(-jnp.abs(x)))


def _layer_norm(z, g, b):
    mu = jnp.mean(z, axis=-1, keepdims=True)
    zc = z - mu
    var = jnp.mean(zc * zc, axis=-1, keepdims=True)
    return zc * lax.rsqrt(var + LN_EPS) * g + b


def _split3(x):
    hi = x.astype(BF16).astype(F32)
    rest = x - hi
    mid = rest.astype(BF16).astype(F32)
    return hi, mid, rest - mid


def _params(*semantics):
    return pltpu.CompilerParams(dimension_semantics=semantics, vmem_limit_bytes=VMEM_LIMIT)


def _const_spec(shape):
    zeros = (0,) * len(shape)
    return pl.BlockSpec(shape, lambda *_: zeros, pipeline_mode=pl.Buffered(1))


def _mod_spec(arr, tm):
    d = arr.shape[-1]
    if arr.shape[1] == 1:
        return pl.BlockSpec((1, 1, d), lambda g, i: (g, 0, 0))
    return pl.BlockSpec((1, tm, d), lambda g, i: (g, i, 0))


def _ada_kernel(c_ref, w_ref, b_ref, o_ref):
    c = c_ref[...]
    o_ref[0] = _dot_exact(c * _sigmoid(c), w_ref[0]) + b_ref[0]


def _ada(c_all, w_ada, b_ada):
    depth, d, n = w_ada.shape
    g = c_all.shape[0]
    tn = 1536
    return pl.pallas_call(
        _ada_kernel,
        out_shape=jax.ShapeDtypeStruct((depth, g, n), F32),
        grid=(depth, n // tn),
        in_specs=[pl.BlockSpec((g, d), lambda l, j: (0, 0)),
                  pl.BlockSpec((1, d, tn), lambda l, j: (l, 0, j)),
                  pl.BlockSpec((1, 1, tn), lambda l, j: (l, 0, j))],
        out_specs=pl.BlockSpec((1, g, tn), lambda l, j: (l, 0, j)),
        compiler_params=_params("arbitrary", "arbitrary"),
        name="ada",
    )(c_all, w_ada, b_ada.reshape(depth, 1, n))


N_KV_STATES = 4


def _inproj_kernel(x_ref, sc_ref, sh_ref, wtok_ref, wqkvt_ref, bf_ref, *rest):
    conv_ref, lf_ref, fqt_ref, mqt_ref, fkt_ref, fvt_ref, mkt_ref, mvt_ref = rest[len(rest) - 8:]
    h = (x_ref[0] * (1.0 + sc_ref[0]) + sh_ref[0]).astype(BF16)
    conv_ref[0] = _dot_nt(h, wtok_ref[0:3 * CONV_W, :])
    lf_ref[0] = _log_sigmoid(_dot_nt(h, wtok_ref[3 * CONV_W:, :]) + bf_ref[...])
    for c, ref in enumerate((fqt_ref, fkt_ref, fvt_ref, mqt_ref, mkt_ref, mvt_ref)):
        t = _dot_nt(wqkvt_ref[c * HEADS_W:(c + 1) * HEADS_W, :], h)
        ref[0] = t.reshape(N_HEADS, HEAD_DIM, t.shape[-1])


def _inproj(layer, depth, x3, sc, sh, wtok, wqkvt, bf, kv_states, tm):
    gb, t, d = x3.shape
    tok = lambda w: pl.BlockSpec((1, tm, w), lambda g, i: (g, i, 0))
    qt = pl.BlockSpec((1, N_HEADS, HEAD_DIM, tm), lambda g, i: (g, 0, 0, i))
    kvt = pl.BlockSpec((None, 1, N_HEADS, HEAD_DIM, tm), lambda g, i: (layer, g, 0, 0, i))
    qt_shape = jax.ShapeDtypeStruct((gb, N_HEADS, HEAD_DIM, t), F32)
    kvt_shape = jax.ShapeDtypeStruct((depth, gb, N_HEADS, HEAD_DIM, t), F32)
    in_specs = [tok(d), _mod_spec(sc, tm), _mod_spec(sh, tm),
                _const_spec(wtok.shape), _const_spec(wqkvt.shape), _const_spec(bf.shape)]
    args = [x3, sc, sh, wtok, wqkvt, bf]
    aliases = {}
    if kv_states is not None:
        aliases = {len(args) + k: 4 + k for k in range(N_KV_STATES)}
        in_specs += [pl.BlockSpec(memory_space=pl.ANY)] * N_KV_STATES
        args += list(kv_states)
    return pl.pallas_call(
        _inproj_kernel,
        out_shape=(jax.ShapeDtypeStruct((gb, t, 3 * CONV_W), F32),
                   jax.ShapeDtypeStruct((gb, t, V7X_LANES), F32),
                   qt_shape, qt_shape, kvt_shape, kvt_shape, kvt_shape, kvt_shape),
        grid=(gb, t // tm),
        in_specs=in_specs,
        out_specs=(tok(3 * CONV_W), tok(V7X_LANES), qt, qt, kvt, kvt, kvt, kvt),
        input_output_aliases=aliases,
        compiler_params=_params("arbitrary", "arbitrary"),
        name="inproj",
    )(*args)


def _key_term_movers():
    row = lax.broadcasted_iota(jnp.int32, (V7X_LANES, V7X_LANES), 0)
    col = lax.broadcasted_iota(jnp.int32, (V7X_LANES, V7X_LANES), 1)
    movers = []
    for h in range(N_HEADS):
        m = jnp.zeros((V7X_LANES, V7X_LANES), F32)
        for k in range(3):
            m = jnp.where(row == h + V7X_SUBLANES * k, jnp.where(col == AUG_KEY_TERM + k, 1.0, m), m)
        movers.append(m.astype(BF16))
    return movers


def _pack_key_terms(term, lane):
    hi, mid, lo = _split3(jnp.where(lane < N_HEADS, term, 0.0))
    return (hi + pltpu.roll(mid, V7X_SUBLANES, axis=1) + pltpu.roll(lo, 2 * V7X_SUBLANES, axis=1)).astype(BF16)


def _augment_keys(k_tok, packed_terms, mover, lane):
    x = jnp.where(lane < HEAD_DIM, jnp.concatenate([k_tok, jnp.zeros_like(k_tok)], axis=1),
                  jnp.where(lane < AUG_QUERY_TERM, _dot(packed_terms, mover), jnp.where(lane < AUG_END, 1.0, 0.0)))
    return x.astype(BF16)


def _augment_queries(qt, query_term):
    row = lax.broadcasted_iota(jnp.int32, qt.shape, 0)
    tail = jnp.where(row < AUG_QUERY_TERM - HEAD_DIM, 1.0, 0.0)
    if query_term is not None:
        for k, piece in enumerate(_split3(query_term)):
            tail = jnp.where(row == AUG_QUERY_TERM - HEAD_DIM + k, piece, tail)
    return jnp.concatenate([qt, tail], axis=0).astype(BF16)


def _flash_heads(i, tq, ka_ref, vb_ref, qa_ref, s_ref, p_ref, acc_ref, past_bias):
    key = lax.broadcasted_iota(jnp.int32, (tq, tq), 0)
    qry = lax.broadcasted_iota(jnp.int32, (tq, tq), 1)
    causal = key <= qry

    def step(tiles, carry):
        for t, (j, diag) in enumerate(tiles):
            for h in range(N_HEADS):
                s = _dot(ka_ref[j, h], qa_ref[h])
                if diag:
                    s = jnp.where(causal, s, -jnp.inf)
                elif past_bias is not None:
                    s = s + past_bias(h, j)
                s_ref[t, h] = s
        out, alphas = [], []
        for h in range(N_HEADS):
            m, l = carry[h]
            scores = [s_ref[t, h] for t in range(len(tiles))]
            m_new = m
            for s in scores:
                m_new = jnp.maximum(m_new, jnp.max(s, axis=0, keepdims=True))
            alpha = jnp.exp2(m - m_new)
            l = alpha * l
            for t, s in enumerate(scores):
                p = jnp.exp2(s - m_new)
                p_ref[t, h] = p.astype(BF16)
                l = l + jnp.sum(p, axis=0, keepdims=True)
            out.append((m_new, l))
            alphas.append(alpha)
        for h in range(N_HEADS):
            acc = alphas[h] * acc_ref[h]
            for t, (j, _) in enumerate(tiles):
                acc = acc + _dot(vb_ref[j, h], p_ref[t, h])
            acc_ref[h] = acc
        return tuple(out)

    acc_ref[...] = jnp.zeros(acc_ref.shape, F32)
    init = tuple((jnp.full((1, tq), -jnp.inf, F32), jnp.zeros((1, tq), F32)) for _ in range(N_HEADS))
    carry = lax.fori_loop(0, i // 2, lambda jj, c: step([(2 * jj, False), (2 * jj + 1, False)], c), init)
    carry = lax.cond(i % 2 == 1, lambda c: step([(i - 1, False), (i, True)], c),
                     lambda c: step([(i, True)], c), carry)
    return [(acc_ref[h] / carry[h][1]).T for h in range(N_HEADS)]


def _fox_prompt_kernel(qt_ref, kt_ref, vt_ref, lf_ref, o_ref, ka_ref, vb_ref, *work, tq, nkt):
    i = pl.program_id(1)

    @pl.when(i == 0)
    def _prepare_sequence():
        rows = lax.broadcasted_iota(jnp.int32, (tq, tq), 0)
        cols = lax.broadcasted_iota(jnp.int32, (tq, tq), 1)
        lower = (cols <= rows).astype(F32)
        lane = lax.broadcasted_iota(jnp.int32, (tq, V7X_LANES), 1)
        movers = _key_term_movers()
        carry = jnp.zeros((1, V7X_LANES), F32)
        terms = []
        for c in range(nkt):
            sl = slice(c * tq, (c + 1) * tq)
            cum = _dot_exact(lower, lf_ref[0, sl, :]) + carry
            carry = cum[tq - 1:tq, :]
            vb_ref[c] = vt_ref[0, :, :, sl].astype(BF16)
            terms.append(_pack_key_terms(-LOG2E * cum, lane))
        for h in range(N_HEADS):
            for c in range(nkt):
                ka_ref[c, h] = _augment_keys(kt_ref[0, h, :, c * tq:(c + 1) * tq].T, terms[c], movers[h], lane)

    qt = qt_ref[0] * (ATTN_SCALE * LOG2E)
    qa_ref = work[0]
    for h in range(N_HEADS):
        qa_ref[h] = _augment_queries(qt[h], None)
    heads = _flash_heads(i, tq, ka_ref, vb_ref, *work, None)
    o_ref[0] = jnp.concatenate(heads, axis=1).astype(o_ref.dtype)


def _moba_prompt_kernel(qt_ref, kt_ref, vt_ref, o_ref, ka_ref, vb_ref, means_ref, *work, nb):
    blk = MOBA_BLOCK
    i = pl.program_id(1)

    @pl.when(i == 0)
    def _prepare_sequence():
        lane = lax.broadcasted_iota(jnp.int32, (blk, V7X_LANES), 1)
        pos = lax.broadcasted_iota(jnp.int32, (blk, V7X_LANES), 0)
        brow = lax.broadcasted_iota(jnp.int32, (V7X_SUBLANES, HEAD_DIM), 0)
        means = [jnp.zeros((V7X_SUBLANES, HEAD_DIM), F32) for _ in range(N_HEADS)]
        movers = _key_term_movers()
        slopes = jnp.zeros((1, V7X_LANES), F32)
        for h in range(N_HEADS):
            slopes = jnp.where(lane[0:1] == h, LOG2E * ALIBI_SLOPES[h], slopes)
        terms = [_pack_key_terms((pos + c * blk).astype(F32) * slopes, lane) for c in range(nb)]
        for c in range(nb):
            vb_ref[c] = vt_ref[0, :, :, c * blk:(c + 1) * blk].astype(BF16)
        for h in range(N_HEADS):
            for c in range(nb):
                k_tok = kt_ref[0, h, :, c * blk:(c + 1) * blk].T
                means[h] = jnp.where(brow == c, jnp.sum(k_tok, axis=0, keepdims=True) * (1.0 / blk), means[h])
                ka_ref[c, h] = _augment_keys(k_tok, terms[c], movers[h], lane)
        for h in range(N_HEADS):
            means_ref[h] = means[h]

    q_raw = qt_ref[0]
    brow = lax.broadcasted_iota(jnp.int32, (V7X_SUBLANES, blk), 0)
    brow_f = brow.astype(F32)
    qa_ref = work[0]
    biases = []
    for h in range(N_HEADS):
        gate = jnp.where(brow < i, _dot_exact(means_ref[h], q_raw[h]), -jnp.inf)
        bias = jnp.full(gate.shape, MASKED, F32)
        for _ in range(MOBA_TOPK):
            best = jnp.max(gate, axis=0, keepdims=True)
            first = jnp.min(jnp.where(gate == best, brow_f, float(V7X_SUBLANES)), axis=0, keepdims=True)
            hit = brow_f == first
            bias = jnp.where(hit, jnp.where(best > -jnp.inf, 0.0, bias), bias)
            gate = jnp.where(hit, -jnp.inf, gate)
        biases.append(bias)
        tile_term = jnp.full((1, blk), -LOG2E * ALIBI_SLOPES[h], F32) * (i * blk).astype(F32)
        qa_ref[h] = _augment_queries(q_raw[h] * (ATTN_SCALE * LOG2E), tile_term)

    def past_bias(h, j):
        return jnp.sum(jnp.where(brow == j, biases[h], 0.0), axis=0, keepdims=True)

    heads = _flash_heads(i, blk, ka_ref, vb_ref, *work, past_bias)
    o_ref[0] = jnp.concatenate(heads, axis=1).astype(o_ref.dtype)


def _prompt_attention(kind, layer, qt, kt, vt, lf=None):
    g, _, _, t = qt.shape
    tq = MOBA_BLOCK
    nkt = t // tq
    assert nkt <= V7X_SUBLANES, "block gates are kept on one sublane group"
    qspec = pl.BlockSpec((1, N_HEADS, HEAD_DIM, tq), lambda b, i: (b, 0, 0, i))
    kvspec = pl.BlockSpec((None, 1, N_HEADS, HEAD_DIM, t), lambda b, i: (layer, b, 0, 0, 0))
    scratch = [pltpu.VMEM((nkt, N_HEADS, tq, V7X_LANES), BF16), pltpu.VMEM((nkt, N_HEADS, HEAD_DIM, tq), BF16)]
    work = [pltpu.VMEM((N_HEADS, V7X_LANES, tq), BF16), pltpu.VMEM((2, N_HEADS, tq, tq), F32),
            pltpu.VMEM((2, N_HEADS, tq, tq), BF16), pltpu.VMEM((N_HEADS, HEAD_DIM, tq), F32)]
    if kind == "fox":
        body = functools.partial(_fox_prompt_kernel, tq=tq, nkt=nkt)
        in_specs = [qspec, kvspec, kvspec, pl.BlockSpec((1, t, V7X_LANES), lambda b, i: (b, 0, 0))]
        args = (qt, kt, vt, lf)
    else:
        body = functools.partial(_moba_prompt_kernel, nb=nkt)
        in_specs = [qspec, kvspec, kvspec]
        args = (qt, kt, vt)
        scratch.append(pltpu.VMEM((N_HEADS, V7X_SUBLANES, HEAD_DIM), F32))
    return pl.pallas_call(
        body,
        out_shape=jax.ShapeDtypeStruct((g, t, HEADS_W), BF16),
        grid=(g, nkt),
        in_specs=in_specs,
        out_specs=pl.BlockSpec((1, tq, HEADS_W), lambda b, i: (b, i, 0)),
        scratch_shapes=scratch + work,
        compiler_params=_params("arbitrary", "arbitrary"),
        name=kind + "_prompt",
    )(*args)


def _page_copies(pt_ref, caches, bufs, sems, layer, g, s, slot, pages):
    return [pltpu.make_async_copy(cache.at[layer, pt_ref[g, s * pages + r]], buf.at[slot, r], sems.at[slot, a])
            for r in range(pages) for a, (cache, buf) in enumerate(zip(caches, bufs))]


def _stream_pages(pt_ref, caches, bufs, sems, layer, pages, n_steps):
    g, s = pl.program_id(0), pl.program_id(1)
    step = g * n_steps + s
    slot = step & 1
    last = pl.num_programs(0) * n_steps - 1

    @pl.when(step == 0)
    def _first():
        for c in _page_copies(pt_ref, caches, bufs, sems, layer, g, s, slot, pages):
            c.start()

    @pl.when(step < last)
    def _prefetch():
        wrap = s == n_steps - 1
        g_next = jnp.where(wrap, g + 1, g)
        s_next = jnp.where(wrap, 0, s + 1)
        for c in _page_copies(pt_ref, caches, bufs, sems, layer, g_next, s_next, 1 - slot, pages):
            c.start()

    for c in _page_copies(pt_ref, caches, bufs, sems, layer, g, s, slot, pages):
        c.wait()
    return slot


def _all_heads(page_ref):
    return page_ref[...].reshape(HEADS_W, page_ref.shape[-1])


def _own_tokens(xt_ref, t_new):
    x = _all_heads(xt_ref)
    n = x.shape[1]
    return pltpu.roll(x, lax.rem(n - pl.program_id(0) * t_new, n), axis=1)[:, :PAGE]


def _block_diag_queries(qt_ref, t_new, scale):
    q_tok = (_own_tokens(qt_ref, t_new) * scale).T[:t_new]
    z = jnp.zeros((t_new, HEAD_DIM), F32)
    return jnp.concatenate(
        [jnp.concatenate([q_tok[:, h * HEAD_DIM:(h + 1) * HEAD_DIM] if c == h else z for c in range(N_HEADS)], axis=1)
         for h in range(N_HEADS)], axis=0)


def _own_head_columns(x, t):
    lane = lax.broadcasted_iota(jnp.int32, (t, HEADS_W), 1)
    out = jnp.zeros((t, HEADS_W), x.dtype)
    for h in range(N_HEADS):
        own = (lane >= h * HEAD_DIM) & (lane < (h + 1) * HEAD_DIM)
        out = jnp.where(own, x[h * t:(h + 1) * t], out)
    return out


def _per_head_rows(values, t):
    row = lax.broadcasted_iota(jnp.int32, (N_HEADS * t, 1), 0)
    out = jnp.zeros((N_HEADS * t, 1), F32)
    for h, v in enumerate(values):
        out = jnp.where((row >= h * t) & (row < (h + 1) * t), v, out)
    return out


def _fox_sample_kernel(pt_ref, q_ref, knt_ref, vnt_ref, lfn_ref, ck_ref, cv_ref, cl_ref, o_ref,
                       qbd_ref, m_ref, l_ref, acc_ref, carry_ref, kbuf, vbuf, lbuf, sems, *,
                       layer, pages, n_steps, t_new):
    slot = _stream_pages(pt_ref, (ck_ref, cv_ref, cl_ref), (kbuf, vbuf, lbuf), sems, layer, pages, n_steps)
    kp, vp, lp = ([buf.at[slot, r] for r in range(pages)] for buf in (kbuf, vbuf, lbuf))
    s = pl.program_id(1)
    n_rows = N_HEADS * t_new

    @pl.when(s == 0)
    def _init():
        qbd_ref[...] = _block_diag_queries(q_ref, t_new, ATTN_SCALE).astype(BF16)
        m_ref[...] = jnp.full(m_ref.shape, -jnp.inf, F32)
        l_ref[...] = jnp.zeros(l_ref.shape, F32)
        acc_ref[...] = jnp.zeros(acc_ref.shape, F32)
        carry_ref[...] = jnp.zeros(carry_ref.shape, F32)

    m, l, acc = m_ref[:, 0:1], l_ref[:, 0:1], acc_ref[...]

    n_rows_lf = pages * V7X_SUBLANES
    lf = jnp.concatenate([lp[r][...] for r in range(pages)], axis=0)
    ri = lax.broadcasted_iota(jnp.int32, (PAGE, PAGE), 0)
    ci = lax.broadcasted_iota(jnp.int32, (PAGE, PAGE), 1)
    upper = jnp.where(ri <= ci, 1.0, 0.0).astype(BF16)
    pi = lax.broadcasted_iota(jnp.int32, (n_rows_lf, n_rows_lf), 0)
    pj = lax.broadcasted_iota(jnp.int32, (n_rows_lf, n_rows_lf), 1)
    same_head = (pi & (V7X_SUBLANES - 1)) == (pj & (V7X_SUBLANES - 1))
    earlier_page = jnp.where(same_head, jnp.where(pj < (pi & -V7X_SUBLANES), 1.0, 0.0), 0.0).astype(BF16)
    any_page = jnp.where(same_head, 1.0, 0.0).astype(BF16)
    lf_pieces = [x.astype(BF16) for x in _split3(lf)]
    totals = sum(_dot(x, jnp.ones((PAGE, PAGE), BF16)) for x in lf_pieces)
    total_pieces = [x.astype(BF16) for x in _split3(totals)]
    carry = carry_ref[...]
    cum = (sum(_dot(x, upper) for x in lf_pieces) + sum(_dot(earlier_page, x) for x in total_pieces)) + carry
    carry = carry + sum(_dot(any_page, x) for x in total_pieces)
    key_gate = jnp.concatenate(
        [jnp.concatenate([jnp.broadcast_to(cum[r * V7X_SUBLANES + h:r * V7X_SUBLANES + h + 1, :], (t_new, PAGE))
                          for r in range(pages)], axis=1) for h in range(N_HEADS)], axis=0)

    def update(scores, vt, m, l, acc):
        m_new = jnp.maximum(m, jnp.max(scores, axis=1, keepdims=True))
        alpha = jnp.exp(m - m_new)
        p = jnp.exp(scores - m_new)
        return m_new, alpha * l + jnp.sum(p, axis=1, keepdims=True), alpha * acc + _dot_nt(p.astype(BF16), vt)

    kt = jnp.concatenate([_all_heads(kp[r]) for r in range(pages)], axis=1).astype(BF16)
    vt = jnp.concatenate([_all_heads(vp[r]) for r in range(pages)], axis=1).astype(BF16)
    m, l, acc = update(_dot(qbd_ref[...], kt) - key_gate, vt, m, l, acc)

    carry_ref[...] = carry
    m_ref[...] = jnp.broadcast_to(m, m_ref.shape)
    l_ref[...] = jnp.broadcast_to(l, l_ref.shape)
    acc_ref[...] = acc

    @pl.when(s == n_steps - 1)
    def _new_tokens():
        lane = lax.broadcasted_iota(jnp.int32, (n_rows, PAGE), 1)
        qi = lax.broadcasted_iota(jnp.int32, (n_rows, PAGE), 0) & (t_new - 1)
        cum_n = sum(_dot(x.astype(BF16), upper) for x in _split3(lfn_ref[0])) + carry[:V7X_SUBLANES]
        gate_n = jnp.concatenate([jnp.broadcast_to(cum_n[h:h + 1, :], (t_new, PAGE)) for h in range(N_HEADS)], axis=0)
        scores = jnp.where(lane <= qi, _dot(qbd_ref[...], _own_tokens(knt_ref, t_new).astype(BF16)) - gate_n, -jnp.inf)
        _, l_fin, acc_fin = update(scores, _own_tokens(vnt_ref, t_new).astype(BF16), m, l, acc)
        o_ref[0] = _own_head_columns(acc_fin / l_fin, t_new).astype(o_ref.dtype)


def _moba_sample_kernel(pt_ref, q_ref, knt_ref, vnt_ref, ck_ref, cv_ref, o_ref, qbd_ref, q32_ref, means_ref,
                        mb_ref, lb_ref, accb_ref, kbuf, vbuf, sems, *, layer, pages, n_steps, p_len, t_new):
    slot = _stream_pages(pt_ref, (ck_ref, cv_ref), (kbuf, vbuf), sems, layer, pages, n_steps)
    kp, vp = ([buf.at[slot, r] for r in range(pages)] for buf in (kbuf, vbuf))
    blk = MOBA_BLOCK
    nb = p_len // blk
    per_step = pages * PAGE // blk
    s = pl.program_id(1)
    n_rows = N_HEADS * t_new

    @pl.when(s == 0)
    def _init():
        q32_ref[...] = _block_diag_queries(q_ref, t_new, 1.0)
        qbd_ref[...] = _block_diag_queries(q_ref, t_new, ATTN_SCALE).astype(BF16)
        means_ref[...] = jnp.zeros(means_ref.shape, F32)
        mb_ref[...] = jnp.full(mb_ref.shape, -jnp.inf, F32)
        lb_ref[...] = jnp.zeros(lb_ref.shape, F32)

    means, mb, lb = means_ref[...], mb_ref[...], lb_ref[...]
    lane_q = lax.broadcasted_iota(jnp.int32, (n_rows, V7X_LANES), 1)
    lane_d = lax.broadcasted_iota(jnp.int32, (HEADS_W, V7X_LANES), 1)
    kcol = lax.broadcasted_iota(jnp.int32, (1, blk), 1).astype(F32)
    slope = _per_head_rows(ALIBI_SLOPES, t_new)

    k32 = jnp.concatenate([_all_heads(kp[r]) for r in range(pages)], axis=1)
    scores = _dot(qbd_ref[...], k32.astype(BF16))
    probs = []
    for bb in range(per_step):
        b = s * per_step + bb
        cols = slice(bb * blk, (bb + 1) * blk)
        means = jnp.where(lane_d == b, jnp.sum(k32[:, cols], axis=1, keepdims=True) * (1.0 / blk), means)
        sc = scores[:, cols] + slope * (kcol + (b * blk - p_len).astype(F32))
        mx = jnp.max(sc, axis=1, keepdims=True)
        p = jnp.exp(sc - mx)
        probs.append(p.astype(BF16))
        mb = jnp.where(lane_q == b, mx, mb)
        lb = jnp.where(lane_q == b, jnp.sum(p, axis=1, keepdims=True), lb)
    ppb = blk // PAGE
    for bb in range(per_step):
        vblk = jnp.concatenate([_all_heads(vp[bb * ppb + r]) for r in range(ppb)], axis=1).astype(BF16)
        accb_ref[s * per_step + bb] = _dot_nt(probs[bb], vblk)

    means_ref[...] = means
    mb_ref[...] = mb
    lb_ref[...] = lb

    @pl.when(s == n_steps - 1)
    def _select_and_combine():
        lane_f = lane_q.astype(F32)
        qi = lax.broadcasted_iota(jnp.int32, (n_rows, V7X_LANES), 0) & (t_new - 1)
        gate = jnp.where(lane_q < nb, _dot_exact(q32_ref[...], means), -jnp.inf)
        sel = jnp.zeros(gate.shape, F32)
        for _ in range(MOBA_TOPK):
            best = jnp.max(gate, axis=1, keepdims=True)
            first = jnp.min(jnp.where(gate == best, lane_f, float(V7X_LANES)), axis=1, keepdims=True)
            hit = lane_f == first
            sel = jnp.where(hit, jnp.where(best > -jnp.inf, 1.0, sel), sel)
            gate = jnp.where(hit, -jnp.inf, gate)
        sc = _dot(qbd_ref[...], _own_tokens(knt_ref, t_new).astype(BF16)) + slope * lane_f
        sc = jnp.where(lane_q <= qi, sc, -jnp.inf)
        m_own = jnp.max(sc, axis=1, keepdims=True)
        p_own = jnp.exp(sc - m_own)
        l_own = jnp.sum(p_own, axis=1, keepdims=True)
        acc_own = _dot_nt(p_own.astype(BF16), _own_tokens(vnt_ref, t_new).astype(BF16))
        picked = sel > 0.0
        m_all = jnp.maximum(m_own, jnp.max(jnp.where(picked, mb, -jnp.inf), axis=1, keepdims=True))
        w = jnp.where(picked, jnp.exp(mb - m_all), 0.0)
        w_own = jnp.exp(m_own - m_all)
        den = jnp.sum(w * lb, axis=1, keepdims=True) + w_own * l_own
        num = w_own * acc_own
        for b in range(nb):
            num = num + jnp.sum(jnp.where(lane_q == b, w, 0.0), axis=1, keepdims=True) * accb_ref[b]
        o_ref[0] = _own_head_columns(num / den, t_new).astype(o_ref.dtype)


def _sample_attention(kind, layer, g, t_new, page_table, qt, knt, vnt, cache_kt, cache_vt, lfn=None, cache_lft=None):
    n_pages = page_table.shape[1]
    pages = min(PAGES_PER_STEP, n_pages)
    n_steps = n_pages // pages
    n_rows = N_HEADS * t_new
    n_tok = g * t_new
    assert t_new & (t_new - 1) == 0 and pages * PAGE % MOBA_BLOCK == 0 and n_tok % V7X_LANES == 0
    per_seq = lambda blk: pl.BlockSpec(blk, lambda b, s, pt: (b,) + (0,) * (len(blk) - 1))
    in_hbm = pl.BlockSpec(memory_space=pl.ANY)
    new_kv = pl.BlockSpec((None, None, N_HEADS, HEAD_DIM, n_tok), lambda b, s, pt: (layer, 0, 0, 0, 0))
    in_specs = [pl.BlockSpec((None, N_HEADS, HEAD_DIM, n_tok), lambda b, s, pt: (0, 0, 0, 0)), new_kv, new_kv]
    args = [qt, knt, vnt]
    if kind == "fox":
        in_specs.append(per_seq((1, V7X_SUBLANES, PAGE)))
        args.append(lfn)
    in_specs += [in_hbm, in_hbm]
    args += [cache_kt, cache_vt]
    stat = pltpu.VMEM((n_rows, V7X_LANES), F32)
    q_bd = pltpu.VMEM((n_rows, HEADS_W), BF16)
    page_buf = pltpu.VMEM((2, pages, N_HEADS, HEAD_DIM, PAGE), F32)
    if kind == "fox":
        in_specs.append(in_hbm)
        args.append(cache_lft)
        body = functools.partial(_fox_sample_kernel, layer=layer, pages=pages, n_steps=n_steps, t_new=t_new)
        scratch = [q_bd, stat, stat, pltpu.VMEM((n_rows, HEADS_W), F32),
                   pltpu.VMEM((pages * V7X_SUBLANES, V7X_LANES), F32),
                   page_buf, page_buf, pltpu.VMEM((2, pages, V7X_SUBLANES, PAGE), F32),
                   pltpu.SemaphoreType.DMA((2, 3))]
    else:
        p_len = n_pages * PAGE
        assert p_len // MOBA_BLOCK <= V7X_LANES
        body = functools.partial(_moba_sample_kernel, layer=layer, pages=pages, n_steps=n_steps, p_len=p_len,
                                 t_new=t_new)
        scratch = [q_bd, pltpu.VMEM((n_rows, HEADS_W), F32), pltpu.VMEM((HEADS_W, V7X_LANES), F32), stat, stat,
                   pltpu.VMEM((p_len // MOBA_BLOCK, n_rows, HEADS_W), F32),
                   page_buf, page_buf, pltpu.SemaphoreType.DMA((2, 2))]
    return pl.pallas_call(
        body,
        out_shape=jax.ShapeDtypeStruct((g, t_new, HEADS_W), BF16),
        grid_spec=pltpu.PrefetchScalarGridSpec(
            num_scalar_prefetch=1,
            grid=(g, n_steps),
            in_specs=in_specs,
            out_specs=pl.BlockSpec((1, t_new, HEADS_W), lambda b, s, pt: (b, 0, 0)),
            scratch_shapes=scratch),
        compiler_params=_params("arbitrary", "arbitrary"),
        name=kind + "_sample",
    )(page_table, *args)


def _causal_shifts(u, fill, period):
    out = []
    for j in (1, 2):
        r = pltpu.roll(u, j, axis=0)
        f = pltpu.roll(fill, j, axis=0)
        if period is None:
            row = lax.broadcasted_iota(jnp.int32, f.shape, 0)
            out.append(jnp.concatenate([jnp.where(row >= j, r[:V7X_SUBLANES], f), r[V7X_SUBLANES:]], axis=0))
        else:
            t = lax.broadcasted_iota(jnp.int32, u.shape, 0) & (period - 1)
            out.append(jnp.where(t >= j, r, f))
    return out


def _conv3(u, fill, period, w_ref, cols):
    s1, s2 = _causal_shifts(u, fill, period)
    return w_ref[0:1, cols] * s2 + w_ref[1:2, cols] * s1 + w_ref[2:3, cols] * u


def _outproj_kernel(x_ref, pc_ref, yf_ref, ym_ref, gt_ref, cw_ref, wo_ref, g_ref, b_ref, *rest,
                    period, dn_alpha):
    if period is None:
        o_ref, us_ref, carry_ref = rest

        @pl.when(pl.program_id(1) == 0)
        def _fresh_sequence():
            carry_ref[...] = jnp.zeros(carry_ref.shape, F32)

        fill = carry_ref[...]
    else:
        fill_ref, o_ref, us_ref = rest
        fill = fill_ref[0]
    pc = pc_ref[0]
    u = pc[:, CONV_W:2 * CONV_W] * pc[:, 2 * CONV_W:]
    y_conv = pc[:, :CONV_W] * _conv3(u, fill, period, cw_ref, slice(None))
    mixed = _dot(jnp.concatenate([y_conv.astype(BF16), yf_ref[0], ym_ref[0]], axis=1), wo_ref[...])
    z = dn_alpha * x_ref[0] + (1.0 + gt_ref[0]) * mixed
    o_ref[0] = _layer_norm(z, g_ref[...], b_ref[...])
    if period is None:
        tail = u[u.shape[0] - V7X_SUBLANES:]
        carry_ref[...] = tail
        us_ref[0] = tail
    else:
        us_ref[0] = u


def _outproj(x3, pconv, yf, ym, gt, conv_w, wo, ln_g, ln_b, fill, tm, dn_alpha):
    gb, t, d = x3.shape
    period = None if fill is None else V7X_SUBLANES
    tok = lambda w: pl.BlockSpec((1, tm, w), lambda g, i: (g, i, 0))
    in_specs = [tok(d), tok(3 * CONV_W), tok(HEADS_W), tok(HEADS_W), _mod_spec(gt, tm),
                _const_spec(conv_w.shape), _const_spec(wo.shape), _const_spec(ln_g.shape), _const_spec(ln_b.shape)]
    args = [x3, pconv, yf, ym, gt, conv_w, wo, ln_g, ln_b]
    scratch = []
    if period is None:
        us_rows = V7X_SUBLANES
        us_spec = pl.BlockSpec((1, us_rows, CONV_W), lambda g, i: (g, 0, 0))
        scratch.append(pltpu.VMEM((V7X_SUBLANES, CONV_W), F32))
    else:
        us_rows = t
        us_spec = tok(CONV_W)
        in_specs.append(tok(CONV_W))
        args.append(fill)
    return pl.pallas_call(
        functools.partial(_outproj_kernel, period=period, dn_alpha=dn_alpha),
        out_shape=(jax.ShapeDtypeStruct((gb, t, d), F32), jax.ShapeDtypeStruct((gb, us_rows, CONV_W), F32)),
        grid=(gb, t // tm),
        in_specs=in_specs,
        out_specs=(tok(d), us_spec),
        scratch_shapes=scratch,
        compiler_params=_params("arbitrary", "arbitrary"),
        name="outproj",
    )(*args)


def _ffn_kernel(x_ref, sc_ref, sh_ref, gt_ref, wup_ref, cw_ref, cb_ref, wdn_ref, g_ref, b_ref, *rest,
                period, dn_alpha):
    if period is None:
        o_ref, us_ref, carry_ref = rest

        @pl.when(pl.program_id(1) == 0)
        def _fresh_sequence():
            carry_ref[...] = jnp.zeros(carry_ref.shape, F32)
    else:
        fill_ref, o_ref, us_ref = rest
    x = x_ref[0]
    tm = x.shape[0]
    d_ff = wdn_ref.shape[0]
    h = (x * (1.0 + sc_ref[0]) + sh_ref[0]).astype(BF16)
    halves = []
    for half in range(2):
        cols = slice(half * d_ff, (half + 1) * d_ff)
        u = _dot(h, wup_ref[:, cols])
        fill = carry_ref[:, cols] if period is None else fill_ref[0, :, cols]
        halves.append(_conv3(u, fill, period, cw_ref, cols) + cb_ref[:, cols])
        if period is None:
            tail = u[tm - V7X_SUBLANES:]
            carry_ref[:, cols] = tail
            us_ref[0, :, cols] = tail
        else:
            us_ref[0, :, cols] = u
    gate, val = halves
    y = _dot((gate * _sigmoid(gate) * val).astype(BF16), wdn_ref[...])
    z = dn_alpha * x + (1.0 + gt_ref[0]) * y
    o_ref[0] = _layer_norm(z, g_ref[...], b_ref[...])


def _ffn(x3, sc, sh, gt, wup, conv_w, conv_b, wdn, ln_g, ln_b, fill, tm, dn_alpha):
    gb, t, d = x3.shape
    n_up = wup.shape[1]
    period = None if fill is None else V7X_SUBLANES
    tok = lambda w: pl.BlockSpec((1, tm, w), lambda g, i: (g, i, 0))
    in_specs = [tok(d), _mod_spec(sc, tm), _mod_spec(sh, tm), _mod_spec(gt, tm),
                _const_spec(wup.shape), _const_spec(conv_w.shape), _const_spec(conv_b.shape),
                _const_spec(wdn.shape), _const_spec(ln_g.shape), _const_spec(ln_b.shape)]
    args = [x3, sc, sh, gt, wup, conv_w, conv_b, wdn, ln_g, ln_b]
    scratch = []
    if period is None:
        us_rows = V7X_SUBLANES
        us_spec = pl.BlockSpec((1, us_rows, n_up), lambda g, i: (g, 0, 0))
        scratch.append(pltpu.VMEM((V7X_SUBLANES, n_up), F32))
    else:
        us_rows = t
        us_spec = tok(n_up)
        in_specs.append(tok(n_up))
        args.append(fill)
    return pl.pallas_call(
        functools.partial(_ffn_kernel, period=period, dn_alpha=dn_alpha),
        out_shape=(jax.ShapeDtypeStruct((gb, t, d), F32), jax.ShapeDtypeStruct((gb, us_rows, n_up), F32)),
        grid=(gb, t // tm),
        in_specs=in_specs,
        out_specs=(tok(d), us_spec),
        scratch_shapes=scratch,
        compiler_params=_params("arbitrary", "arbitrary"),
        name="ffn",
    )(*args)


def _seq_fill(state):
    g, w, c = state.shape
    padded = jnp.pad(state, ((0, 0), (V7X_SUBLANES - w, 0), (0, 0)))
    return jnp.roll(padded, -1, axis=0).reshape(1, g * V7X_SUBLANES, c)


def _layer(l, depth, x3, mods, weights, kv_states, dn_alpha, tm, sample=None):
    sh_a, sc_a, gt_a, sh_f, sc_f, gt_f = mods
    (wtok, wqkvt, bf, conv_w, wo, ln1_g, ln1_b, wup, ffn_w, ffn_b, wdn, ln2_g, ln2_b) = weights
    pconv, lf, fqt, mqt, *kv_states = _inproj(l, depth, x3, sc_a, sh_a, wtok, wqkvt, bf, kv_states, tm)
    fkt, fvt, mkt, mvt = kv_states
    if sample is None:
        y_fox = _prompt_attention("fox", l, fqt, fkt, fvt, lf)
        y_moba = _prompt_attention("moba", l, mqt, mkt, mvt)
        conv_fill = ffn_fill = None
    else:
        g, t_new = sample["g"], sample["t"]
        lfn = lf[0].reshape(g, t_new, V7X_LANES)[:, :, :V7X_SUBLANES].transpose(0, 2, 1)
        lfn = jnp.pad(lfn, ((0, 0), (0, 0), (0, PAGE - t_new)))
        y_fox = _sample_attention("fox", l, g, t_new, sample["page_table"], fqt, fkt, fvt,
                                  sample["fox_kt"], sample["fox_vt"], lfn, sample["fox_lft"])
        y_moba = _sample_attention("moba", l, g, t_new, sample["page_table"], mqt, mkt, mvt,
                                   sample["moba_kt"], sample["moba_vt"])
        y_fox = y_fox.reshape(1, g * t_new, HEADS_W)
        y_moba = y_moba.reshape(1, g * t_new, HEADS_W)
        conv_fill = _seq_fill(sample["state_conv"][l])
        ffn_fill = _seq_fill(sample["state_ffn"][l])
    x1, conv_tail = _outproj(x3, pconv, y_fox, y_moba, gt_a, conv_w, wo, ln1_g, ln1_b, conv_fill, tm, dn_alpha)
    x2, ffn_tail = _ffn(x1, sc_f, sh_f, gt_f, wup, ffn_w, ffn_b, wdn, ln2_g, ln2_b, ffn_fill, tm, dn_alpha)
    return x2, kv_states, (lf, conv_tail, ffn_tail)


def kernel(x_prompt, x_sample, cache_fox_k, cache_fox_v, cache_fox_logf, cache_moba_k, cache_moba_v, state_conv, state_ffn, page_table, c_prompt, c_sample, w_ada, b_ada, w_in, b_f, conv_w, w_o, ln1_g, ln1_b, w_up, ffn_conv_w, ffn_conv_b, w_down, ln2_g, ln2_b):
    depth, d, _ = w_ada.shape
    gp, tp, _ = x_prompt.shape
    gs, ts, _ = x_sample.shape
    dn_alpha = (2 * depth) ** 0.25
    n_conv, n_qkv = 3 * CONV_W, 3 * HEADS_W

    ada = _ada(jnp.concatenate([c_prompt, c_sample], axis=0), w_ada, b_ada)
    to_kt = lambda c: jnp.transpose(c, (0, 1, 3, 4, 2))
    lft_cache = jnp.pad(jnp.transpose(cache_fox_logf, (0, 1, 3, 2)),
                        ((0, 0), (0, 0), (0, V7X_SUBLANES - N_HEADS), (0, 0)))
    sample = dict(g=gs, t=ts, page_table=page_table, state_conv=state_conv, state_ffn=state_ffn,
                  fox_kt=to_kt(cache_fox_k), fox_vt=to_kt(cache_fox_v), fox_lft=lft_cache,
                  moba_kt=to_kt(cache_moba_k), moba_vt=to_kt(cache_moba_v))

    xp = x_prompt
    xs = x_sample.reshape(1, gs * ts, d)
    kv_p = kv_s = None
    small_p, small_s = [], []
    for l in range(depth):
        wt = jnp.transpose(w_in, (2, 0, 1))[:, l]
        fox0, ff0, moba0 = n_conv, n_conv + n_qkv, n_conv + n_qkv + N_HEADS
        wtok = jnp.concatenate([wt[:n_conv], jnp.pad(wt[ff0:moba0], ((0, V7X_LANES - N_HEADS), (0, 0)))],
                               axis=0).astype(BF16)
        wqkvt = jnp.concatenate([wt[fox0:ff0], wt[moba0:]], axis=0).astype(BF16)
        bf = jnp.pad(b_f[l], (0, V7X_LANES - N_HEADS))[None]
        weights = (wtok, wqkvt, bf, conv_w[l], w_o[l].astype(BF16), ln1_g[l][None], ln1_b[l][None],
                   w_up[l].astype(BF16), ffn_conv_w[l], ffn_conv_b[l][None], w_down[l].astype(BF16),
                   ln2_g[l][None], ln2_b[l][None])
        mods_p = tuple(a[:, None, :] for a in jnp.split(ada[l, :gp], 6, axis=-1))
        mods_s = tuple(jnp.repeat(a, ts, axis=0)[None] for a in jnp.split(ada[l, gp:], 6, axis=-1))
        xp, kv_p, st_p = _layer(l, depth, xp, mods_p, weights, kv_p, dn_alpha, tm=512)
        xs, kv_s, st_s = _layer(l, depth, xs, mods_s, weights, kv_s, dn_alpha, tm=gs * ts, sample=sample)
        small_p.append(st_p)
        small_s.append(st_s)

    stack = lambda states, i: jnp.stack([s[i] for s in states])
    out = [xp, xs.reshape(gs, ts, d)]
    fk, fv, mk, mv = (jnp.transpose(a, (0, 1, 4, 2, 3)) for a in kv_p)
    out += [fk, fv, stack(small_p, 0)[..., :N_HEADS], mk, mv,
            stack(small_p, 1)[:, :, V7X_SUBLANES - 2:], stack(small_p, 2)[:, :, V7X_SUBLANES - 2:]]
    fk, fv, mk, mv = (jnp.transpose(a[:, 0].reshape(depth, N_HEADS, HEAD_DIM, gs, ts), (0, 3, 4, 1, 2))
                      for a in kv_s)
    tail_s = lambda i: stack(small_s, i).reshape(depth, gs, ts, -1)[:, :, ts - 2:]
    out += [fk, fv, stack(small_s, 0).reshape(depth, gs, ts, V7X_LANES)[..., :N_HEADS], mk, mv,
            tail_s(1), tail_s(2)]
    return tuple(out)
```
